```python
import math
import jax, jax.numpy as jnp
from jax import lax
import numpy as np

D_MODEL = 1024
BATCH = 1
SEQ = 16384
DEPTH = 4

N_MIXERS = 4
N_REP = DEPTH // N_MIXERS
HEAD_DIM = 64
Q_BLOCK = 128
NEG_INF = -1e30
EPS = 1e-6
ROPE_THETA = 500000.0
ROT_DIM = HEAD_DIM // 4
A_HEADS = D_MODEL // (2 * HEAD_DIM)
B_PAIRS = ((128, 1), (512, 4), (2048, 16))
B_HEADS_PER_GROUP = 4
B_HEADS = B_HEADS_PER_GROUP * len(B_PAIRS)
B_WIDTH = B_HEADS * HEAD_DIM
C_HEADS = D_MODEL // HEAD_DIM
C_KV_HEADS = 4
C_WINDOW = 128
D_HEADS = D_MODEL // HEAD_DIM
D_KV_HEADS = 4
GRID_W = 64
AXIAL_THETA = 10000.0
PLE_DIM = 256
MOE_GROUPS = 4
MOE_EXPERTS_PER_GROUP = 4
MOE_EXPERTS = MOE_GROUPS * MOE_EXPERTS_PER_GROUP
MOE_TOP_K = 2
MOE_FF = 512

kernel_name = "hybrid_interleaved_encoder_hmoe"

F32 = jnp.float32


def rms_norm(x, g):
    xf = x.astype(F32)
    y = xf * lax.rsqrt(jnp.mean(xf * xf, axis=-1, keepdims=True) + EPS)
    return (y * g.astype(F32)).astype(x.dtype)


def inv_freq(dim, theta):
    return theta ** (-jnp.arange(0, dim, 2, dtype=F32) / dim)


def rotate(x, ang):
    f = ang.shape[-1]
    cos = jnp.cos(ang)[:, None, :]
    sin = jnp.sin(ang)[:, None, :]
    xf = x.astype(F32)
    x1, x2 = xf[..., :f], xf[..., f:]
    return jnp.concatenate([x1 * cos - x2 * sin, x2 * cos + x1 * sin], axis=-1).astype(x.dtype)


def partial_rope(x, pos):
    ang = pos.astype(F32)[:, None] * inv_freq(ROT_DIM, ROPE_THETA)[None, :]
    return jnp.concatenate([rotate(x[..., :ROT_DIM], ang), x[..., ROT_DIM:]], axis=-1)


def axial_rope(x, row, col):
    half = HEAD_DIM // 2
    fr = inv_freq(half, AXIAL_THETA)
    ang_r = row.astype(F32)[:, None] * fr[None, :]
    ang_c = col.astype(F32)[:, None] * fr[None, :]
    return jnp.concatenate([rotate(x[..., :half], ang_r), rotate(x[..., half:], ang_c)], axis=-1)


def to_blocks(t):
    bn, s = t.shape[0], t.shape[1]
    return jnp.moveaxis(t.reshape(bn, s // Q_BLOCK, Q_BLOCK, *t.shape[2:]), 1, 0)


def from_blocks(t):
    nb, bn, qb = t.shape[0], t.shape[1], t.shape[2]
    return jnp.moveaxis(t, 0, 1).reshape(bn, nb * qb, *t.shape[3:])


def mixer_diff_attn(hn, w_in, w_out, lam_q1, lam_k1, lam_q2, lam_k2, subln, lam_init, pos):
    bn, s, _ = hn.shape
    q, k, v = jnp.split(hn @ w_in, 3, axis=-1)
    q = partial_rope(q.reshape(bn, s, 2 * A_HEADS, HEAD_DIM), pos).reshape(bn, s, A_HEADS, 2, HEAD_DIM)
    k = partial_rope(k.reshape(bn, s, 2 * A_HEADS, HEAD_DIM), pos).reshape(bn, s, A_HEADS, 2, HEAD_DIM)
    v = v.reshape(bn, s, A_HEADS, 2 * HEAD_DIM).astype(F32)
    lam = (jnp.exp(jnp.sum(lam_q1.astype(F32) * lam_k1.astype(F32)))
           - jnp.exp(jnp.sum(lam_q2.astype(F32) * lam_k2.astype(F32))) + lam_init)
    scale = HEAD_DIM ** -0.5

    def block(qb):
        sc = jnp.einsum('bqhcd,bshcd->bhcqs', qb, k, preferred_element_type=F32) * scale
        a = jax.nn.softmax(sc, axis=-1)
        w = a[:, :, 0] - lam * a[:, :, 1]
        return jnp.einsum('bhqs,bshe->bqhe', w, v)

    o = from_blocks(lax.map(block, to_blocks(q)))
    o = rms_norm(o, subln) * (1.0 - lam_init)
    return o.reshape(bn, s, 2 * A_HEADS * HEAD_DIM).astype(hn.dtype) @ w_out


def mixer_dilated(hn, w_in, w_out, pos):
    bn, s, _ = hn.shape
    q, k, v = jnp.split(hn @ w_in, 3, axis=-1)
    q = partial_rope(q.reshape(bn, s, B_HEADS, HEAD_DIM), pos)
    k = partial_rope(k.reshape(bn, s, B_HEADS, HEAD_DIM), pos)
    v = v.reshape(bn, s, B_HEADS, HEAD_DIM)
    hg = B_HEADS_PER_GROUP
    k_groups = [k[:, :, g * hg:(g + 1) * hg] for g in range(len(B_PAIRS))]
    v_groups = [v[:, :, g * hg:(g + 1) * hg] for g in range(len(B_PAIRS))]
    scale = HEAD_DIM ** -0.5

    def block(args):
        qb, b = args
        qpos = b * Q_BLOCK + jnp.arange(Q_BLOCK)
        outs, lses = [], []
        for g, (win, dil) in enumerate(B_PAIRS):
            half = win // (2 * dil)
            offs = jnp.arange(-half, half + 1) * dil
            kidx = qpos[:, None] + offs[None, :]
            valid = (kidx >= 0) & (kidx < s)
            kidx = jnp.clip(kidx, 0, s - 1)
            kg = k_groups[g][:, kidx]
            vg = v_groups[g][:, kidx].astype(F32)
            qg = qb[:, :, g * hg:(g + 1) * hg]
            sc = jnp.einsum('bqhd,bqjhd->bhqj', qg, kg, preferred_element_type=F32) * scale
            sc = jnp.where(valid[None, None], sc, NEG_INF)
            lse = jax.nn.logsumexp(sc, axis=-1)
            pr = jnp.exp(sc - lse[..., None])
            outs.append(jnp.einsum('bhqj,bqjhd->bqhd', pr, vg))
            lses.append(lse)
        alpha = jax.nn.softmax(jnp.stack(lses, axis=0), axis=0)
        alpha = jnp.swapaxes(alpha, 2, 3)[..., None]
        return jnp.concatenate([alpha[g] * outs[g] for g in range(len(B_PAIRS))], axis=2)

    o = from_blocks(lax.map(block, (to_blocks(q), jnp.arange(s // Q_BLOCK))))
    return o.reshape(bn, s, B_WIDTH).astype(hn.dtype) @ w_out


def mixer_window_sink(hn, w_in, w_out, sink, pos):
    bn, s, _ = hn.shape
    qd, kvd = C_HEADS * HEAD_DIM, C_KV_HEADS * HEAD_DIM
    grp = C_HEADS // C_KV_HEADS
    proj = hn @ w_in
    q = partial_rope(proj[..., :qd].reshape(bn, s, C_HEADS, HEAD_DIM), pos).reshape(bn, s, C_KV_HEADS, grp, HEAD_DIM)
    k = partial_rope(proj[..., qd:qd + kvd].reshape(bn, s, C_KV_HEADS, HEAD_DIM), pos)
    v = proj[..., qd + kvd:].reshape(bn, s, C_KV_HEADS, HEAD_DIM)
    pad = ((0, 0), (C_WINDOW, C_WINDOW), (0, 0), (0, 0))
    kp, vp = jnp.pad(k, pad), jnp.pad(v, pad)
    span = Q_BLOCK + 2 * C_WINDOW
    sink_col = sink.astype(F32).reshape(1, C_KV_HEADS, grp, 1, 1)
    scale = HEAD_DIM ** -0.5

    def block(args):
        qb, b = args
        start = b * Q_BLOCK
        kb = lax.dynamic_slice_in_dim(kp, start, span, axis=1)
        vb = lax.dynamic_slice_in_dim(vp, start, span, axis=1).astype(F32)
        qpos = start + jnp.arange(Q_BLOCK)
        kpos = start - C_WINDOW + jnp.arange(span)
        valid = ((jnp.abs(kpos[None, :] - qpos[:, None]) <= C_WINDOW)
                 & (kpos >= 0)[None, :] & (kpos < s)[None, :])
        sc = jnp.einsum('bqkgd,bskd->bkgqs', qb, kb, preferred_element_type=F32) * scale
        sc = jnp.where(valid, sc, NEG_INF)
        sc = jnp.concatenate([sc, jnp.broadcast_to(sink_col, sc.shape[:-1] + (1,))], axis=-1)
        a = jax.nn.softmax(sc, axis=-1)[..., :-1]
        return jnp.einsum('bkgqs,bskd->bqkgd', a, vb)

    o = from_blocks(lax.map(block, (to_blocks(q), jnp.arange(s // Q_BLOCK))))
    return o.reshape(bn, s, qd).astype(hn.dtype) @ w_out


def mixer_axial_gqa(hn, w_in, w_out, q_norm, k_norm, row, col):
    bn, s, _ = hn.shape
    qd, kvd = D_HEADS * HEAD_DIM, D_KV_HEADS * HEAD_DIM
    grp = D_HEADS // D_KV_HEADS
    proj = hn @ w_in
    q = axial_rope(rms_norm(proj[..., :qd].reshape(bn, s, D_HEADS, HEAD_DIM), q_norm), row, col)
    q = q.reshape(bn, s, D_KV_HEADS, grp, HEAD_DIM)
    k = axial_rope(rms_norm(proj[..., qd:qd + kvd].reshape(bn, s, D_KV_HEADS, HEAD_DIM), k_norm), row, col)
    v = proj[..., qd + kvd:].reshape(bn, s, D_KV_HEADS, HEAD_DIM).astype(F32)
    scale = HEAD_DIM ** -0.5

    def block(qb):
        sc = jnp.einsum('bqkgd,bskd->bkgqs', qb, k, preferred_element_type=F32) * scale
        a = jax.nn.softmax(sc, axis=-1)
        return jnp.einsum('bkgqs,bskd->bqkgd', a, v)

    o = from_blocks(lax.map(block, to_blocks(q)))
    return o.reshape(bn, s, qd).astype(hn.dtype) @ w_out


def hier_moe(xn, w_group, b_group, w_expert, b_expert, w_gate, w_up, w_down):
    bn, s, _ = xn.shape
    xf = xn.astype(F32)
    g_prob = jax.nn.softmax(xf @ w_group.astype(F32) + b_group.astype(F32), axis=-1)
    g_w, g_idx = lax.top_k(g_prob, 1)
    e_logits = (xf @ w_expert.astype(F32) + b_expert.astype(F32)).reshape(bn, s, MOE_GROUPS, MOE_EXPERTS_PER_GROUP)
    e_in = jnp.einsum('bsg,bsge->bse', jax.nn.one_hot(g_idx[..., 0], MOE_GROUPS, dtype=F32), e_logits)
    e_val, e_idx = lax.top_k(e_in, MOE_TOP_K)
    e_w = jax.nn.softmax(e_val, axis=-1) * g_w
    expert_id = g_idx * MOE_EXPERTS_PER_GROUP + e_idx
    comb = jnp.einsum('bsk,bske->ebs', e_w, jax.nn.one_hot(expert_id, MOE_EXPERTS, dtype=F32))
    gate = jnp.einsum('bsd,edf->ebsf', xn, w_gate)
    up = jnp.einsum('bsd,edf->ebsf', xn, w_up)
    act = jax.nn.silu(gate) * up * comb.astype(xn.dtype)[..., None]
    return jnp.einsum('ebsf,efd->bsd', act, w_down).astype(xn.dtype)


def setup_inputs(seed: int = 0) -> dict:
    key = jax.random.key(seed)
    ks = iter(jax.random.split(key, 40))

    def nrm(shape, scale):
        return scale * jax.random.normal(next(ks), shape, dtype=F32)

    def gain(shape):
        return 1.0 + 0.05 * jax.random.normal(next(ks), shape, dtype=F32)

    d, hd = D_MODEL, HEAD_DIM
    a_w = 2 * A_HEADS * hd
    c_in = (C_HEADS + 2 * C_KV_HEADS) * hd
    d_in = (D_HEADS + 2 * D_KV_HEADS) * hd
    return {
        "x": nrm((BATCH, SEQ, d), 1.0),
        "p": nrm((DEPTH, BATCH, SEQ, PLE_DIM), 1.0),
        "norm_mix": gain((DEPTH, d)),
        "norm_ffn": gain((DEPTH, d)),
        "norm_ple": gain((DEPTH, d)),
        "norm_final": gain((d,)),
        "a_w_in": nrm((N_REP, d, 3 * a_w), d ** -0.5),
        "a_w_out": nrm((N_REP, a_w, d), a_w ** -0.5),
        "a_lam_q1": nrm((N_REP, hd), 0.1),
        "a_lam_k1": nrm((N_REP, hd), 0.1),
        "a_lam_q2": nrm((N_REP, hd), 0.1),
        "a_lam_k2": nrm((N_REP, hd), 0.1),
        "a_subln": gain((N_REP, 2 * hd)),
        "b_w_in": nrm((N_REP, d, 3 * B_WIDTH), d ** -0.5),
        "b_w_out": nrm((N_REP, B_WIDTH, d), B_WIDTH ** -0.5),
        "c_w_in": nrm((N_REP, d, c_in), d ** -0.5),
        "c_w_out": nrm((N_REP, C_HEADS * hd, d), (C_HEADS * hd) ** -0.5),
        "c_sink": nrm((N_REP, C_HEADS), 0.5),
        "d_w_in": nrm((N_REP, d, d_in), d ** -0.5),
        "d_w_out": nrm((N_REP, D_HEADS * hd, d), (D_HEADS * hd) ** -0.5),
        "d_q_norm": gain((N_REP, hd)),
        "d_k_norm": gain((N_REP, hd)),
        "moe_w_group": nrm((DEPTH, d, MOE_GROUPS), d ** -0.5),
        "moe_b_group": nrm((DEPTH, MOE_GROUPS), 0.01),
        "moe_w_expert": nrm((DEPTH, d, MOE_EXPERTS), d ** -0.5),
        "moe_b_expert": nrm((DEPTH, MOE_EXPERTS), 0.01),
        "moe_w_gate": nrm((DEPTH, MOE_EXPERTS, d, MOE_FF), d ** -0.5),
        "moe_w_up": nrm((DEPTH, MOE_EXPERTS, d, MOE_FF), d ** -0.5),
        "moe_w_down": nrm((DEPTH, MOE_EXPERTS, MOE_FF, d), MOE_FF ** -0.5),
        "ple_w_gate": nrm((DEPTH, d, d), d ** -0.5),
        "ple_w_proj": nrm((DEPTH, PLE_DIM, d), PLE_DIM ** -0.5),
    }


def reference(x, p, norm_mix, norm_ffn, norm_ple, norm_final,
              a_w_in, a_w_out, a_lam_q1, a_lam_k1, a_lam_q2, a_lam_k2, a_subln,
              b_w_in, b_w_out,
              c_w_in, c_w_out, c_sink,
              d_w_in, d_w_out, d_q_norm, d_k_norm,
              moe_w_group, moe_b_group, moe_w_expert, moe_b_expert, moe_w_gate, moe_w_up, moe_w_down,
              ple_w_gate, ple_w_proj):
    s = x.shape[1]
    rows = s // GRID_W
    pos = jnp.arange(s)
    row = jnp.repeat(jnp.arange(rows), GRID_W)
    col = jnp.tile(jnp.arange(GRID_W), rows)
    h = x
    for i in range(DEPTH):
        r, kind = divmod(i, N_MIXERS)
        hn = rms_norm(h, norm_mix[i])
        if kind == 0:
            lam_init = 0.8 - 0.6 * math.exp(-0.3 * i)
            y = mixer_diff_attn(hn, a_w_in[r], a_w_out[r], a_lam_q1[r], a_lam_k1[r],
                                a_lam_q2[r], a_lam_k2[r], a_subln[r], lam_init, pos)
        elif kind == 1:
            y = mixer_dilated(hn, b_w_in[r], b_w_out[r], pos)
        elif kind == 2:
            y = mixer_window_sink(hn, c_w_in[r], c_w_out[r], c_sink[r], pos)
        else:
            y = mixer_axial_gqa(hn, d_w_in[r], d_w_out[r], d_q_norm[r], d_k_norm[r], row, col)
        h = h + y
        h = h + hier_moe(rms_norm(h, norm_ffn[i]), moe_w_group[i], moe_b_group[i], moe_w_expert[i],
                         moe_b_expert[i], moe_w_gate[i], moe_w_up[i], moe_w_down[i])
        gate = jax.nn.sigmoid(rms_norm(h, norm_ple[i]) @ ple_w_gate[i])
        h = h + gate * (p[i] @ ple_w_proj[i])
    return rms_norm(h, norm_final)
```

```python
import functools
import math

import jax
import jax.numpy as jnp
from jax import lax
from jax.experimental import pallas as pl
from jax.experimental.pallas import tpu as pltpu

F32 = jnp.float32
BF16 = jnp.bfloat16

HEAD_DIM = 64
LANES = 128
EPS = 1e-6
LOG2E = 1.4426950408889634
NEG_INF = -1e30
ROPE_THETA = 500000.0
ROT_DIM = HEAD_DIM // 4
AXIAL_THETA = 10000.0
GRID_W = 64
MOE_GROUPS = 4
MOE_PER_GROUP = 4
MOE_EXPERTS = 16
VMEM_LIMIT = 56 * 1024 * 1024


def _cparams(sem):
    return pltpu.CompilerParams(dimension_semantics=sem, vmem_limit_bytes=VMEM_LIMIT)


def _rms(x, g):
    return x * lax.rsqrt(jnp.mean(x * x, axis=-1, keepdims=True) + EPS) * g


def _proj_kernel(h_ref, g_ref, wq_ref, wkt_ref, wv_ref, qc_ref, qs1_ref, qs2_ref, kc_ref, ks_ref,
                 qg_ref, kg_ref, q_ref, kt_ref, v_ref, *, hq, hk, nv, dv, shift, kperm, qk_norm):
    xn = _rms(h_ref[...], g_ref[...]).astype(BF16)
    qf = jnp.dot(xn, wq_ref[...], preferred_element_type=F32)
    qc, qs1, qs2 = qc_ref[...], qs1_ref[...], qs2_ref[...]
    for h in range(hq):
        s = qf[:, h * LANES:(h + 1) * LANES]
        if qk_norm:
            ms = jnp.sum(s * s, axis=-1, keepdims=True) * (1.0 / HEAD_DIM)
            s = s * lax.rsqrt(ms + EPS) * qg_ref[...]
        r = s * qc + pltpu.roll(s, LANES - shift, 1) * qs1 + pltpu.roll(s, shift, 1) * qs2
        q_ref[h] = r[:, :HEAD_DIM].astype(BF16)
    kf = lax.dot_general(wkt_ref[...], xn, (((1,), (1,)), ((), ())), preferred_element_type=F32)
    kc, ks = kc_ref[...], ks_ref[...]
    for h in range(hk):
        s = kf[h * HEAD_DIM:(h + 1) * HEAD_DIM, :]
        if qk_norm:
            ms = jnp.sum(s * s, axis=0, keepdims=True) * (1.0 / HEAD_DIM)
            s = s * lax.rsqrt(ms + EPS) * kg_ref[...]
        partner = jnp.concatenate([s[a:b] for a, b in kperm], axis=0)
        kt_ref[h] = (s * kc + partner * ks).astype(BF16)
    vf = jnp.dot(xn, wv_ref[...], preferred_element_type=F32)
    lane = lax.broadcasted_iota(jnp.int32, vf.shape, 1)
    v_ref[...] = jnp.where((lane & (nv - 1)) == dv, 1.0, vf).astype(BF16)


def _project(h, g, wq, wk, wv, tabs, *, hq, hk, hv, dv, nv, shift, kperm, qk_norm, qg=None, kg=None, tm=512):
    s, d = h.shape
    tm = min(tm, s)
    wq_p = jnp.pad(wq.reshape(d, hq, HEAD_DIM), ((0, 0), (0, 0), (0, LANES - HEAD_DIM))).reshape(d, hq * LANES).astype(BF16)
    wkt = wk.T.astype(BF16)
    wv_p = jnp.pad(wv.reshape(d, hv, dv), ((0, 0), (0, 0), (0, nv - dv))).reshape(d, hv * nv).astype(BF16)
    qc, qs1, qs2, kc, ks = tabs
    if qg is None:
        qg = jnp.ones((1, LANES), F32)
        kg = jnp.ones((HEAD_DIM, 1), F32)
    full = lambda a: pl.BlockSpec(a.shape, lambda i: (0,) * a.ndim)
    kern = functools.partial(_proj_kernel, hq=hq, hk=hk, nv=nv, dv=dv, shift=shift, kperm=kperm, qk_norm=qk_norm)
    return pl.pallas_call(
        kern,
        out_shape=(jax.ShapeDtypeStruct((hq, s, HEAD_DIM), BF16),
                   jax.ShapeDtypeStruct((hk, HEAD_DIM, s), BF16),
                   jax.ShapeDtypeStruct((s, hv * nv), BF16)),
        grid=(s // tm,),
        in_specs=[pl.BlockSpec((tm, d), lambda i: (i, 0)), full(g), full(wq_p), full(wkt), full(wv_p),
                  pl.BlockSpec((tm, LANES), lambda i: (i, 0)), pl.BlockSpec((tm, LANES), lambda i: (i, 0)),
                  pl.BlockSpec((tm, LANES), lambda i: (i, 0)),
                  pl.BlockSpec((HEAD_DIM, tm), lambda i: (0, i)), pl.BlockSpec((HEAD_DIM, tm), lambda i: (0, i)),
                  full(qg), full(kg)],
        out_specs=(pl.BlockSpec((hq, tm, HEAD_DIM), lambda i: (0, i, 0)),
                   pl.BlockSpec((hk, HEAD_DIM, tm), lambda i: (0, 0, i)),
                   pl.BlockSpec((tm, hv * nv), lambda i: (i, 0))),
        compiler_params=_cparams(("parallel",)),
        name="proj",
    )(h, g, wq_p, wkt, wv_p, qc, qs1, qs2, kc, ks, qg, kg)


def _rope_tables(ang_list, s, scale):
    cs, sn1, sn2, ksn = [], [], [], []
    used = 0
    for ang in ang_list:
        c, sn = jnp.cos(ang), jnp.sin(ang)
        z = jnp.zeros_like(sn)
        cs += [c, c]
        sn1 += [-sn, z]
        sn2 += [z, sn]
        ksn += [-sn, sn]
        used += 2 * ang.shape[1]
    rest = HEAD_DIM - used
    ones, zeros = jnp.ones((s, rest), F32), jnp.zeros((s, rest), F32)
    c64 = jnp.concatenate(cs + [ones], axis=1)
    pad = jnp.zeros((s, LANES - HEAD_DIM), F32)
    qc = jnp.concatenate([c64, pad], axis=1) * scale
    qs1 = jnp.concatenate(sn1 + [zeros, pad], axis=1) * scale
    qs2 = jnp.concatenate(sn2 + [zeros, pad], axis=1) * scale
    kc = c64.T
    ks = jnp.concatenate(ksn + [zeros], axis=1).T
    return qc, qs1, qs2, kc, ks


def _flash_kernel(*refs, streams, tq, tk, nv, dv, band, n_kblocks, mode, lam_init, has_sink):
    it = iter(refs)
    q_ref, kt_ref, v_ref = next(it), next(it), next(it)
    sink_ref = next(it) if has_sink else None
    lam_ref = subln_ref = None
    if mode == "diff":
        lam_ref, subln_ref = next(it), next(it)
    o_ref = next(it)
    lse_ref = next(it) if mode == "lse" else None
    m_scr, acc_scr = next(it), next(it)

    qb, kb = pl.program_id(1), pl.program_id(2)
    nsteps = pl.num_programs(2)

    @pl.when(kb == 0)
    def _init():
        for si, (qh, _, _) in enumerate(streams):
            rows = len(qh) * tq
            if has_sink:
                u = pl.program_id(0)
                m0 = jnp.concatenate([jnp.full((tq, 1), LOG2E, F32) * sink_ref[u, j] for j in qh], axis=0)
                lane = lax.broadcasted_iota(jnp.int32, (rows, nv), 1)
                m_scr[si] = m0
                acc_scr[si] = jnp.where(lane == dv, 1.0, 0.0).astype(F32)
            else:
                m_scr[si] = jnp.full((rows, 1), NEG_INF, F32)
                acc_scr[si] = jnp.zeros((rows, nv), F32)

    def _step():
        for si, (qh, ki, vi) in enumerate(streams):
            rows = len(qh) * tq
            if len(qh) == 1:
                q = q_ref[qh[0]]
            else:
                q = jnp.concatenate([q_ref[j] for j in qh], axis=0)
            sc = jnp.dot(q, kt_ref[ki], preferred_element_type=F32)
            if band is not None:
                halfw, dil, off = band
                row = lax.broadcasted_iota(jnp.int32, (rows, tk), 0) & (tq - 1)
                col = lax.broadcasted_iota(jnp.int32, (rows, tk), 1)
                diff = col - row + ((kb - off) * tk)
                valid = (jnp.abs(diff) <= halfw) & ((diff & (dil - 1)) == 0)
                sc = jnp.where(valid, sc, NEG_INF)
            m_old = m_scr[si]
            m_new = jnp.maximum(m_old, jnp.max(sc, axis=1, keepdims=True))
            p = jnp.exp2(sc - m_new)
            if band is not None:
                p = jnp.where(valid, p, 0.0)
            alpha = jnp.exp2(m_old - m_new)
            pv = jnp.dot(p.astype(BF16), v_ref[:, vi * nv:(vi + 1) * nv], preferred_element_type=F32)
            acc_scr[si] = acc_scr[si] * alpha + pv
            m_scr[si] = m_new

    if band is None:
        _step()
    else:
        _, _, off = band
        kabs = qb * (tq // tk) + kb - off
        pl.when((kabs >= 0) & (kabs < n_kblocks))(_step)

    @pl.when(kb == nsteps - 1)
    def _fin():
        if mode == "diff":
            a0, a1 = acc_scr[0], acc_scr[1]
            lam_rows = lam_ref[...]
            lam = (jnp.exp(jnp.sum(lam_rows[0:1] * lam_rows[1:2], axis=1, keepdims=True))
                   - jnp.exp(jnp.sum(lam_rows[2:3] * lam_rows[3:4], axis=1, keepdims=True)) + lam_init)
            o = a0[:, :dv] / a0[:, dv:dv + 1] - lam * (a1[:, :dv] / a1[:, dv:dv + 1])
            o = _rms(o, subln_ref[...]) * (1.0 - lam_init)
            o_ref[...] = o.astype(o_ref.dtype)
        else:
            col = 0
            for si, (qh, _, _) in enumerate(streams):
                a = acc_scr[si]
                l = a[:, dv:dv + 1]
                o = a[:, :dv] / l
                if mode == "lse":
                    lse = m_scr[si] + jnp.log2(l)
                for j in range(len(qh)):
                    o_ref[:, col:col + dv] = o[j * tq:(j + 1) * tq].astype(o_ref.dtype)
                    if mode == "lse":
                        lse_ref[:, col:col + dv] = jnp.broadcast_to(lse[j * tq:(j + 1) * tq], (tq, dv))
                    col += dv


def _flash(q, kt, v, *, units, q_per_unit, k_per_unit, v_per_unit, streams, tq, tk, nv, dv, out_w,
           unit0=0, band=None, mode="gqa", sink=None, lam=None, subln=None, lam_init=0.0, name="flash"):
    s = q.shape[1]
    tq, tk = min(tq, s), min(tk, s)
    assert s % tq == 0 and s % tk == 0 and (band is None or tq % tk == 0)
    n_kblocks = s // tk
    if band is None:
        nsteps = n_kblocks
        kmap = lambda u, i, j: (u + unit0, 0, j)
        vmap = lambda u, i, j: (j, u + unit0)
    else:
        halfw = band[0]
        off = -(-halfw // tk)
        nsteps = tq // tk + 2 * off
        band = (band[0], band[1], off)
        kidx = lambda i, j: jnp.clip(i * (tq // tk) + j - off, 0, n_kblocks - 1)
        kmap = lambda u, i, j: (u + unit0, 0, kidx(i, j))
        vmap = lambda u, i, j: (kidx(i, j), u + unit0)
    in_specs = [pl.BlockSpec((q_per_unit, tq, HEAD_DIM), lambda u, i, j: (u + unit0, i, 0)),
                pl.BlockSpec((k_per_unit, HEAD_DIM, tk), kmap),
                pl.BlockSpec((tk, v_per_unit * nv), vmap)]
    args = [q, kt, v]
    if sink is not None:
        in_specs.append(pl.BlockSpec(memory_space=pltpu.SMEM))
        args.append(sink)
    if mode == "diff":
        in_specs += [pl.BlockSpec(lam.shape, lambda u, i, j: (0, 0)), pl.BlockSpec(subln.shape, lambda u, i, j: (0, 0))]
        args += [lam, subln]
    out_shape = [jax.ShapeDtypeStruct((s, units * out_w), BF16)]
    out_specs = [pl.BlockSpec((tq, out_w), lambda u, i, j: (i, u))]
    if mode == "lse":
        out_shape.append(jax.ShapeDtypeStruct((s, units * out_w), F32))
        out_specs.append(pl.BlockSpec((tq, out_w), lambda u, i, j: (i, u)))
    rows = [len(qh) * tq for qh, _, _ in streams]
    assert len(set(rows)) == 1
    kern = functools.partial(_flash_kernel, streams=streams, tq=tq, tk=tk, nv=nv, dv=dv, band=band,
                             n_kblocks=n_kblocks, mode=mode, lam_init=lam_init, has_sink=sink is not None)
    res = pl.pallas_call(
        kern,
        out_shape=tuple(out_shape),
        grid=(units, s // tq, nsteps),
        in_specs=in_specs,
        out_specs=tuple(out_specs),
        scratch_shapes=[pltpu.VMEM((len(streams), rows[0], 1), F32), pltpu.VMEM((len(streams), rows[0], nv), F32)],
        compiler_params=_cparams(("parallel", "parallel", "arbitrary")),
        name=name,
    )(*args)
    return res if mode == "lse" else res[0]


def _outproj_kernel(h_ref, o_ref, w_ref, out_ref):
    out_ref[...] = h_ref[...] + jnp.dot(o_ref[...], w_ref[...], preferred_element_type=F32)


def _outproj(h, o, w, tm=512):
    s, d = h.shape
    tm = min(tm, s)
    w = w.astype(BF16)
    return pl.pallas_call(
        _outproj_kernel,
        out_shape=jax.ShapeDtypeStruct((s, d), F32),
        grid=(s // tm,),
        in_specs=[pl.BlockSpec((tm, d), lambda i: (i, 0)), pl.BlockSpec((tm, o.shape[1]), lambda i: (i, 0)),
                  pl.BlockSpec(w.shape, lambda i: (0, 0))],
        out_specs=pl.BlockSpec((tm, d), lambda i: (i, 0)),
        compiler_params=_cparams(("parallel",)),
        name="outproj",
    )(h, o, w)


def _outproj_groups_kernel(h_ref, o0_ref, o1_ref, o2_ref, l0_ref, l1_ref, l2_ref, w_ref, out_ref):
    l0, l1, l2 = l0_ref[...], l1_ref[...], l2_ref[...]
    mx = jnp.maximum(jnp.maximum(l0, l1), l2)
    e0, e1, e2 = jnp.exp2(l0 - mx), jnp.exp2(l1 - mx), jnp.exp2(l2 - mx)
    tot = e0 + e1 + e2
    acc = h_ref[...]
    gw = o0_ref.shape[1]
    for g, (o_ref, e) in enumerate(((o0_ref, e0), (o1_ref, e1), (o2_ref, e2))):
        og = (o_ref[...].astype(F32) * (e / tot)).astype(BF16)
        acc = acc + jnp.dot(og, w_ref[g * gw:(g + 1) * gw, :], preferred_element_type=F32)
    out_ref[...] = acc


def _outproj_groups(h, os_, ls_, w, tm=512):
    s, d = h.shape
    tm = min(tm, s)
    w = w.astype(BF16)
    gw = os_[0].shape[1]
    row = lambda width: pl.BlockSpec((tm, width), lambda i: (i, 0))
    return pl.pallas_call(
        _outproj_groups_kernel,
        out_shape=jax.ShapeDtypeStruct((s, d), F32),
        grid=(s // tm,),
        in_specs=[row(d)] + [row(gw)] * 6 + [pl.BlockSpec(w.shape, lambda i: (0, 0))],
        out_specs=row(d),
        compiler_params=_cparams(("parallel",)),
        name="outproj_groups",
    )(h, *os_, *ls_, w)


def _route(logits):
    lane = lax.broadcasted_iota(jnp.int32, logits.shape, 1).astype(F32)
    big = 1e6
    gl = jnp.where(lane < MOE_GROUPS, logits, NEG_INF)
    gmax = jnp.max(gl, axis=1, keepdims=True)
    gidx = jnp.min(jnp.where(gl == gmax, lane, big), axis=1, keepdims=True)
    gw = 1.0 / jnp.sum(jnp.exp(gl - gmax), axis=1, keepdims=True)
    lo = MOE_GROUPS + gidx * MOE_PER_GROUP
    el = jnp.where((lane >= lo) & (lane < lo + MOE_PER_GROUP), logits, NEG_INF)
    v1 = jnp.max(el, axis=1, keepdims=True)
    i1 = jnp.min(jnp.where(el == v1, lane, big), axis=1, keepdims=True)
    el2 = jnp.where(lane == i1, NEG_INF, el)
    v2 = jnp.max(el2, axis=1, keepdims=True)
    i2 = jnp.min(jnp.where(el2 == v2, lane, big), axis=1, keepdims=True)
    e2 = jnp.exp(v2 - v1)
    w1 = gw / (1.0 + e2)
    w2 = w1 * e2
    return jnp.where(lane == i1, w1, 0.0) + jnp.where(lane == i2, w2, 0.0)


def _moe_kernel(h_ref, g_ref, wr_ref, br_ref, wg_ref, wu_ref, wd_ref, out_ref, xn_scr, comb_scr, acc_scr):
    e = pl.program_id(1)

    @pl.when(e == 0)
    def _init():
        xn = _rms(h_ref[...], g_ref[...])
        xn_scr[...] = xn.astype(BF16)
        logits = jnp.dot(xn, wr_ref[...], preferred_element_type=F32, precision=lax.Precision.HIGHEST) + br_ref[...]
        comb_scr[...] = _route(logits)
        acc_scr[...] = jnp.zeros_like(acc_scr)

    xn = xn_scr[...]
    gate = jnp.dot(xn, wg_ref[0], preferred_element_type=F32)
    up = jnp.dot(xn, wu_ref[0], preferred_element_type=F32)
    comb = comb_scr[...]
    lane = lax.broadcasted_iota(jnp.int32, comb.shape, 1)
    we = jnp.sum(jnp.where(lane == e + MOE_GROUPS, comb, 0.0), axis=1, keepdims=True)
    act = gate * jax.nn.sigmoid(gate) * up * we
    acc_scr[...] += jnp.dot(act.astype(BF16), wd_ref[0], preferred_element_type=F32)

    @pl.when(e == pl.num_programs(1) - 1)
    def _fin():
        out_ref[...] = h_ref[...] + acc_scr[...]


def _moe(h, g, w_group, b_group, w_expert, b_expert, w_gate, w_up, w_down, tm=1024):
    s, d = h.shape
    tm = min(tm, s)
    ne, _, ff = w_gate.shape
    nr = MOE_GROUPS + MOE_EXPERTS
    wr = jnp.pad(jnp.concatenate([w_group, w_expert], axis=1), ((0, 0), (0, LANES - nr)))
    br = jnp.pad(jnp.concatenate([b_group, b_expert]), (0, LANES - nr)).reshape(1, LANES)
    wg, wu, wd = w_gate.astype(BF16), w_up.astype(BF16), w_down.astype(BF16)
    return pl.pallas_call(
        _moe_kernel,
        out_shape=jax.ShapeDtypeStruct((s, d), F32),
        grid=(s // tm, ne),
        in_specs=[pl.BlockSpec((tm, d), lambda i, e: (i, 0)), pl.BlockSpec((1, d), lambda i, e: (0, 0)),
                  pl.BlockSpec((d, LANES), lambda i, e: (0, 0)), pl.BlockSpec((1, LANES), lambda i, e: (0, 0)),
                  pl.BlockSpec((1, d, ff), lambda i, e: (e, 0, 0)), pl.BlockSpec((1, d, ff), lambda i, e: (e, 0, 0)),
                  pl.BlockSpec((1, ff, d), lambda i, e: (e, 0, 0))],
        out_specs=pl.BlockSpec((tm, d), lambda i, e: (i, 0)),
        scratch_shapes=[pltpu.VMEM((tm, d), BF16), pltpu.VMEM((tm, LANES), F32), pltpu.VMEM((tm, d), F32)],
        compiler_params=_cparams(("parallel", "arbitrary")),
        name="moe",
    )(h, g, wr, br, wg, wu, wd)


def _ple_kernel(h_ref, g_ref, wg_ref, p_ref, wp_ref, gf_ref, out_ref, *, final):
    x = h_ref[...]
    xn = _rms(x, g_ref[...]).astype(BF16)
    gate = jax.nn.sigmoid(jnp.dot(xn, wg_ref[...], preferred_element_type=F32))
    proj = jnp.dot(p_ref[...].astype(BF16), wp_ref[...], preferred_element_type=F32)
    y = x + gate * proj
    if final:
        y = _rms(y, gf_ref[...])
    out_ref[...] = y


def _ple(h, g, wg, p, wp, gf, final, tm=512):
    s, d = h.shape
    tm = min(tm, s)
    wg, wp = wg.astype(BF16), wp.astype(BF16)
    const = lambda a: pl.BlockSpec(a.shape, lambda i: (0, 0))
    return pl.pallas_call(
        functools.partial(_ple_kernel, final=final),
        out_shape=jax.ShapeDtypeStruct((s, d), F32),
        grid=(s // tm,),
        in_specs=[pl.BlockSpec((tm, d), lambda i: (i, 0)), const(g), const(wg),
                  pl.BlockSpec((tm, p.shape[1]), lambda i: (i, 0)), const(wp), const(gf)],
        out_specs=pl.BlockSpec((tm, d), lambda i: (i, 0)),
        compiler_params=_cparams(("parallel",)),
        name="ple",
    )(h, g, wg, p, wp, gf)


PARTIAL_KPERM = ((8, 16), (0, 8), (16, 64))
AXIAL_KPERM = ((16, 32), (0, 16), (48, 64), (32, 48))
B_PAIRS = ((128, 1), (512, 4), (2048, 16))


def _partial_tables(s):
    inv = ROPE_THETA ** (-jnp.arange(0, ROT_DIM, 2, dtype=F32) / ROT_DIM)
    ang = jnp.arange(s).astype(F32)[:, None] * inv[None, :]
    return _rope_tables([ang], s, LOG2E * HEAD_DIM ** -0.5)


def _axial_tables(s):
    half = HEAD_DIM // 2
    inv = AXIAL_THETA ** (-jnp.arange(0, half, 2, dtype=F32) / half)
    t = jnp.arange(s)
    ang_r = (t // GRID_W).astype(F32)[:, None] * inv[None, :]
    ang_c = (t % GRID_W).astype(F32)[:, None] * inv[None, :]
    return _rope_tables([ang_r, ang_c], s, LOG2E * HEAD_DIM ** -0.5)


def _mixer_diff(h, g, w_in, w_out, lam_rows, subln, lam_init, tabs):
    d = h.shape[1]
    heads = d // (2 * HEAD_DIM)
    aw = 2 * heads * HEAD_DIM
    q, kt, v = _project(h, g, w_in[:, :aw], w_in[:, aw:2 * aw], w_in[:, 2 * aw:], tabs,
                        hq=2 * heads, hk=2 * heads, hv=heads, dv=2 * HEAD_DIM, nv=4 * HEAD_DIM,
                        shift=ROT_DIM // 2, kperm=PARTIAL_KPERM, qk_norm=False)
    o = _flash(q, kt, v, units=heads, q_per_unit=2, k_per_unit=2, v_per_unit=1,
               streams=(((0,), 0, 0), ((1,), 1, 0)), tq=512, tk=1024, nv=4 * HEAD_DIM, dv=2 * HEAD_DIM,
               out_w=2 * HEAD_DIM, mode="diff", lam=lam_rows, subln=subln, lam_init=lam_init, name="flash_diff")
    return _outproj(h, o, w_out)


def _mixer_dilated(h, g, w_in, w_out, tabs):
    nh, hg = 12, 4
    bw = nh * HEAD_DIM
    q, kt, v = _project(h, g, w_in[:, :bw], w_in[:, bw:2 * bw], w_in[:, 2 * bw:], tabs,
                        hq=nh, hk=nh, hv=nh, dv=HEAD_DIM, nv=LANES,
                        shift=ROT_DIM // 2, kperm=PARTIAL_KPERM, qk_norm=False)
    os_, ls_ = [], []
    for gi, (win, dil) in enumerate(B_PAIRS):
        halfw = (win // (2 * dil)) * dil
        o, lse = _flash(q, kt, v, units=1, unit0=gi, q_per_unit=hg, k_per_unit=hg, v_per_unit=hg,
                        streams=tuple(((j,), j, j) for j in range(hg)), tq=1024, tk=max(halfw, 256),
                        nv=LANES, dv=HEAD_DIM, out_w=hg * HEAD_DIM, band=(halfw, dil), mode="lse",
                        name=f"flash_dilated{gi}")
        os_.append(o)
        ls_.append(lse)
    return _outproj_groups(h, os_, ls_, w_out)


def _gqa_project(h, g, w_in, tabs, shift, kperm, qk_norm, qg=None, kg=None):
    nq, nkv = 16, 4
    qd, kvd = nq * HEAD_DIM, nkv * HEAD_DIM
    return _project(h, g, w_in[:, :qd], w_in[:, qd:qd + kvd], w_in[:, qd + kvd:], tabs,
                    hq=nq, hk=nkv, hv=nkv, dv=HEAD_DIM, nv=LANES, shift=shift, kperm=kperm,
                    qk_norm=qk_norm, qg=qg, kg=kg)


def _mixer_window(h, g, w_in, w_out, sink, tabs):
    q, kt, v = _gqa_project(h, g, w_in, tabs, ROT_DIM // 2, PARTIAL_KPERM, False)
    o = _flash(q, kt, v, units=4, q_per_unit=4, k_per_unit=1, v_per_unit=1, streams=(((0, 1, 2, 3), 0, 0),),
               tq=256, tk=256, nv=LANES, dv=HEAD_DIM, out_w=4 * HEAD_DIM, band=(128, 1), mode="gqa",
               sink=sink.reshape(4, 4), name="flash_window")
    return _outproj(h, o, w_out)


def _mixer_axial(h, g, w_in, w_out, q_norm, k_norm, tabs):
    qg = jnp.pad(q_norm, (0, LANES - HEAD_DIM)).reshape(1, LANES)
    kg = k_norm.reshape(HEAD_DIM, 1)
    q, kt, v = _gqa_project(h, g, w_in, tabs, HEAD_DIM // 4, AXIAL_KPERM, True, qg, kg)
    o = _flash(q, kt, v, units=4, q_per_unit=4, k_per_unit=1, v_per_unit=1, streams=(((0, 1, 2, 3), 0, 0),),
               tq=256, tk=1024, nv=LANES, dv=HEAD_DIM, out_w=4 * HEAD_DIM, mode="gqa", name="flash_axial")
    return _outproj(h, o, w_out)


def kernel(x, p, norm_mix, norm_ffn, norm_ple, norm_final, a_w_in, a_w_out, a_lam_q1, a_lam_k1, a_lam_q2, a_lam_k2, a_subln, b_w_in, b_w_out, c_w_in, c_w_out, c_sink, d_w_in, d_w_out, d_q_norm, d_k_norm, moe_w_group, moe_b_group, moe_w_expert, moe_b_expert, moe_w_gate, moe_w_up, moe_w_down, ple_w_gate, ple_w_proj):
    bn, s, d = x.shape
    assert bn == 1
    depth = p.shape[0]
    h = x[0]
    ptabs = _partial_tables(s)
    atabs = _axial_tables(s)
    row = lambda a: a.reshape(1, -1)
    for i in range(depth):
        r, kind = divmod(i, 4)
        g = row(norm_mix[i])
        if kind == 0:
            lam_init = 0.8 - 0.6 * math.exp(-0.3 * i)
            lam_rows = jnp.stack([a_lam_q1[r], a_lam_k1[r], a_lam_q2[r], a_lam_k2[r]])
            h = _mixer_diff(h, g, a_w_in[r], a_w_out[r], lam_rows, row(a_subln[r]), lam_init, ptabs)
        elif kind == 1:
            h = _mixer_dilated(h, g, b_w_in[r], b_w_out[r], ptabs)
        elif kind == 2:
            h = _mixer_window(h, g, c_w_in[r], c_w_out[r], c_sink[r], ptabs)
        else:
            h = _mixer_axial(h, g, d_w_in[r], d_w_out[r], d_q_norm[r], d_k_norm[r], atabs)
        h = _moe(h, row(norm_ffn[i]), moe_w_group[i], moe_b_group[i], moe_w_expert[i], moe_b_expert[i],
                 moe_w_gate[i], moe_w_up[i], moe_w_down[i])
        h = _ple(h, row(norm_ple[i]), ple_w_gate[i], p[i, 0], ple_w_proj[i], row(norm_final), final=(i == depth - 1))
    return h[None]
```

```python
import functools
import math

import jax
import jax.numpy as jnp
from jax import lax
from jax.experimental import pallas as pl
from jax.experimental.pallas import tpu as pltpu

F32 = jnp.float32
BF16 = jnp.bfloat16

HEAD_DIM = 64
LANES = 128
EPS = 1e-6
LOG2E = 1.4426950408889634
NEG_INF = -1e30
ROPE_THETA = 500000.0
ROT_DIM = HEAD_DIM // 4
AXIAL_THETA = 10000.0
GRID_W = 64
MOE_GROUPS = 4
MOE_PER_GROUP = 4
MOE_EXPERTS = 16
VMEM_LIMIT = 56 * 1024 * 1024


def _cparams(sem):
    return pltpu.CompilerParams(dimension_semantics=sem, vmem_limit_bytes=VMEM_LIMIT)


def _rms(x, g):
    return x * lax.rsqrt(jnp.mean(x * x, axis=-1, keepdims=True) + EPS) * g


def _proj_kernel(h_ref, g_ref, wq_ref, wkt_ref, wv_ref, qc_ref, qs1_ref, qs2_ref, kc_ref, ks_ref,
                 qg_ref, kg_ref, q_ref, kt_ref, v_ref, *, hq, hk, nv, dv, shift, kperm, qk_norm):
    xn = _rms(h_ref[...], g_ref[...]).astype(BF16)
    qf = jnp.dot(xn, wq_ref[...], preferred_element_type=F32)
    qc, qs1, qs2 = qc_ref[...], qs1_ref[...], qs2_ref[...]
    for h in range(hq):
        s = qf[:, h * LANES:(h + 1) * LANES]
        if qk_norm:
            ms = jnp.sum(s * s, axis=-1, keepdims=True) * (1.0 / HEAD_DIM)
            s = s * lax.rsqrt(ms + EPS) * qg_ref[...]
        r = s * qc + pltpu.roll(s, LANES - shift, 1) * qs1 + pltpu.roll(s, shift, 1) * qs2
        q_ref[h] = r[:, :HEAD_DIM].astype(BF16)
    kf = lax.dot_general(wkt_ref[...], xn, (((1,), (1,)), ((), ())), preferred_element_type=F32)
    kc, ks = kc_ref[...], ks_ref[...]
    for h in range(hk):
        s = kf[h * HEAD_DIM:(h + 1) * HEAD_DIM, :]
        if qk_norm:
            ms = jnp.sum(s * s, axis=0, keepdims=True) * (1.0 / HEAD_DIM)
            s = s * lax.rsqrt(ms + EPS) * kg_ref[...]
        partner = jnp.concatenate([s[a:b] for a, b in kperm], axis=0)
        kt_ref[h] = (s * kc + partner * ks).astype(BF16)
    vf = jnp.dot(xn, wv_ref[...], preferred_element_type=F32)
    lane = lax.broadcasted_iota(jnp.int32, vf.shape, 1)
    v_ref[...] = jnp.where((lane & (nv - 1)) == dv, 1.0, vf).astype(BF16)


def _project(h, g, wq, wk, wv, tabs, *, hq, hk, hv, dv, nv, shift, kperm, qk_norm, qg=None, kg=None, tm=512):
    s, d = h.shape
    tm = min(tm, s)
    wq_p = jnp.pad(wq.reshape(d, hq, HEAD_DIM), ((0, 0), (0, 0), (0, LANES - HEAD_DIM))).reshape(d, hq * LANES).astype(BF16)
    wkt = wk.T.astype(BF16)
    wv_p = jnp.pad(wv.reshape(d, hv, dv), ((0, 0), (0, 0), (0, nv - dv))).reshape(d, hv * nv).astype(BF16)
    qc, qs1, qs2, kc, ks = tabs
    if qg is None:
        qg = jnp.ones((1, LANES), F32)
        kg = jnp.ones((HEAD_DIM, 1), F32)
    full = lambda a: pl.BlockSpec(a.shape, lambda i: (0,) * a.ndim)
    kern = functools.partial(_proj_kernel, hq=hq, hk=hk, nv=nv, dv=dv, shift=shift, kperm=kperm, qk_norm=qk_norm)
    return pl.pallas_call(
        kern,
        out_shape=(jax.ShapeDtypeStruct((hq, s, HEAD_DIM), BF16),
                   jax.ShapeDtypeStruct((hk, HEAD_DIM, s), BF16),
                   jax.ShapeDtypeStruct((s, hv * nv), BF16)),
        grid=(s // tm,),
        in_specs=[pl.BlockSpec((tm, d), lambda i: (i, 0)), full(g), full(wq_p), full(wkt), full(wv_p),
                  pl.BlockSpec((tm, LANES), lambda i: (i, 0)), pl.BlockSpec((tm, LANES), lambda i: (i, 0)),
                  pl.BlockSpec((tm, LANES), lambda i: (i, 0)),
                  pl.BlockSpec((HEAD_DIM, tm), lambda i: (0, i)), pl.BlockSpec((HEAD_DIM, tm), lambda i: (0, i)),
                  full(qg), full(kg)],
        out_specs=(pl.BlockSpec((hq, tm, HEAD_DIM), lambda i: (0, i, 0)),
                   pl.BlockSpec((hk, HEAD_DIM, tm), lambda i: (0, 0, i)),
                   pl.BlockSpec((tm, hv * nv), lambda i: (i, 0))),
        compiler_params=_cparams(("parallel",)),
        name="proj",
    )(h, g, wq_p, wkt, wv_p, qc, qs1, qs2, kc, ks, qg, kg)


def _rope_tables(ang_list, s, scale):
    cs, sn1, sn2, ksn = [], [], [], []
    used = 0
    for ang in ang_list:
        c, sn = jnp.cos(ang), jnp.sin(ang)
        z = jnp.zeros_like(sn)
        cs += [c, c]
        sn1 += [-sn, z]
        sn2 += [z, sn]
        ksn += [-sn, sn]
        used += 2 * ang.shape[1]
    rest = HEAD_DIM - used
    ones, zeros = jnp.ones((s, rest), F32), jnp.zeros((s, rest), F32)
    c64 = jnp.concatenate(cs + [ones], axis=1)
    pad = jnp.zeros((s, LANES - HEAD_DIM), F32)
    qc = jnp.concatenate([c64, pad], axis=1) * scale
    qs1 = jnp.concatenate(sn1 + [zeros, pad], axis=1) * scale
    qs2 = jnp.concatenate(sn2 + [zeros, pad], axis=1) * scale
    kc = c64.T
    ks = jnp.concatenate(ksn + [zeros], axis=1).T
    return qc, qs1, qs2, kc, ks


def _lane_tile(x, width):
    reps = width // LANES
    return x if reps == 1 else jnp.concatenate([x] * reps, axis=1)


def _flash_kernel(*refs, streams, split, tq, tk, nv, dv, band, n_kblocks, mode, lam_init, has_sink):
    it = iter(refs)
    q_ref, kt_ref, v_ref = next(it), next(it), next(it)
    sink_ref = next(it) if has_sink else None
    lam_ref = subln_ref = None
    if mode == "diff":
        lam_ref, subln_ref = next(it), next(it)
    o_ref = next(it)
    lse_ref = next(it) if mode == "lse" else None
    m_scr, acc_scr = next(it), next(it)

    qb, kb = pl.program_id(1), pl.program_id(2)
    nsteps = pl.num_programs(2)
    cr = tq // split

    @pl.when(kb == 0)
    def _init():
        for si, (qi, _, _) in enumerate(streams):
            if has_sink:
                m_scr[si] = jnp.full((tq, LANES), LOG2E, F32) * sink_ref[pl.program_id(0), qi]
                lane = lax.broadcasted_iota(jnp.int32, (tq, nv), 1)
                acc_scr[si] = jnp.where(lane == dv, 1.0, 0.0).astype(F32)
            else:
                m_scr[si] = jnp.full((tq, LANES), NEG_INF, F32)
                acc_scr[si] = jnp.zeros((tq, nv), F32)

    def _step():
        chunks = [(si, c * cr, qi, ki, vi) for si, (qi, ki, vi) in enumerate(streams) for c in range(split)]
        scores = [jnp.dot(q_ref[qi, r0:r0 + cr, :], kt_ref[ki], preferred_element_type=F32)
                  for _, r0, qi, ki, _ in chunks]
        for sc, (si, r0, _, _, vi) in zip(scores, chunks):
            if band is not None:
                halfw, dil, off = band
                row = lax.broadcasted_iota(jnp.int32, (cr, tk), 0) + r0
                col = lax.broadcasted_iota(jnp.int32, (cr, tk), 1)
                diff = col - row + ((kb - off) * tk)
                valid = (jnp.abs(diff) <= halfw) & ((diff & (dil - 1)) == 0)
                sc = jnp.where(valid, sc, NEG_INF)
            m_old = m_scr[si, r0:r0 + cr, :]
            m_new = jnp.maximum(m_old, jnp.broadcast_to(jnp.max(sc, axis=1, keepdims=True), (cr, LANES)))
            p = jnp.exp2(sc - _lane_tile(m_new, tk))
            if band is not None:
                p = jnp.where(valid, p, 0.0)
            alpha = jnp.exp2(m_old - m_new)
            pv = jnp.dot(p.astype(BF16), v_ref[:, vi * nv:(vi + 1) * nv], preferred_element_type=F32)
            acc_scr[si, r0:r0 + cr, :] = acc_scr[si, r0:r0 + cr, :] * _lane_tile(alpha, nv) + pv
            m_scr[si, r0:r0 + cr, :] = m_new

    if band is None:
        _step()
    else:
        _, _, off = band
        kabs = qb * (tq // tk) + kb - off
        pl.when((kabs >= 0) & (kabs < n_kblocks))(_step)

    @pl.when(kb == nsteps - 1)
    def _fin():
        if mode == "diff":
            a0, a1 = acc_scr[0], acc_scr[1]
            lam_rows = lam_ref[...]
            lam = (jnp.exp(jnp.sum(lam_rows[0:1] * lam_rows[1:2], axis=1, keepdims=True))
                   - jnp.exp(jnp.sum(lam_rows[2:3] * lam_rows[3:4], axis=1, keepdims=True)) + lam_init)
            o = a0[:, :dv] / a0[:, dv:dv + 1] - lam * (a1[:, :dv] / a1[:, dv:dv + 1])
            o = _rms(o, subln_ref[...]) * (1.0 - lam_init)
            o_ref[...] = o.astype(o_ref.dtype)
        else:
            for si in range(len(streams)):
                a = acc_scr[si]
                l = a[:, dv:dv + 1]
                o_ref[:, si * dv:(si + 1) * dv] = (a[:, :dv] / l).astype(o_ref.dtype)
                if mode == "lse":
                    lse_ref[:, si * dv:(si + 1) * dv] = m_scr[si][:, :dv] + jnp.log2(l)


def _flash(q, kt, v, *, units, q_per_unit, k_per_unit, v_per_unit, streams, tq, tk, nv, dv, out_w,
           split=1, unit0=0, band=None, mode="gqa", sink=None, lam=None, subln=None, lam_init=0.0, name="flash"):
    s = q.shape[1]
    tq, tk = min(tq, s), min(tk, s)
    assert s % tq == 0 and s % tk == 0 and (band is None or tq % tk == 0)
    n_kblocks = s // tk
    if band is None:
        nsteps = n_kblocks
        kmap = lambda u, i, j: (u + unit0, 0, j)
        vmap = lambda u, i, j: (j, u + unit0)
    else:
        halfw = band[0]
        off = -(-halfw // tk)
        nsteps = tq // tk + 2 * off
        band = (band[0], band[1], off)
        kidx = lambda i, j: jnp.clip(i * (tq // tk) + j - off, 0, n_kblocks - 1)
        kmap = lambda u, i, j: (u + unit0, 0, kidx(i, j))
        vmap = lambda u, i, j: (kidx(i, j), u + unit0)
    in_specs = [pl.BlockSpec((q_per_unit, tq, HEAD_DIM), lambda u, i, j: (u + unit0, i, 0)),
                pl.BlockSpec((k_per_unit, HEAD_DIM, tk), kmap),
                pl.BlockSpec((tk, v_per_unit * nv), vmap)]
    args = [q, kt, v]
    if sink is not None:
        in_specs.append(pl.BlockSpec(memory_space=pltpu.SMEM))
        args.append(sink)
    if mode == "diff":
        in_specs += [pl.BlockSpec(lam.shape, lambda u, i, j: (0, 0)), pl.BlockSpec(subln.shape, lambda u, i, j: (0, 0))]
        args += [lam, subln]
    out_shape = [jax.ShapeDtypeStruct((s, units * out_w), BF16)]
    out_specs = [pl.BlockSpec((tq, out_w), lambda u, i, j: (i, u))]
    if mode == "lse":
        out_shape.append(jax.ShapeDtypeStruct((s, units * out_w), F32))
        out_specs.append(pl.BlockSpec((tq, out_w), lambda u, i, j: (i, u)))
    assert tq % split == 0
    kern = functools.partial(_flash_kernel, streams=streams, split=split, tq=tq, tk=tk, nv=nv, dv=dv, band=band,
                             n_kblocks=n_kblocks, mode=mode, lam_init=lam_init, has_sink=sink is not None)
    res = pl.pallas_call(
        kern,
        out_shape=tuple(out_shape),
        grid=(units, s // tq, nsteps),
        in_specs=in_specs,
        out_specs=tuple(out_specs),
        scratch_shapes=[pltpu.VMEM((len(streams), tq, LANES), F32), pltpu.VMEM((len(streams), tq, nv), F32)],
        compiler_params=_cparams(("parallel", "parallel", "arbitrary")),
        name=name,
    )(*args)
    return res if mode == "lse" else res[0]


def _outproj_kernel(h_ref, o_ref, w_ref, out_ref):
    out_ref[...] = h_ref[...] + jnp.dot(o_ref[...], w_ref[...], preferred_element_type=F32)


def _outproj(h, o, w, tm=512):
    s, d = h.shape
    tm = min(tm, s)
    w = w.astype(BF16)
    return pl.pallas_call(
        _outproj_kernel,
        out_shape=jax.ShapeDtypeStruct((s, d), F32),
        grid=(s // tm,),
        in_specs=[pl.BlockSpec((tm, d), lambda i: (i, 0)), pl.BlockSpec((tm, o.shape[1]), lambda i: (i, 0)),
                  pl.BlockSpec(w.shape, lambda i: (0, 0))],
        out_specs=pl.BlockSpec((tm, d), lambda i: (i, 0)),
        compiler_params=_cparams(("parallel",)),
        name="outproj",
    )(h, o, w)


def _outproj_groups_kernel(h_ref, o0_ref, o1_ref, o2_ref, l0_ref, l1_ref, l2_ref, w_ref, out_ref):
    l0, l1, l2 = l0_ref[...], l1_ref[...], l2_ref[...]
    mx = jnp.maximum(jnp.maximum(l0, l1), l2)
    e0, e1, e2 = jnp.exp2(l0 - mx), jnp.exp2(l1 - mx), jnp.exp2(l2 - mx)
    tot = e0 + e1 + e2
    acc = h_ref[...]
    gw = o0_ref.shape[1]
    for g, (o_ref, e) in enumerate(((o0_ref, e0), (o1_ref, e1), (o2_ref, e2))):
        og = (o_ref[...].astype(F32) * (e / tot)).astype(BF16)
        acc = acc + jnp.dot(og, w_ref[g * gw:(g + 1) * gw, :], preferred_element_type=F32)
    out_ref[...] = acc


def _outproj_groups(h, os_, ls_, w, tm=512):
    s, d = h.shape
    tm = min(tm, s)
    w = w.astype(BF16)
    gw = os_[0].shape[1]
    row = lambda width: pl.BlockSpec((tm, width), lambda i: (i, 0))
    return pl.pallas_call(
        _outproj_groups_kernel,
        out_shape=jax.ShapeDtypeStruct((s, d), F32),
        grid=(s // tm,),
        in_specs=[row(d)] + [row(gw)] * 6 + [pl.BlockSpec(w.shape, lambda i: (0, 0))],
        out_specs=row(d),
        compiler_params=_cparams(("parallel",)),
        name="outproj_groups",
    )(h, *os_, *ls_, w)


def _route(logits):
    lane = lax.broadcasted_iota(jnp.int32, logits.shape, 1).astype(F32)
    big = 1e6
    gl = jnp.where(lane < MOE_GROUPS, logits, NEG_INF)
    gmax = jnp.max(gl, axis=1, keepdims=True)
    gidx = jnp.min(jnp.where(gl == gmax, lane, big), axis=1, keepdims=True)
    gw = 1.0 / jnp.sum(jnp.exp(gl - gmax), axis=1, keepdims=True)
    lo = MOE_GROUPS + gidx * MOE_PER_GROUP
    el = jnp.where((lane >= lo) & (lane < lo + MOE_PER_GROUP), logits, NEG_INF)
    v1 = jnp.max(el, axis=1, keepdims=True)
    i1 = jnp.min(jnp.where(el == v1, lane, big), axis=1, keepdims=True)
    el2 = jnp.where(lane == i1, NEG_INF, el)
    v2 = jnp.max(el2, axis=1, keepdims=True)
    i2 = jnp.min(jnp.where(el2 == v2, lane, big), axis=1, keepdims=True)
    e2 = jnp.exp(v2 - v1)
    w1 = gw / (1.0 + e2)
    w2 = w1 * e2
    return jnp.where(lane == i1, w1, 0.0) + jnp.where(lane == i2, w2, 0.0)


def _moe_kernel(h_ref, g_ref, wr_ref, br_ref, wg_ref, wu_ref, wd_ref, out_ref, xn_scr, comb_scr, acc_scr):
    e = pl.program_id(1)

    @pl.when(e == 0)
    def _init():
        xn = _rms(h_ref[...], g_ref[...])
        xn_scr[...] = xn.astype(BF16)
        logits = jnp.dot(xn, wr_ref[...], preferred_element_type=F32, precision=lax.Precision.HIGHEST) + br_ref[...]
        comb_scr[...] = _route(logits)
        acc_scr[...] = jnp.zeros_like(acc_scr)

    xn = xn_scr[...]
    gate = jnp.dot(xn, wg_ref[0], preferred_element_type=F32)
    up = jnp.dot(xn, wu_ref[0], preferred_element_type=F32)
    comb = comb_scr[...]
    lane = lax.broadcasted_iota(jnp.int32, comb.shape, 1)
    we = jnp.sum(jnp.where(lane == e + MOE_GROUPS, comb, 0.0), axis=1, keepdims=True)
    act = gate * jax.nn.sigmoid(gate) * up * we
    acc_scr[...] += jnp.dot(act.astype(BF16), wd_ref[0], preferred_element_type=F32)

    @pl.when(e == pl.num_programs(1) - 1)
    def _fin():
        out_ref[...] = h_ref[...] + acc_scr[...]


def _moe(h, g, w_group, b_group, w_expert, b_expert, w_gate, w_up, w_down, tm=1024):
    s, d = h.shape
    tm = min(tm, s)
    ne, _, ff = w_gate.shape
    nr = MOE_GROUPS + MOE_EXPERTS
    wr = jnp.pad(jnp.concatenate([w_group, w_expert], axis=1), ((0, 0), (0, LANES - nr)))
    br = jnp.pad(jnp.concatenate([b_group, b_expert]), (0, LANES - nr)).reshape(1, LANES)
    wg, wu, wd = w_gate.astype(BF16), w_up.astype(BF16), w_down.astype(BF16)
    return pl.pallas_call(
        _moe_kernel,
        out_shape=jax.ShapeDtypeStruct((s, d), F32),
        grid=(s // tm, ne),
        in_specs=[pl.BlockSpec((tm, d), lambda i, e: (i, 0)), pl.BlockSpec((1, d), lambda i, e: (0, 0)),
                  pl.BlockSpec((d, LANES), lambda i, e: (0, 0)), pl.BlockSpec((1, LANES), lambda i, e: (0, 0)),
                  pl.BlockSpec((1, d, ff), lambda i, e: (e, 0, 0)), pl.BlockSpec((1, d, ff), lambda i, e: (e, 0, 0)),
                  pl.BlockSpec((1, ff, d), lambda i, e: (e, 0, 0))],
        out_specs=pl.BlockSpec((tm, d), lambda i, e: (i, 0)),
        scratch_shapes=[pltpu.VMEM((tm, d), BF16), pltpu.VMEM((tm, LANES), F32), pltpu.VMEM((tm, d), F32)],
        compiler_params=_cparams(("parallel", "arbitrary")),
        name="moe",
    )(h, g, wr, br, wg, wu, wd)


def _ple_kernel(h_ref, g_ref, wg_ref, p_ref, wp_ref, gf_ref, out_ref, *, final):
    x = h_ref[...]
    xn = _rms(x, g_ref[...]).astype(BF16)
    gate = jax.nn.sigmoid(jnp.dot(xn, wg_ref[...], preferred_element_type=F32))
    proj = jnp.dot(p_ref[...].astype(BF16), wp_ref[...], preferred_element_type=F32)
    y = x + gate * proj
    if final:
        y = _rms(y, gf_ref[...])
    out_ref[...] = y


def _ple(h, g, wg, p, wp, gf, final, tm=512):
    s, d = h.shape
    tm = min(tm, s)
    wg, wp = wg.astype(BF16), wp.astype(BF16)
    const = lambda a: pl.BlockSpec(a.shape, lambda i: (0, 0))
    return pl.pallas_call(
        functools.partial(_ple_kernel, final=final),
        out_shape=jax.ShapeDtypeStruct((s, d), F32),
        grid=(s // tm,),
        in_specs=[pl.BlockSpec((tm, d), lambda i: (i, 0)), const(g), const(wg),
                  pl.BlockSpec((tm, p.shape[1]), lambda i: (i, 0)), const(wp), const(gf)],
        out_specs=pl.BlockSpec((tm, d), lambda i: (i, 0)),
        compiler_params=_cparams(("parallel",)),
        name="ple",
    )(h, g, wg, p, wp, gf)


PARTIAL_KPERM = ((8, 16), (0, 8), (16, 64))
AXIAL_KPERM = ((16, 32), (0, 16), (48, 64), (32, 48))
B_PAIRS = ((128, 1), (512, 4), (2048, 16))


def _partial_tables(s):
    inv = ROPE_THETA ** (-jnp.arange(0, ROT_DIM, 2, dtype=F32) / ROT_DIM)
    ang = jnp.arange(s).astype(F32)[:, None] * inv[None, :]
    return _rope_tables([ang], s, LOG2E * HEAD_DIM ** -0.5)


def _axial_tables(s):
    half = HEAD_DIM // 2
    inv = AXIAL_THETA ** (-jnp.arange(0, half, 2, dtype=F32) / half)
    t = jnp.arange(s)
    ang_r = (t // GRID_W).astype(F32)[:, None] * inv[None, :]
    ang_c = (t % GRID_W).astype(F32)[:, None] * inv[None, :]
    return _rope_tables([ang_r, ang_c], s, LOG2E * HEAD_DIM ** -0.5)


def _mixer_diff(h, g, w_in, w_out, lam_rows, subln, lam_init, tabs):
    d = h.shape[1]
    heads = d // (2 * HEAD_DIM)
    aw = 2 * heads * HEAD_DIM
    q, kt, v = _project(h, g, w_in[:, :aw], w_in[:, aw:2 * aw], w_in[:, 2 * aw:], tabs,
                        hq=2 * heads, hk=2 * heads, hv=heads, dv=2 * HEAD_DIM, nv=4 * HEAD_DIM,
                        shift=ROT_DIM // 2, kperm=PARTIAL_KPERM, qk_norm=False)
    o = _flash(q, kt, v, units=heads, q_per_unit=2, k_per_unit=2, v_per_unit=1,
               streams=((0, 0, 0), (1, 1, 0)), split=2, tq=512, tk=1024, nv=4 * HEAD_DIM, dv=2 * HEAD_DIM,
               out_w=2 * HEAD_DIM, mode="diff", lam=lam_rows, subln=subln, lam_init=lam_init, name="flash_diff")
    return _outproj(h, o, w_out)


def _mixer_dilated(h, g, w_in, w_out, tabs):
    nh, hg = 12, 4
    bw = nh * HEAD_DIM
    q, kt, v = _project(h, g, w_in[:, :bw], w_in[:, bw:2 * bw], w_in[:, 2 * bw:], tabs,
                        hq=nh, hk=nh, hv=nh, dv=HEAD_DIM, nv=LANES,
                        shift=ROT_DIM // 2, kperm=PARTIAL_KPERM, qk_norm=False)
    os_, ls_ = [], []
    for gi, (win, dil) in enumerate(B_PAIRS):
        halfw = (win // (2 * dil)) * dil
        o, lse = _flash(q, kt, v, units=1, unit0=gi, q_per_unit=hg, k_per_unit=hg, v_per_unit=hg,
                        streams=tuple((j, j, j) for j in range(hg)), tq=1024, tk=max(halfw, 256),
                        nv=LANES, dv=HEAD_DIM, out_w=hg * HEAD_DIM, band=(halfw, dil), mode="lse",
                        name=f"flash_dilated{gi}")
        os_.append(o)
        ls_.append(lse)
    return _outproj_groups(h, os_, ls_, w_out)


def _gqa_project(h, g, w_in, tabs, shift, kperm, qk_norm, qg=None, kg=None):
    nq, nkv = 16, 4
    qd, kvd = nq * HEAD_DIM, nkv * HEAD_DIM
    return _project(h, g, w_in[:, :qd], w_in[:, qd:qd + kvd], w_in[:, qd + kvd:], tabs,
                    hq=nq, hk=nkv, hv=nkv, dv=HEAD_DIM, nv=LANES, shift=shift, kperm=kperm,
                    qk_norm=qk_norm, qg=qg, kg=kg)


def _mixer_window(h, g, w_in, w_out, sink, tabs):
    q, kt, v = _gqa_project(h, g, w_in, tabs, ROT_DIM // 2, PARTIAL_KPERM, False)
    o = _flash(q, kt, v, units=4, q_per_unit=4, k_per_unit=1, v_per_unit=1,
               streams=tuple((j, 0, 0) for j in range(4)), tq=256, tk=256, nv=LANES, dv=HEAD_DIM, out_w=4 * HEAD_DIM, band=(128, 1), mode="gqa",
               sink=sink.reshape(4, 4), name="flash_window")
    return _outproj(h, o, w_out)


def _mixer_axial(h, g, w_in, w_out, q_norm, k_norm, tabs):
    qg = jnp.pad(q_norm, (0, LANES - HEAD_DIM)).reshape(1, LANES)
    kg = k_norm.reshape(HEAD_DIM, 1)
    q, kt, v = _gqa_project(h, g, w_in, tabs, HEAD_DIM // 4, AXIAL_KPERM, True, qg, kg)
    o = _flash(q, kt, v, units=4, q_per_unit=4, k_per_unit=1, v_per_unit=1,
               streams=tuple((j, 0, 0) for j in range(4)), split=1,
               tq=256, tk=1024, nv=LANES, dv=HEAD_DIM, out_w=4 * HEAD_DIM, mode="gqa", name="flash_axial")
    return _outproj(h, o, w_out)


def kernel(x, p, norm_mix, norm_ffn, norm_ple, norm_final, a_w_in, a_w_out, a_lam_q1, a_lam_k1, a_lam_q2, a_lam_k2, a_subln, b_w_in, b_w_out, c_w_in, c_w_out, c_sink, d_w_in, d_w_out, d_q_norm, d_k_norm, moe_w_group, moe_b_group, moe_w_expert, moe_b_expert, moe_w_gate, moe_w_up, moe_w_down, ple_w_gate, ple_w_proj):
    bn, s, d = x.shape
    assert bn == 1
    depth = p.shape[0]
    h = x[0]
    ptabs = _partial_tables(s)
    atabs = _axial_tables(s)
    row = lambda a: a.reshape(1, -1)
    for i in range(depth):
        r, kind = divmod(i, 4)
        g = row(norm_mix[i])
        if kind == 0:
            lam_init = 0.8 - 0.6 * math.exp(-0.3 * i)
            lam_rows = jnp.stack([a_lam_q1[r], a_lam_k1[r], a_lam_q2[r], a_lam_k2[r]])
            h = _mixer_diff(h, g, a_w_in[r], a_w_out[r], lam_rows, row(a_subln[r]), lam_init, ptabs)
        elif kind == 1:
            h = _mixer_dilated(h, g, b_w_in[r], b_w_out[r], ptabs)
        elif kind == 2:
            h = _mixer_window(h, g, c_w_in[r], c_w_out[r], c_sink[r], ptabs)
        else:
            h = _mixer_axial(h, g, d_w_in[r], d_w_out[r], d_q_norm[r], d_k_norm[r], atabs)
        h = _moe(h, row(norm_ffn[i]), moe_w_group[i], moe_b_group[i], moe_w_expert[i], moe_b_expert[i],
                 moe_w_gate[i], moe_w_up[i], moe_w_down[i])
        h = _ple(h, row(norm_ple[i]), ple_w_gate[i], p[i, 0], ple_w_proj[i], row(norm_final), final=(i == depth - 1))
    return h[None]
```

```python
import functools
import math

import jax
import jax.numpy as jnp
from jax import lax
from jax.experimental import pallas as pl
from jax.experimental.pallas import tpu as pltpu

F32 = jnp.float32
BF16 = jnp.bfloat16

HEAD_DIM = 64
LANES = 128
EPS = 1e-6
LOG2E = 1.4426950408889634
NEG_INF = -1e30
ROPE_THETA = 500000.0
ROT_DIM = HEAD_DIM // 4
AXIAL_THETA = 10000.0
GRID_W = 64
MOE_GROUPS = 4
MOE_PER_GROUP = 4
MOE_EXPERTS = 16
VMEM_LIMIT = 56 * 1024 * 1024


def _cparams(sem):
    return pltpu.CompilerParams(dimension_semantics=sem, vmem_limit_bytes=VMEM_LIMIT)


def _rms(x, g):
    return x * lax.rsqrt(jnp.mean(x * x, axis=-1, keepdims=True) + EPS) * g


def _proj_kernel(h_ref, g_ref, wq_ref, wkt_ref, wv_ref, qc_ref, qs1_ref, qs2_ref, kc_ref, ks_ref,
                 qg_ref, kg_ref, q_ref, kt_ref, v_ref, *, hq, hk, nv, dv, shift, kperm, qk_norm):
    xn = _rms(h_ref[...], g_ref[...]).astype(BF16)
    qf = jnp.dot(xn, wq_ref[...], preferred_element_type=F32)
    qc, qs1, qs2 = qc_ref[...], qs1_ref[...], qs2_ref[...]
    for h in range(hq):
        s = qf[:, h * LANES:(h + 1) * LANES]
        if qk_norm:
            ms = jnp.sum(s * s, axis=-1, keepdims=True) * (1.0 / HEAD_DIM)
            s = s * lax.rsqrt(ms + EPS) * qg_ref[...]
        r = s * qc + pltpu.roll(s, LANES - shift, 1) * qs1 + pltpu.roll(s, shift, 1) * qs2
        q_ref[h] = r[:, :HEAD_DIM].astype(BF16)
    kf = lax.dot_general(wkt_ref[...], xn, (((1,), (1,)), ((), ())), preferred_element_type=F32)
    kc, ks = kc_ref[...], ks_ref[...]
    for h in range(hk):
        s = kf[h * HEAD_DIM:(h + 1) * HEAD_DIM, :]
        if qk_norm:
            ms = jnp.sum(s * s, axis=0, keepdims=True) * (1.0 / HEAD_DIM)
            s = s * lax.rsqrt(ms + EPS) * kg_ref[...]
        partner = jnp.concatenate([s[a:b] for a, b in kperm], axis=0)
        kt_ref[h] = (s * kc + partner * ks).astype(BF16)
    vf = jnp.dot(xn, wv_ref[...], preferred_element_type=F32)
    lane = lax.broadcasted_iota(jnp.int32, vf.shape, 1)
    v_ref[...] = jnp.where((lane & (nv - 1)) == dv, 1.0, vf).astype(BF16)


def _project(h, g, wq, wk, wv, tabs, *, hq, hk, hv, dv, nv, shift, kperm, qk_norm, qg=None, kg=None, tm=512):
    s, d = h.shape
    tm = min(tm, s)
    wq_p = jnp.pad(wq.reshape(d, hq, HEAD_DIM), ((0, 0), (0, 0), (0, LANES - HEAD_DIM))).reshape(d, hq * LANES).astype(BF16)
    wkt = wk.T.astype(BF16)
    wv_p = jnp.pad(wv.reshape(d, hv, dv), ((0, 0), (0, 0), (0, nv - dv))).reshape(d, hv * nv).astype(BF16)
    qc, qs1, qs2, kc, ks = tabs
    if qg is None:
        qg = jnp.ones((1, LANES), F32)
        kg = jnp.ones((HEAD_DIM, 1), F32)
    full = lambda a: pl.BlockSpec(a.shape, lambda i: (0,) * a.ndim)
    kern = functools.partial(_proj_kernel, hq=hq, hk=hk, nv=nv, dv=dv, shift=shift, kperm=kperm, qk_norm=qk_norm)
    return pl.pallas_call(
        kern,
        out_shape=(jax.ShapeDtypeStruct((hq, s, HEAD_DIM), BF16),
                   jax.ShapeDtypeStruct((hk, HEAD_DIM, s), BF16),
                   jax.ShapeDtypeStruct((s, hv * nv), BF16)),
        grid=(s // tm,),
        in_specs=[pl.BlockSpec((tm, d), lambda i: (i, 0)), full(g), full(wq_p), full(wkt), full(wv_p),
                  pl.BlockSpec((tm, LANES), lambda i: (i, 0)), pl.BlockSpec((tm, LANES), lambda i: (i, 0)),
                  pl.BlockSpec((tm, LANES), lambda i: (i, 0)),
                  pl.BlockSpec((HEAD_DIM, tm), lambda i: (0, i)), pl.BlockSpec((HEAD_DIM, tm), lambda i: (0, i)),
                  full(qg), full(kg)],
        out_specs=(pl.BlockSpec((hq, tm, HEAD_DIM), lambda i: (0, i, 0)),
                   pl.BlockSpec((hk, HEAD_DIM, tm), lambda i: (0, 0, i)),
                   pl.BlockSpec((tm, hv * nv), lambda i: (i, 0))),
        compiler_params=_cparams(("parallel",)),
        name="proj",
    )(h, g, wq_p, wkt, wv_p, qc, qs1, qs2, kc, ks, qg, kg)


def _rope_tables(ang_list, s, scale):
    cs, sn1, sn2, ksn = [], [], [], []
    used = 0
    for ang in ang_list:
        c, sn = jnp.cos(ang), jnp.sin(ang)
        z = jnp.zeros_like(sn)
        cs += [c, c]
        sn1 += [-sn, z]
        sn2 += [z, sn]
        ksn += [-sn, sn]
        used += 2 * ang.shape[1]
    rest = HEAD_DIM - used
    ones, zeros = jnp.ones((s, rest), F32), jnp.zeros((s, rest), F32)
    c64 = jnp.concatenate(cs + [ones], axis=1)
    pad = jnp.zeros((s, LANES - HEAD_DIM), F32)
    qc = jnp.concatenate([c64, pad], axis=1) * scale
    qs1 = jnp.concatenate(sn1 + [zeros, pad], axis=1) * scale
    qs2 = jnp.concatenate(sn2 + [zeros, pad], axis=1) * scale
    kc = c64.T
    ks = jnp.concatenate(ksn + [zeros], axis=1).T
    return qc, qs1, qs2, kc, ks


def _lane_tile(x, width):
    reps = width // LANES
    return x if reps == 1 else jnp.concatenate([x] * reps, axis=1)


def _flash_kernel(*refs, streams, split, tq, tk, nv, dv, band, n_kblocks, mode, lam_init, has_sink):
    it = iter(refs)
    q_ref, kt_ref, v_ref = next(it), next(it), next(it)
    sink_ref = next(it) if has_sink else None
    lam_ref = subln_ref = None
    if mode == "diff":
        lam_ref, subln_ref = next(it), next(it)
    o_ref = next(it)
    lse_ref = next(it) if mode == "lse" else None
    m_scr, acc_scr = next(it), next(it)

    qb, kb = pl.program_id(1), pl.program_id(2)
    nsteps = pl.num_programs(2)
    cr = tq // split

    @pl.when(kb == 0)
    def _init():
        for si, (qi, _, _) in enumerate(streams):
            if has_sink:
                m_scr[si] = jnp.full((tq, LANES), LOG2E, F32) * sink_ref[pl.program_id(0), qi]
                lane = lax.broadcasted_iota(jnp.int32, (tq, nv), 1)
                acc_scr[si] = jnp.where(lane == dv, 1.0, 0.0).astype(F32)
            else:
                m_scr[si] = jnp.full((tq, LANES), NEG_INF, F32)
                acc_scr[si] = jnp.zeros((tq, nv), F32)

    def _step():
        chunks = [(si, c * cr, qi, ki, vi) for si, (qi, ki, vi) in enumerate(streams) for c in range(split)]
        scores = [jnp.dot(q_ref[qi, r0:r0 + cr, :], kt_ref[ki], preferred_element_type=F32)
                  for _, r0, qi, ki, _ in chunks]
        for sc, (si, r0, _, _, vi) in zip(scores, chunks):
            if band is not None:
                halfw, dil, off = band
                row = lax.broadcasted_iota(jnp.int32, (cr, tk), 0) + r0
                col = lax.broadcasted_iota(jnp.int32, (cr, tk), 1)
                diff = col - row + ((kb - off) * tk)
                valid = (jnp.abs(diff) <= halfw) & ((diff & (dil - 1)) == 0)
                sc = jnp.where(valid, sc, NEG_INF)
            m_old = m_scr[si, r0:r0 + cr, :]
            m_new = jnp.maximum(m_old, jnp.broadcast_to(jnp.max(sc, axis=1, keepdims=True), (cr, LANES)))
            p = jnp.exp2(sc - _lane_tile(m_new, tk))
            if band is not None:
                p = jnp.where(valid, p, 0.0)
            alpha = jnp.exp2(m_old - m_new)
            pv = jnp.dot(p.astype(BF16), v_ref[:, vi * nv:(vi + 1) * nv], preferred_element_type=F32)
            acc_scr[si, r0:r0 + cr, :] = acc_scr[si, r0:r0 + cr, :] * _lane_tile(alpha, nv) + pv
            m_scr[si, r0:r0 + cr, :] = m_new

    if band is None:
        _step()
    else:
        _, _, off = band
        kabs = qb * (tq // tk) + kb - off
        pl.when((kabs >= 0) & (kabs < n_kblocks))(_step)

    @pl.when(kb == nsteps - 1)
    def _fin():
        if mode == "diff":
            a0, a1 = acc_scr[0], acc_scr[1]
            lam_rows = lam_ref[...]
            lam = (jnp.exp(jnp.sum(lam_rows[0:1] * lam_rows[1:2], axis=1, keepdims=True))
                   - jnp.exp(jnp.sum(lam_rows[2:3] * lam_rows[3:4], axis=1, keepdims=True)) + lam_init)
            o = a0[:, :dv] / a0[:, dv:dv + 1] - lam * (a1[:, :dv] / a1[:, dv:dv + 1])
            o = _rms(o, subln_ref[...]) * (1.0 - lam_init)
            o_ref[...] = o.astype(o_ref.dtype)
        else:
            for si in range(len(streams)):
                a = acc_scr[si]
                l = a[:, dv:dv + 1]
                o_ref[:, si * dv:(si + 1) * dv] = (a[:, :dv] / l).astype(o_ref.dtype)
                if mode == "lse":
                    lse_ref[:, si * dv:(si + 1) * dv] = m_scr[si][:, :dv] + jnp.log2(l)


def _flash(q, kt, v, *, units, q_per_unit, k_per_unit, v_per_unit, streams, tq, tk, nv, dv, out_w,
           split=1, unit0=0, band=None, mode="gqa", sink=None, lam=None, subln=None, lam_init=0.0, name="flash"):
    s = q.shape[1]
    tq, tk = min(tq, s), min(tk, s)
    assert s % tq == 0 and s % tk == 0 and (band is None or tq % tk == 0)
    n_kblocks = s // tk
    if band is None:
        nsteps = n_kblocks
        kmap = lambda u, i, j: (u + unit0, 0, j)
        vmap = lambda u, i, j: (j, u + unit0)
    else:
        halfw = band[0]
        off = -(-halfw // tk)
        nsteps = tq // tk + 2 * off
        band = (band[0], band[1], off)
        kidx = lambda i, j: jnp.clip(i * (tq // tk) + j - off, 0, n_kblocks - 1)
        kmap = lambda u, i, j: (u + unit0, 0, kidx(i, j))
        vmap = lambda u, i, j: (kidx(i, j), u + unit0)
    in_specs = [pl.BlockSpec((q_per_unit, tq, HEAD_DIM), lambda u, i, j: (u + unit0, i, 0)),
                pl.BlockSpec((k_per_unit, HEAD_DIM, tk), kmap),
                pl.BlockSpec((tk, v_per_unit * nv), vmap)]
    args = [q, kt, v]
    if sink is not None:
        in_specs.append(pl.BlockSpec(memory_space=pltpu.SMEM))
        args.append(sink)
    if mode == "diff":
        in_specs += [pl.BlockSpec(lam.shape, lambda u, i, j: (0, 0)), pl.BlockSpec(subln.shape, lambda u, i, j: (0, 0))]
        args += [lam, subln]
    out_shape = [jax.ShapeDtypeStruct((s, units * out_w), BF16)]
    out_specs = [pl.BlockSpec((tq, out_w), lambda u, i, j: (i, u))]
    if mode == "lse":
        out_shape.append(jax.ShapeDtypeStruct((s, units * out_w), F32))
        out_specs.append(pl.BlockSpec((tq, out_w), lambda u, i, j: (i, u)))
    assert tq % split == 0
    kern = functools.partial(_flash_kernel, streams=streams, split=split, tq=tq, tk=tk, nv=nv, dv=dv, band=band,
                             n_kblocks=n_kblocks, mode=mode, lam_init=lam_init, has_sink=sink is not None)
    res = pl.pallas_call(
        kern,
        out_shape=tuple(out_shape),
        grid=(units, s // tq, nsteps),
        in_specs=in_specs,
        out_specs=tuple(out_specs),
        scratch_shapes=[pltpu.VMEM((len(streams), tq, LANES), F32), pltpu.VMEM((len(streams), tq, nv), F32)],
        compiler_params=_cparams(("parallel", "parallel", "arbitrary")),
        name=name,
    )(*args)
    return res if mode == "lse" else res[0]


def _outproj_kernel(h_ref, o_ref, w_ref, out_ref):
    out_ref[...] = h_ref[...] + jnp.dot(o_ref[...], w_ref[...], preferred_element_type=F32)


def _outproj(h, o, w, tm=512):
    s, d = h.shape
    tm = min(tm, s)
    w = w.astype(BF16)
    return pl.pallas_call(
        _outproj_kernel,
        out_shape=jax.ShapeDtypeStruct((s, d), F32),
        grid=(s // tm,),
        in_specs=[pl.BlockSpec((tm, d), lambda i: (i, 0)), pl.BlockSpec((tm, o.shape[1]), lambda i: (i, 0)),
                  pl.BlockSpec(w.shape, lambda i: (0, 0))],
        out_specs=pl.BlockSpec((tm, d), lambda i: (i, 0)),
        compiler_params=_cparams(("parallel",)),
        name="outproj",
    )(h, o, w)


def _outproj_groups_kernel(h_ref, o0_ref, o1_ref, o2_ref, l0_ref, l1_ref, l2_ref, w_ref, out_ref):
    l0, l1, l2 = l0_ref[...], l1_ref[...], l2_ref[...]
    mx = jnp.maximum(jnp.maximum(l0, l1), l2)
    e0, e1, e2 = jnp.exp2(l0 - mx), jnp.exp2(l1 - mx), jnp.exp2(l2 - mx)
    tot = e0 + e1 + e2
    acc = h_ref[...]
    gw = o0_ref.shape[1]
    for g, (o_ref, e) in enumerate(((o0_ref, e0), (o1_ref, e1), (o2_ref, e2))):
        og = (o_ref[...].astype(F32) * (e / tot)).astype(BF16)
        acc = acc + jnp.dot(og, w_ref[g * gw:(g + 1) * gw, :], preferred_element_type=F32)
    out_ref[...] = acc


def _outproj_groups(h, os_, ls_, w, tm=512):
    s, d = h.shape
    tm = min(tm, s)
    w = w.astype(BF16)
    gw = os_[0].shape[1]
    row = lambda width: pl.BlockSpec((tm, width), lambda i: (i, 0))
    return pl.pallas_call(
        _outproj_groups_kernel,
        out_shape=jax.ShapeDtypeStruct((s, d), F32),
        grid=(s // tm,),
        in_specs=[row(d)] + [row(gw)] * 6 + [pl.BlockSpec(w.shape, lambda i: (0, 0))],
        out_specs=row(d),
        compiler_params=_cparams(("parallel",)),
        name="outproj_groups",
    )(h, *os_, *ls_, w)


def _route(logits):
    lane = lax.broadcasted_iota(jnp.int32, logits.shape, 1).astype(F32)
    big = 1e6
    gl = jnp.where(lane < MOE_GROUPS, logits, NEG_INF)
    gmax = jnp.max(gl, axis=1, keepdims=True)
    gidx = jnp.min(jnp.where(gl == gmax, lane, big), axis=1, keepdims=True)
    gw = 1.0 / jnp.sum(jnp.exp(gl - gmax), axis=1, keepdims=True)
    lo = MOE_GROUPS + gidx * MOE_PER_GROUP
    el = jnp.where((lane >= lo) & (lane < lo + MOE_PER_GROUP), logits, NEG_INF)
    v1 = jnp.max(el, axis=1, keepdims=True)
    i1 = jnp.min(jnp.where(el == v1, lane, big), axis=1, keepdims=True)
    el2 = jnp.where(lane == i1, NEG_INF, el)
    v2 = jnp.max(el2, axis=1, keepdims=True)
    i2 = jnp.min(jnp.where(el2 == v2, lane, big), axis=1, keepdims=True)
    e2 = jnp.exp(v2 - v1)
    w1 = gw / (1.0 + e2)
    w2 = w1 * e2
    return jnp.where(lane == i1, w1, 0.0) + jnp.where(lane == i2, w2, 0.0)


def _moe_kernel(h_ref, g_ref, wr_ref, br_ref, wg_ref, wu_ref, wd_ref, out_ref, xn_scr, comb_scr, acc_scr):
    e = pl.program_id(1)

    @pl.when(e == 0)
    def _init():
        xn = _rms(h_ref[...], g_ref[...])
        xn_scr[...] = xn.astype(BF16)
        logits = jnp.dot(xn, wr_ref[...], preferred_element_type=F32, precision=lax.Precision.HIGHEST) + br_ref[...]
        comb_scr[...] = _route(logits)
        acc_scr[...] = jnp.zeros_like(acc_scr)

    xn = xn_scr[...]
    gate = jnp.dot(xn, wg_ref[0], preferred_element_type=F32)
    up = jnp.dot(xn, wu_ref[0], preferred_element_type=F32)
    comb = comb_scr[...]
    lane = lax.broadcasted_iota(jnp.int32, comb.shape, 1)
    we = jnp.sum(jnp.where(lane == e + MOE_GROUPS, comb, 0.0), axis=1, keepdims=True)
    act = gate * jax.nn.sigmoid(gate) * up * we
    acc_scr[...] += jnp.dot(act.astype(BF16), wd_ref[0], preferred_element_type=F32)

    @pl.when(e == pl.num_programs(1) - 1)
    def _fin():
        out_ref[...] = h_ref[...] + acc_scr[...]


def _moe(h, g, w_group, b_group, w_expert, b_expert, w_gate, w_up, w_down, tm=1024):
    s, d = h.shape
    tm = min(tm, s)
    ne, _, ff = w_gate.shape
    nr = MOE_GROUPS + MOE_EXPERTS
    wr = jnp.pad(jnp.concatenate([w_group, w_expert], axis=1), ((0, 0), (0, LANES - nr)))
    br = jnp.pad(jnp.concatenate([b_group, b_expert]), (0, LANES - nr)).reshape(1, LANES)
    wg, wu, wd = w_gate.astype(BF16), w_up.astype(BF16), w_down.astype(BF16)
    return pl.pallas_call(
        _moe_kernel,
        out_shape=jax.ShapeDtypeStruct((s, d), F32),
        grid=(s // tm, ne),
        in_specs=[pl.BlockSpec((tm, d), lambda i, e: (i, 0)), pl.BlockSpec((1, d), lambda i, e: (0, 0)),
                  pl.BlockSpec((d, LANES), lambda i, e: (0, 0)), pl.BlockSpec((1, LANES), lambda i, e: (0, 0)),
                  pl.BlockSpec((1, d, ff), lambda i, e: (e, 0, 0)), pl.BlockSpec((1, d, ff), lambda i, e: (e, 0, 0)),
                  pl.BlockSpec((1, ff, d), lambda i, e: (e, 0, 0))],
        out_specs=pl.BlockSpec((tm, d), lambda i, e: (i, 0)),
        scratch_shapes=[pltpu.VMEM((tm, d), BF16), pltpu.VMEM((tm, LANES), F32), pltpu.VMEM((tm, d), F32)],
        compiler_params=_cparams(("parallel", "arbitrary")),
        name="moe",
    )(h, g, wr, br, wg, wu, wd)


def _ple_kernel(h_ref, g_ref, wg_ref, p_ref, wp_ref, gf_ref, out_ref, *, final):
    x = h_ref[...]
    xn = _rms(x, g_ref[...]).astype(BF16)
    gate = jax.nn.sigmoid(jnp.dot(xn, wg_ref[...], preferred_element_type=F32))
    proj = jnp.dot(p_ref[...].astype(BF16), wp_ref[...], preferred_element_type=F32)
    y = x + gate * proj
    if final:
        y = _rms(y, gf_ref[...])
    out_ref[...] = y


def _ple(h, g, wg, p, wp, gf, final, tm=512):
    s, d = h.shape
    tm = min(tm, s)
    wg, wp = wg.astype(BF16), wp.astype(BF16)
    const = lambda a: pl.BlockSpec(a.shape, lambda i: (0, 0))
    return pl.pallas_call(
        functools.partial(_ple_kernel, final=final),
        out_shape=jax.ShapeDtypeStruct((s, d), F32),
        grid=(s // tm,),
        in_specs=[pl.BlockSpec((tm, d), lambda i: (i, 0)), const(g), const(wg),
                  pl.BlockSpec((tm, p.shape[1]), lambda i: (i, 0)), const(wp), const(gf)],
        out_specs=pl.BlockSpec((tm, d), lambda i: (i, 0)),
        compiler_params=_cparams(("parallel",)),
        name="ple",
    )(h, g, wg, p, wp, gf)


PARTIAL_KPERM = ((8, 16), (0, 8), (16, 64))
AXIAL_KPERM = ((16, 32), (0, 16), (48, 64), (32, 48))
B_PAIRS = ((128, 1), (512, 4), (2048, 16))


def _partial_tables(s):
    inv = ROPE_THETA ** (-jnp.arange(0, ROT_DIM, 2, dtype=F32) / ROT_DIM)
    ang = jnp.arange(s).astype(F32)[:, None] * inv[None, :]
    return _rope_tables([ang], s, LOG2E * HEAD_DIM ** -0.5)


def _axial_tables(s):
    half = HEAD_DIM // 2
    inv = AXIAL_THETA ** (-jnp.arange(0, half, 2, dtype=F32) / half)
    t = jnp.arange(s)
    ang_r = (t // GRID_W).astype(F32)[:, None] * inv[None, :]
    ang_c = (t % GRID_W).astype(F32)[:, None] * inv[None, :]
    return _rope_tables([ang_r, ang_c], s, LOG2E * HEAD_DIM ** -0.5)


def _mixer_diff(h, g, w_in, w_out, lam_rows, subln, lam_init, tabs):
    d = h.shape[1]
    heads = d // (2 * HEAD_DIM)
    aw = 2 * heads * HEAD_DIM
    q, kt, v = _project(h, g, w_in[:, :aw], w_in[:, aw:2 * aw], w_in[:, 2 * aw:], tabs,
                        hq=2 * heads, hk=2 * heads, hv=heads, dv=2 * HEAD_DIM, nv=4 * HEAD_DIM,
                        shift=ROT_DIM // 2, kperm=PARTIAL_KPERM, qk_norm=False)
    o = _flash(q, kt, v, units=heads, q_per_unit=2, k_per_unit=2, v_per_unit=1,
               streams=((0, 0, 0), (1, 1, 0)), split=4, tq=1024, tk=1024, nv=4 * HEAD_DIM, dv=2 * HEAD_DIM,
               out_w=2 * HEAD_DIM, mode="diff", lam=lam_rows, subln=subln, lam_init=lam_init, name="flash_diff")
    return _outproj(h, o, w_out)


def _mixer_dilated(h, g, w_in, w_out, tabs):
    nh, hg = 12, 4
    bw = nh * HEAD_DIM
    q, kt, v = _project(h, g, w_in[:, :bw], w_in[:, bw:2 * bw], w_in[:, 2 * bw:], tabs,
                        hq=nh, hk=nh, hv=nh, dv=HEAD_DIM, nv=LANES,
                        shift=ROT_DIM // 2, kperm=PARTIAL_KPERM, qk_norm=False)
    os_, ls_ = [], []
    for gi, (win, dil) in enumerate(B_PAIRS):
        halfw = (win // (2 * dil)) * dil
        o, lse = _flash(q, kt, v, units=1, unit0=gi, q_per_unit=hg, k_per_unit=hg, v_per_unit=hg,
                        streams=tuple((j, j, j) for j in range(hg)), tq=1024, tk=max(halfw, 256),
                        nv=LANES, dv=HEAD_DIM, out_w=hg * HEAD_DIM, band=(halfw, dil), mode="lse",
                        name=f"flash_dilated{gi}")
        os_.append(o)
        ls_.append(lse)
    return _outproj_groups(h, os_, ls_, w_out)


def _gqa_project(h, g, w_in, tabs, shift, kperm, qk_norm, qg=None, kg=None):
    nq, nkv = 16, 4
    qd, kvd = nq * HEAD_DIM, nkv * HEAD_DIM
    return _project(h, g, w_in[:, :qd], w_in[:, qd:qd + kvd], w_in[:, qd + kvd:], tabs,
                    hq=nq, hk=nkv, hv=nkv, dv=HEAD_DIM, nv=LANES, shift=shift, kperm=kperm,
                    qk_norm=qk_norm, qg=qg, kg=kg)


def _mixer_window(h, g, w_in, w_out, sink, tabs):
    q, kt, v = _gqa_project(h, g, w_in, tabs, ROT_DIM // 2, PARTIAL_KPERM, False)
    o = _flash(q, kt, v, units=4, q_per_unit=4, k_per_unit=1, v_per_unit=1,
               streams=tuple((j, 0, 0) for j in range(4)), tq=256, tk=256, nv=LANES, dv=HEAD_DIM, out_w=4 * HEAD_DIM, band=(128, 1), mode="gqa",
               sink=sink.reshape(4, 4), name="flash_window")
    return _outproj(h, o, w_out)


def _mixer_axial(h, g, w_in, w_out, q_norm, k_norm, tabs):
    qg = jnp.pad(q_norm, (0, LANES - HEAD_DIM)).reshape(1, LANES)
    kg = k_norm.reshape(HEAD_DIM, 1)
    q, kt, v = _gqa_project(h, g, w_in, tabs, HEAD_DIM // 4, AXIAL_KPERM, True, qg, kg)
    o = _flash(q, kt, v, units=4, q_per_unit=4, k_per_unit=1, v_per_unit=1,
               streams=tuple((j, 0, 0) for j in range(4)), split=4,
               tq=1024, tk=1024, nv=LANES, dv=HEAD_DIM, out_w=4 * HEAD_DIM, mode="gqa", name="flash_axial")
    return _outproj(h, o, w_out)


def kernel(x, p, norm_mix, norm_ffn, norm_ple, norm_final, a_w_in, a_w_out, a_lam_q1, a_lam_k1, a_lam_q2, a_lam_k2, a_subln, b_w_in, b_w_out, c_w_in, c_w_out, c_sink, d_w_in, d_w_out, d_q_norm, d_k_norm, moe_w_group, moe_b_group, moe_w_expert, moe_b_expert, moe_w_gate, moe_w_up, moe_w_down, ple_w_gate, ple_w_proj):
    bn, s, d = x.shape
    assert bn == 1
    depth = p.shape[0]
    h = x[0]
    ptabs = _partial_tables(s)
    atabs = _axial_tables(s)
    row = lambda a: a.reshape(1, -1)
    for i in range(depth):
        r, kind = divmod(i, 4)
        g = row(norm_mix[i])
        if kind == 0:
            lam_init = 0.8 - 0.6 * math.exp(-0.3 * i)
            lam_rows = jnp.stack([a_lam_q1[r], a_lam_k1[r], a_lam_q2[r], a_lam_k2[r]])
            h = _mixer_diff(h, g, a_w_in[r], a_w_out[r], lam_rows, row(a_subln[r]), lam_init, ptabs)
        elif kind == 1:
            h = _mixer_dilated(h, g, b_w_in[r], b_w_out[r], ptabs)
        elif kind == 2:
            h = _mixer_window(h, g, c_w_in[r], c_w_out[r], c_sink[r], ptabs)
        else:
            h = _mixer_axial(h, g, d_w_in[r], d_w_out[r], d_q_norm[r], d_k_norm[r], atabs)
        h = _moe(h, row(norm_ffn[i]), moe_w_group[i], moe_b_group[i], moe_w_expert[i], moe_b_expert[i],
                 moe_w_gate[i], moe_w_up[i], moe_w_down[i])
        h = _ple(h, row(norm_ple[i]), ple_w_gate[i], p[i, 0], ple_w_proj[i], row(norm_final), final=(i == depth - 1))
    return h[None]
```

```python
import functools
import math

import jax
import jax.numpy as jnp
from jax import lax
from jax.experimental import pallas as pl
from jax.experimental.pallas import tpu as pltpu

F32 = jnp.float32
BF16 = jnp.bfloat16

HEAD_DIM = 64
LANES = 128
EPS = 1e-6
LOG2E = 1.4426950408889634
NEG_INF = -1e30
ROPE_THETA = 500000.0
ROT_DIM = HEAD_DIM // 4
AXIAL_THETA = 10000.0
GRID_W = 64
MOE_GROUPS = 4
MOE_PER_GROUP = 4
MOE_EXPERTS = 16
VMEM_LIMIT = 56 * 1024 * 1024


def _cparams(sem):
    return pltpu.CompilerParams(dimension_semantics=sem, vmem_limit_bytes=VMEM_LIMIT)


def _rms(x, g):
    return x * lax.rsqrt(jnp.mean(x * x, axis=-1, keepdims=True) + EPS) * g


def _proj_kernel(h_ref, g_ref, wq_ref, wkt_ref, wv_ref, qc_ref, qs1_ref, qs2_ref, kc_ref, ks_ref,
                 qg_ref, kg_ref, q_ref, kt_ref, v_ref, *, hq, hk, nv, dv, shift, kperm, qk_norm):
    xn = _rms(h_ref[...], g_ref[...]).astype(BF16)
    qf = jnp.dot(xn, wq_ref[...], preferred_element_type=F32)
    qc, qs1, qs2 = qc_ref[...], qs1_ref[...], qs2_ref[...]
    for h in range(hq):
        s = qf[:, h * LANES:(h + 1) * LANES]
        if qk_norm:
            ms = jnp.sum(s * s, axis=-1, keepdims=True) * (1.0 / HEAD_DIM)
            s = s * lax.rsqrt(ms + EPS) * qg_ref[...]
        r = s * qc + pltpu.roll(s, LANES - shift, 1) * qs1 + pltpu.roll(s, shift, 1) * qs2
        q_ref[h] = r[:, :HEAD_DIM].astype(BF16)
    kf = lax.dot_general(wkt_ref[...], xn, (((1,), (1,)), ((), ())), preferred_element_type=F32)
    kc, ks = kc_ref[...], ks_ref[...]
    for h in range(hk):
        s = kf[h * HEAD_DIM:(h + 1) * HEAD_DIM, :]
        if qk_norm:
            ms = jnp.sum(s * s, axis=0, keepdims=True) * (1.0 / HEAD_DIM)
            s = s * lax.rsqrt(ms + EPS) * kg_ref[...]
        partner = jnp.concatenate([s[a:b] for a, b in kperm], axis=0)
        kt_ref[h] = (s * kc + partner * ks).astype(BF16)
    vf = jnp.dot(xn, wv_ref[...], preferred_element_type=F32)
    lane = lax.broadcasted_iota(jnp.int32, vf.shape, 1)
    v_ref[...] = jnp.where((lane & (nv - 1)) == dv, 1.0, vf).astype(BF16)


def _project(h, g, wq, wk, wv, tabs, *, hq, hk, hv, dv, nv, shift, kperm, qk_norm, qg=None, kg=None, tm=512):
    s, d = h.shape
    tm = min(tm, s)
    wq_p = jnp.pad(wq.reshape(d, hq, HEAD_DIM), ((0, 0), (0, 0), (0, LANES - HEAD_DIM))).reshape(d, hq * LANES).astype(BF16)
    wkt = wk.T.astype(BF16)
    wv_p = jnp.pad(wv.reshape(d, hv, dv), ((0, 0), (0, 0), (0, nv - dv))).reshape(d, hv * nv).astype(BF16)
    qc, qs1, qs2, kc, ks = tabs
    if qg is None:
        qg = jnp.ones((1, LANES), F32)
        kg = jnp.ones((HEAD_DIM, 1), F32)
    full = lambda a: pl.BlockSpec(a.shape, lambda i: (0,) * a.ndim)
    kern = functools.partial(_proj_kernel, hq=hq, hk=hk, nv=nv, dv=dv, shift=shift, kperm=kperm, qk_norm=qk_norm)
    return pl.pallas_call(
        kern,
        out_shape=(jax.ShapeDtypeStruct((hq, s, HEAD_DIM), BF16),
                   jax.ShapeDtypeStruct((hk, HEAD_DIM, s), BF16),
                   jax.ShapeDtypeStruct((s, hv * nv), BF16)),
        grid=(s // tm,),
        in_specs=[pl.BlockSpec((tm, d), lambda i: (i, 0)), full(g), full(wq_p), full(wkt), full(wv_p),
                  pl.BlockSpec((tm, LANES), lambda i: (i, 0)), pl.BlockSpec((tm, LANES), lambda i: (i, 0)),
                  pl.BlockSpec((tm, LANES), lambda i: (i, 0)),
                  pl.BlockSpec((HEAD_DIM, tm), lambda i: (0, i)), pl.BlockSpec((HEAD_DIM, tm), lambda i: (0, i)),
                  full(qg), full(kg)],
        out_specs=(pl.BlockSpec((hq, tm, HEAD_DIM), lambda i: (0, i, 0)),
                   pl.BlockSpec((hk, HEAD_DIM, tm), lambda i: (0, 0, i)),
                   pl.BlockSpec((tm, hv * nv), lambda i: (i, 0))),
        compiler_params=_cparams(("parallel",)),
        name="proj",
    )(h, g, wq_p, wkt, wv_p, qc, qs1, qs2, kc, ks, qg, kg)


def _rope_tables(ang_list, s, scale):
    cs, sn1, sn2, ksn = [], [], [], []
    used = 0
    for ang in ang_list:
        c, sn = jnp.cos(ang), jnp.sin(ang)
        z = jnp.zeros_like(sn)
        cs += [c, c]
        sn1 += [-sn, z]
        sn2 += [z, sn]
        ksn += [-sn, sn]
        used += 2 * ang.shape[1]
    rest = HEAD_DIM - used
    ones, zeros = jnp.ones((s, rest), F32), jnp.zeros((s, rest), F32)
    c64 = jnp.concatenate(cs + [ones], axis=1)
    pad = jnp.zeros((s, LANES - HEAD_DIM), F32)
    qc = jnp.concatenate([c64, pad], axis=1) * scale
    qs1 = jnp.concatenate(sn1 + [zeros, pad], axis=1) * scale
    qs2 = jnp.concatenate(sn2 + [zeros, pad], axis=1) * scale
    kc = c64.T
    ks = jnp.concatenate(ksn + [zeros], axis=1).T
    return qc, qs1, qs2, kc, ks


def _lane_tile(x, width):
    reps = width // LANES
    return x if reps == 1 else jnp.concatenate([x] * reps, axis=1)


def _flash_kernel(*refs, streams, split, tq, tk, nv, dv, band, n_kblocks, mode, lam_init, has_sink):
    it = iter(refs)
    q_ref, kt_ref, v_ref = next(it), next(it), next(it)
    sink_ref = next(it) if has_sink else None
    lam_ref = subln_ref = None
    if mode == "diff":
        lam_ref, subln_ref = next(it), next(it)
    o_ref = next(it)
    lse_ref = next(it) if mode == "lse" else None
    m_scr, acc_scr = next(it), next(it)

    qb, kb = pl.program_id(1), pl.program_id(2)
    nsteps = pl.num_programs(2)
    cr = tq // split

    @pl.when(kb == 0)
    def _init():
        for si, (qi, _, _) in enumerate(streams):
            if has_sink:
                m_scr[si] = jnp.full((tq, LANES), LOG2E, F32) * sink_ref[pl.program_id(0), qi]
                lane = lax.broadcasted_iota(jnp.int32, (tq, nv), 1)
                acc_scr[si] = jnp.where(lane == dv, 1.0, 0.0).astype(F32)
            else:
                m_scr[si] = jnp.full((tq, LANES), NEG_INF, F32)
                acc_scr[si] = jnp.zeros((tq, nv), F32)

    def _step():
        chunks = [(si, c * cr, qi, ki, vi) for si, (qi, ki, vi) in enumerate(streams) for c in range(split)]
        scores = [jnp.dot(q_ref[qi, r0:r0 + cr, :], kt_ref[ki], preferred_element_type=F32)
                  for _, r0, qi, ki, _ in chunks]
        for sc, (si, r0, _, _, vi) in zip(scores, chunks):
            if band is not None:
                halfw, dil, off = band
                row = lax.broadcasted_iota(jnp.int32, (cr, tk), 0) + r0
                col = lax.broadcasted_iota(jnp.int32, (cr, tk), 1)
                diff = col - row + ((kb - off) * tk)
                valid = (jnp.abs(diff) <= halfw) & ((diff & (dil - 1)) == 0)
                sc = jnp.where(valid, sc, NEG_INF)
            m_old = m_scr[si, r0:r0 + cr, :]
            m_new = jnp.maximum(m_old, jnp.broadcast_to(jnp.max(sc, axis=1, keepdims=True), (cr, LANES)))
            p = jnp.exp2(sc - _lane_tile(m_new, tk))
            if band is not None:
                p = jnp.where(valid, p, 0.0)
            alpha = jnp.exp2(m_old - m_new)
            pv = jnp.dot(p.astype(BF16), v_ref[:, vi * nv:(vi + 1) * nv], preferred_element_type=F32)
            acc_scr[si, r0:r0 + cr, :] = acc_scr[si, r0:r0 + cr, :] * _lane_tile(alpha, nv) + pv
            m_scr[si, r0:r0 + cr, :] = m_new

    if band is None:
        _step()
    else:
        _, _, off = band
        kabs = qb * (tq // tk) + kb - off
        pl.when((kabs >= 0) & (kabs < n_kblocks))(_step)

    @pl.when(kb == nsteps - 1)
    def _fin():
        if mode == "diff":
            a0, a1 = acc_scr[0], acc_scr[1]
            lam_rows = lam_ref[...]
            lam = (jnp.exp(jnp.sum(lam_rows[0:1] * lam_rows[1:2], axis=1, keepdims=True))
                   - jnp.exp(jnp.sum(lam_rows[2:3] * lam_rows[3:4], axis=1, keepdims=True)) + lam_init)
            o = a0[:, :dv] / a0[:, dv:dv + 1] - lam * (a1[:, :dv] / a1[:, dv:dv + 1])
            o = _rms(o, subln_ref[...]) * (1.0 - lam_init)
            o_ref[...] = o.astype(o_ref.dtype)
        else:
            for si in range(len(streams)):
                a = acc_scr[si]
                l = a[:, dv:dv + 1]
                o_ref[:, si * dv:(si + 1) * dv] = (a[:, :dv] / l).astype(o_ref.dtype)
                if mode == "lse":
                    lse_ref[:, si * dv:(si + 1) * dv] = m_scr[si][:, :dv] + jnp.log2(l)


def _flash(q, kt, v, *, units, q_per_unit, k_per_unit, v_per_unit, streams, tq, tk, nv, dv, out_w,
           split=1, unit0=0, band=None, mode="gqa", sink=None, lam=None, subln=None, lam_init=0.0, name="flash"):
    s = q.shape[1]
    tq, tk = min(tq, s), min(tk, s)
    assert s % tq == 0 and s % tk == 0 and (band is None or tq % tk == 0)
    n_kblocks = s // tk
    if band is None:
        nsteps = n_kblocks
        kmap = lambda u, i, j: (u + unit0, 0, j)
        vmap = lambda u, i, j: (j, u + unit0)
    else:
        halfw = band[0]
        off = -(-halfw // tk)
        nsteps = tq // tk + 2 * off
        band = (band[0], band[1], off)
        kidx = lambda i, j: jnp.clip(i * (tq // tk) + j - off, 0, n_kblocks - 1)
        kmap = lambda u, i, j: (u + unit0, 0, kidx(i, j))
        vmap = lambda u, i, j: (kidx(i, j), u + unit0)
    in_specs = [pl.BlockSpec((q_per_unit, tq, HEAD_DIM), lambda u, i, j: (u + unit0, i, 0)),
                pl.BlockSpec((k_per_unit, HEAD_DIM, tk), kmap),
                pl.BlockSpec((tk, v_per_unit * nv), vmap)]
    args = [q, kt, v]
    if sink is not None:
        in_specs.append(pl.BlockSpec(memory_space=pltpu.SMEM))
        args.append(sink)
    if mode == "diff":
        in_specs += [pl.BlockSpec(lam.shape, lambda u, i, j: (0, 0)), pl.BlockSpec(subln.shape, lambda u, i, j: (0, 0))]
        args += [lam, subln]
    out_shape = [jax.ShapeDtypeStruct((s, units * out_w), BF16)]
    out_specs = [pl.BlockSpec((tq, out_w), lambda u, i, j: (i, u))]
    if mode == "lse":
        out_shape.append(jax.ShapeDtypeStruct((s, units * out_w), F32))
        out_specs.append(pl.BlockSpec((tq, out_w), lambda u, i, j: (i, u)))
    assert tq % split == 0
    kern = functools.partial(_flash_kernel, streams=streams, split=split, tq=tq, tk=tk, nv=nv, dv=dv, band=band,
                             n_kblocks=n_kblocks, mode=mode, lam_init=lam_init, has_sink=sink is not None)
    res = pl.pallas_call(
        kern,
        out_shape=tuple(out_shape),
        grid=(units, s // tq, nsteps),
        in_specs=in_specs,
        out_specs=tuple(out_specs),
        scratch_shapes=[pltpu.VMEM((len(streams), tq, LANES), F32), pltpu.VMEM((len(streams), tq, nv), F32)],
        compiler_params=_cparams(("parallel", "parallel", "arbitrary")),
        name=name,
    )(*args)
    return res if mode == "lse" else res[0]


def _outproj_kernel(h_ref, o_ref, w_ref, out_ref):
    out_ref[...] = h_ref[...] + jnp.dot(o_ref[...], w_ref[...], preferred_element_type=F32)


def _outproj(h, o, w, tm=512):
    s, d = h.shape
    tm = min(tm, s)
    w = w.astype(BF16)
    return pl.pallas_call(
        _outproj_kernel,
        out_shape=jax.ShapeDtypeStruct((s, d), F32),
        grid=(s // tm,),
        in_specs=[pl.BlockSpec((tm, d), lambda i: (i, 0)), pl.BlockSpec((tm, o.shape[1]), lambda i: (i, 0)),
                  pl.BlockSpec(w.shape, lambda i: (0, 0))],
        out_specs=pl.BlockSpec((tm, d), lambda i: (i, 0)),
        compiler_params=_cparams(("parallel",)),
        name="outproj",
    )(h, o, w)


def _outproj_groups_kernel(h_ref, o0_ref, o1_ref, o2_ref, l0_ref, l1_ref, l2_ref, w_ref, out_ref):
    l0, l1, l2 = l0_ref[...], l1_ref[...], l2_ref[...]
    mx = jnp.maximum(jnp.maximum(l0, l1), l2)
    e0, e1, e2 = jnp.exp2(l0 - mx), jnp.exp2(l1 - mx), jnp.exp2(l2 - mx)
    tot = e0 + e1 + e2
    acc = h_ref[...]
    gw = o0_ref.shape[1]
    for g, (o_ref, e) in enumerate(((o0_ref, e0), (o1_ref, e1), (o2_ref, e2))):
        og = (o_ref[...].astype(F32) * (e / tot)).astype(BF16)
        acc = acc + jnp.dot(og, w_ref[g * gw:(g + 1) * gw, :], preferred_element_type=F32)
    out_ref[...] = acc


def _outproj_groups(h, os_, ls_, w, tm=512):
    s, d = h.shape
    tm = min(tm, s)
    w = w.astype(BF16)
    gw = os_[0].shape[1]
    row = lambda width: pl.BlockSpec((tm, width), lambda i: (i, 0))
    return pl.pallas_call(
        _outproj_groups_kernel,
        out_shape=jax.ShapeDtypeStruct((s, d), F32),
        grid=(s // tm,),
        in_specs=[row(d)] + [row(gw)] * 6 + [pl.BlockSpec(w.shape, lambda i: (0, 0))],
        out_specs=row(d),
        compiler_params=_cparams(("parallel",)),
        name="outproj_groups",
    )(h, *os_, *ls_, w)


def _route(logits):
    lane = lax.broadcasted_iota(jnp.int32, logits.shape, 1).astype(F32)
    big = 1e6
    gl = jnp.where(lane < MOE_GROUPS, logits, NEG_INF)
    gmax = jnp.max(gl, axis=1, keepdims=True)
    gidx = jnp.min(jnp.where(gl == gmax, lane, big), axis=1, keepdims=True)
    gw = 1.0 / jnp.sum(jnp.exp(gl - gmax), axis=1, keepdims=True)
    lo = MOE_GROUPS + gidx * MOE_PER_GROUP
    el = jnp.where((lane >= lo) & (lane < lo + MOE_PER_GROUP), logits, NEG_INF)
    v1 = jnp.max(el, axis=1, keepdims=True)
    i1 = jnp.min(jnp.where(el == v1, lane, big), axis=1, keepdims=True)
    el2 = jnp.where(lane == i1, NEG_INF, el)
    v2 = jnp.max(el2, axis=1, keepdims=True)
    i2 = jnp.min(jnp.where(el2 == v2, lane, big), axis=1, keepdims=True)
    e2 = jnp.exp(v2 - v1)
    w1 = gw / (1.0 + e2)
    w2 = w1 * e2
    return jnp.where(lane == i1, w1, 0.0) + jnp.where(lane == i2, w2, 0.0), gidx


def _moe_kernel(h_ref, g_ref, wr_ref, br_ref, wg_ref, wu_ref, wd_ref, out_ref, xn_scr, comb_scr, acc_scr):
    e = pl.program_id(1)

    @pl.when(e == 0)
    def _init():
        xn = _rms(h_ref[...], g_ref[...])
        xn_scr[...] = xn.astype(BF16)
        logits = jnp.dot(xn, wr_ref[...], preferred_element_type=F32, precision=lax.Precision.HIGHEST) + br_ref[...]
        comb_scr[...] = _route(logits)[0]
        acc_scr[...] = jnp.zeros_like(acc_scr)

    xn = xn_scr[...]
    gate = jnp.dot(xn, wg_ref[0], preferred_element_type=F32)
    up = jnp.dot(xn, wu_ref[0], preferred_element_type=F32)
    comb = comb_scr[...]
    lane = lax.broadcasted_iota(jnp.int32, comb.shape, 1)
    we = jnp.sum(jnp.where(lane == e + MOE_GROUPS, comb, 0.0), axis=1, keepdims=True)
    act = gate * jax.nn.sigmoid(gate) * up * we
    acc_scr[...] += jnp.dot(act.astype(BF16), wd_ref[0], preferred_element_type=F32)

    @pl.when(e == pl.num_programs(1) - 1)
    def _fin():
        out_ref[...] = h_ref[...] + acc_scr[...]


def _moe(h, g, w_group, b_group, w_expert, b_expert, w_gate, w_up, w_down, tm=1024):
    s, d = h.shape
    tm = min(tm, s)
    ne, _, ff = w_gate.shape
    nr = MOE_GROUPS + MOE_EXPERTS
    wr = jnp.pad(jnp.concatenate([w_group, w_expert], axis=1), ((0, 0), (0, LANES - nr)))
    br = jnp.pad(jnp.concatenate([b_group, b_expert]), (0, LANES - nr)).reshape(1, LANES)
    wg, wu, wd = w_gate.astype(BF16), w_up.astype(BF16), w_down.astype(BF16)
    return pl.pallas_call(
        _moe_kernel,
        out_shape=jax.ShapeDtypeStruct((s, d), F32),
        grid=(s // tm, ne),
        in_specs=[pl.BlockSpec((tm, d), lambda i, e: (i, 0)), pl.BlockSpec((1, d), lambda i, e: (0, 0)),
                  pl.BlockSpec((d, LANES), lambda i, e: (0, 0)), pl.BlockSpec((1, LANES), lambda i, e: (0, 0)),
                  pl.BlockSpec((1, d, ff), lambda i, e: (e, 0, 0)), pl.BlockSpec((1, d, ff), lambda i, e: (e, 0, 0)),
                  pl.BlockSpec((1, ff, d), lambda i, e: (e, 0, 0))],
        out_specs=pl.BlockSpec((tm, d), lambda i, e: (i, 0)),
        scratch_shapes=[pltpu.VMEM((tm, d), BF16), pltpu.VMEM((tm, LANES), F32), pltpu.VMEM((tm, d), F32)],
        compiler_params=_cparams(("parallel", "arbitrary")),
        name="moe",
    )(h, g, wr, br, wg, wu, wd)


MOE_ROW_TILE = 1024
INFO_GROUP_LANE = 0
INFO_RANK_LANE = 1


def _moe_route_kernel(h_ref, g_ref, wr_ref, br_ref, tri_ref, hx_ref, cnt_ref, run_scr):
    d = h_ref.shape[1]

    @pl.when(pl.program_id(0) == 0)
    def _init():
        run_scr[...] = jnp.zeros_like(run_scr)

    xn = _rms(h_ref[...], g_ref[...])
    logits = jnp.dot(xn, wr_ref[...], preferred_element_type=F32, precision=lax.Precision.HIGHEST) + br_ref[...]
    comb, gidx = _route(logits)
    lane = lax.broadcasted_iota(jnp.int32, comb.shape, 1).astype(F32)
    onehot = lane == gidx
    before = jnp.dot(tri_ref[...], onehot.astype(BF16), preferred_element_type=F32)
    run = run_scr[...]
    rank = jnp.sum(jnp.where(onehot, before + run, 0.0), axis=1, keepdims=True)
    run = run + jnp.sum(onehot.astype(F32), axis=0, keepdims=True)
    run_scr[...] = run
    cnt_ref[...] = run
    hx_ref[:, :d] = xn
    hx_ref[:, d:] = jnp.where(lane == INFO_GROUP_LANE, gidx, jnp.where(lane == INFO_RANK_LANE, rank, comb))


def _moe_dispatch_kernel(pos_ref, hx_ref, xs_in_ref, xs_ref, sem):
    del xs_in_ref
    tm = hx_ref.shape[0]

    def row_copy(t, p):
        return pltpu.make_async_copy(hx_ref.at[pl.ds(t, 1), :], xs_ref.at[pl.ds(p, 1), :], sem)

    def issue(t, c):
        row_copy(t, pos_ref[0, t]).start()
        return c

    def drain(t, c):
        row_copy(0, 0).wait()
        return c

    lax.fori_loop(0, tm, issue, 0)
    lax.fori_loop(0, tm, drain, 0)


def _moe_expert_kernel(tg_ref, nv_ref, xs_ref, wg_ref, wu_ref, wd_ref, y_ref, acc_scr):
    j, e = pl.program_id(0), pl.program_id(1)
    d = y_ref.shape[1]
    last = pl.num_programs(1) - 1

    @pl.when(j < nv_ref[0])
    def _live():
        @pl.when(e == 0)
        def _zero():
            acc_scr[...] = jnp.zeros_like(acc_scr)

        x = xs_ref[:, :d].astype(BF16)
        info = xs_ref[:, d:]
        lane = lax.broadcasted_iota(jnp.int32, info.shape, 1)
        eid = tg_ref[j] * MOE_PER_GROUP + e + MOE_GROUPS
        we = jnp.sum(jnp.where(lane == eid, info, 0.0), axis=1, keepdims=True)
        gate = jnp.dot(x, wg_ref[0], preferred_element_type=F32)
        up = jnp.dot(x, wu_ref[0], preferred_element_type=F32)
        act = gate * jax.nn.sigmoid(gate) * up * we
        acc_scr[...] += jnp.dot(act.astype(BF16), wd_ref[0], preferred_element_type=F32)

        @pl.when(e == last)
        def _out():
            y_ref[...] = acc_scr[...]

    @pl.when((j >= nv_ref[0]) & (e == last))
    def _dead():
        y_ref[...] = jnp.zeros_like(y_ref)


def _moe_sparse(h, g, w_group, b_group, w_expert, b_expert, w_gate, w_up, w_down, tm=1024, tmd=512):
    s, d = h.shape
    tm, tmd, tb = min(tm, s), min(tmd, s), min(MOE_ROW_TILE, s)
    ne, _, ff = w_gate.shape
    xw = d + LANES
    nr = MOE_GROUPS + MOE_EXPERTS
    wr = jnp.pad(jnp.concatenate([w_group, w_expert], axis=1), ((0, 0), (0, LANES - nr)))
    br = jnp.pad(jnp.concatenate([b_group, b_expert]), (0, LANES - nr)).reshape(1, LANES)
    tri = (lax.broadcasted_iota(jnp.int32, (tm, tm), 0) > lax.broadcasted_iota(jnp.int32, (tm, tm), 1)).astype(BF16)
    const = lambda a: pl.BlockSpec(a.shape, lambda i: (0, 0))
    hx, cnt = pl.pallas_call(
        _moe_route_kernel,
        out_shape=(jax.ShapeDtypeStruct((s, xw), F32), jax.ShapeDtypeStruct((1, LANES), F32)),
        grid=(s // tm,),
        in_specs=[pl.BlockSpec((tm, d), lambda i: (i, 0)), const(g), const(wr), const(br), const(tri)],
        out_specs=(pl.BlockSpec((tm, xw), lambda i: (i, 0)), pl.BlockSpec((1, LANES), lambda i: (0, 0))),
        scratch_shapes=[pltpu.VMEM((1, LANES), F32)],
        compiler_params=_cparams(("arbitrary",)),
        name="moe_route",
    )(h, g, wr, br, tri)

    counts = cnt[0, :MOE_GROUPS].astype(jnp.int32)
    padded = ((counts + tb - 1) // tb) * tb
    ends = jnp.cumsum(padded)
    starts = ends - padded
    tok_group = hx[:, d + INFO_GROUP_LANE].astype(jnp.int32)
    tok_rank = hx[:, d + INFO_RANK_LANE].astype(jnp.int32)
    pos = starts[tok_group] + tok_rank
    n_tiles = s // tb + MOE_GROUPS
    p_rows = n_tiles * tb
    tile_group = jnp.minimum(jnp.sum((jnp.arange(n_tiles) * tb)[:, None] >= ends[None, :], axis=1), MOE_GROUPS - 1)
    n_valid = (ends[-1] // tb).reshape(1)

    xs = pl.pallas_call(
        _moe_dispatch_kernel,
        out_shape=jax.ShapeDtypeStruct((p_rows, xw), F32),
        grid=(s // tmd,),
        in_specs=[pl.BlockSpec((None, 1, tmd), lambda i: (i, 0, 0), memory_space=pltpu.SMEM),
                  pl.BlockSpec((tmd, xw), lambda i: (i, 0)),
                  pl.BlockSpec(memory_space=pl.ANY)],
        out_specs=pl.BlockSpec(memory_space=pl.ANY),
        scratch_shapes=[pltpu.SemaphoreType.DMA],
        input_output_aliases={2: 0},
        compiler_params=_cparams(("arbitrary",)),
        name="moe_dispatch",
    )(pos.reshape(s // tmd, 1, tmd), hx, jnp.zeros((p_rows, xw), F32))

    wg, wu, wd = w_gate.astype(BF16), w_up.astype(BF16), w_down.astype(BF16)
    wsel = lambda j, e, tg, nv: (jnp.where(j < nv[0], tg[j] * MOE_PER_GROUP + e, ne - 1), 0, 0)
    y = pl.pallas_call(
        _moe_expert_kernel,
        out_shape=jax.ShapeDtypeStruct((p_rows, d), F32),
        grid_spec=pltpu.PrefetchScalarGridSpec(
            num_scalar_prefetch=2,
            grid=(n_tiles, MOE_PER_GROUP),
            in_specs=[pl.BlockSpec((tb, xw), lambda j, e, tg, nv: (j, 0)),
                      pl.BlockSpec((1, d, ff), wsel), pl.BlockSpec((1, d, ff), wsel), pl.BlockSpec((1, ff, d), wsel)],
            out_specs=pl.BlockSpec((tb, d), lambda j, e, tg, nv: (j, 0)),
            scratch_shapes=[pltpu.VMEM((tb, d), F32)]),
        compiler_params=_cparams(("arbitrary", "arbitrary")),
        name="moe_expert",
    )(tile_group.astype(jnp.int32), n_valid.astype(jnp.int32), xs, wg, wu, wd)
    return y, pos


def _ple_kernel(pos_ref, h_ref, y_hbm, g_ref, wg_ref, p_ref, wp_ref, gf_ref, out_ref, ybuf, sem, *, final):
    tm = h_ref.shape[0]

    def row_copy(t, p):
        return pltpu.make_async_copy(y_hbm.at[pl.ds(p, 1), :], ybuf.at[pl.ds(t, 1), :], sem)

    def issue(t, c):
        row_copy(t, pos_ref[0, t]).start()
        return c

    def drain(t, c):
        row_copy(0, 0).wait()
        return c

    lax.fori_loop(0, tm, issue, 0)
    lax.fori_loop(0, tm, drain, 0)
    x = h_ref[...] + ybuf[...]
    xn = _rms(x, g_ref[...]).astype(BF16)
    gate = jax.nn.sigmoid(jnp.dot(xn, wg_ref[...], preferred_element_type=F32))
    proj = jnp.dot(p_ref[...].astype(BF16), wp_ref[...], preferred_element_type=F32)
    y = x + gate * proj
    if final:
        y = _rms(y, gf_ref[...])
    out_ref[...] = y


def _ple(h, y_sorted, pos, g, wg, p, wp, gf, final, tm=512):
    s, d = h.shape
    tm = min(tm, s)
    wg, wp = wg.astype(BF16), wp.astype(BF16)
    const = lambda a: pl.BlockSpec(a.shape, lambda i: (0, 0))
    return pl.pallas_call(
        functools.partial(_ple_kernel, final=final),
        out_shape=jax.ShapeDtypeStruct((s, d), F32),
        grid=(s // tm,),
        in_specs=[pl.BlockSpec((None, 1, tm), lambda i: (i, 0, 0), memory_space=pltpu.SMEM),
                  pl.BlockSpec((tm, d), lambda i: (i, 0)), pl.BlockSpec(memory_space=pl.ANY), const(g), const(wg),
                  pl.BlockSpec((tm, p.shape[1]), lambda i: (i, 0)), const(wp), const(gf)],
        out_specs=pl.BlockSpec((tm, d), lambda i: (i, 0)),
        scratch_shapes=[pltpu.VMEM((tm, d), F32), pltpu.SemaphoreType.DMA],
        compiler_params=_cparams(("arbitrary",)),
        name="ple",
    )(pos.reshape(s // tm, 1, tm), h, y_sorted, g, wg, p, wp, gf)


PARTIAL_KPERM = ((8, 16), (0, 8), (16, 64))
AXIAL_KPERM = ((16, 32), (0, 16), (48, 64), (32, 48))
B_PAIRS = ((128, 1), (512, 4), (2048, 16))


def _partial_tables(s):
    inv = ROPE_THETA ** (-jnp.arange(0, ROT_DIM, 2, dtype=F32) / ROT_DIM)
    ang = jnp.arange(s).astype(F32)[:, None] * inv[None, :]
    return _rope_tables([ang], s, LOG2E * HEAD_DIM ** -0.5)


def _axial_tables(s):
    half = HEAD_DIM // 2
    inv = AXIAL_THETA ** (-jnp.arange(0, half, 2, dtype=F32) / half)
    t = jnp.arange(s)
    ang_r = (t // GRID_W).astype(F32)[:, None] * inv[None, :]
    ang_c = (t % GRID_W).astype(F32)[:, None] * inv[None, :]
    return _rope_tables([ang_r, ang_c], s, LOG2E * HEAD_DIM ** -0.5)


def _mixer_diff(h, g, w_in, w_out, lam_rows, subln, lam_init, tabs):
    d = h.shape[1]
    heads = d // (2 * HEAD_DIM)
    aw = 2 * heads * HEAD_DIM
    q, kt, v = _project(h, g, w_in[:, :aw], w_in[:, aw:2 * aw], w_in[:, 2 * aw:], tabs,
                        hq=2 * heads, hk=2 * heads, hv=heads, dv=2 * HEAD_DIM, nv=4 * HEAD_DIM,
                        shift=ROT_DIM // 2, kperm=PARTIAL_KPERM, qk_norm=False)
    o = _flash(q, kt, v, units=heads, q_per_unit=2, k_per_unit=2, v_per_unit=1,
               streams=((0, 0, 0), (1, 1, 0)), split=4, tq=1024, tk=1024, nv=4 * HEAD_DIM, dv=2 * HEAD_DIM,
               out_w=2 * HEAD_DIM, mode="diff", lam=lam_rows, subln=subln, lam_init=lam_init, name="flash_diff")
    return _outproj(h, o, w_out)


def _mixer_dilated(h, g, w_in, w_out, tabs):
    nh, hg = 12, 4
    bw = nh * HEAD_DIM
    q, kt, v = _project(h, g, w_in[:, :bw], w_in[:, bw:2 * bw], w_in[:, 2 * bw:], tabs,
                        hq=nh, hk=nh, hv=nh, dv=HEAD_DIM, nv=LANES,
                        shift=ROT_DIM // 2, kperm=PARTIAL_KPERM, qk_norm=False)
    os_, ls_ = [], []
    for gi, (win, dil) in enumerate(B_PAIRS):
        halfw = (win // (2 * dil)) * dil
        o, lse = _flash(q, kt, v, units=1, unit0=gi, q_per_unit=hg, k_per_unit=hg, v_per_unit=hg,
                        streams=tuple((j, j, j) for j in range(hg)), tq=1024, tk=max(halfw, 256),
                        nv=LANES, dv=HEAD_DIM, out_w=hg * HEAD_DIM, band=(halfw, dil), mode="lse",
                        name=f"flash_dilated{gi}")
        os_.append(o)
        ls_.append(lse)
    return _outproj_groups(h, os_, ls_, w_out)


def _gqa_project(h, g, w_in, tabs, shift, kperm, qk_norm, qg=None, kg=None):
    nq, nkv = 16, 4
    qd, kvd = nq * HEAD_DIM, nkv * HEAD_DIM
    return _project(h, g, w_in[:, :qd], w_in[:, qd:qd + kvd], w_in[:, qd + kvd:], tabs,
                    hq=nq, hk=nkv, hv=nkv, dv=HEAD_DIM, nv=LANES, shift=shift, kperm=kperm,
                    qk_norm=qk_norm, qg=qg, kg=kg)


def _mixer_window(h, g, w_in, w_out, sink, tabs):
    q, kt, v = _gqa_project(h, g, w_in, tabs, ROT_DIM // 2, PARTIAL_KPERM, False)
    o = _flash(q, kt, v, units=4, q_per_unit=4, k_per_unit=1, v_per_unit=1,
               streams=tuple((j, 0, 0) for j in range(4)), tq=256, tk=256, nv=LANES, dv=HEAD_DIM, out_w=4 * HEAD_DIM, band=(128, 1), mode="gqa",
               sink=sink.reshape(4, 4), name="flash_window")
    return _outproj(h, o, w_out)


def _mixer_axial(h, g, w_in, w_out, q_norm, k_norm, tabs):
    qg = jnp.pad(q_norm, (0, LANES - HEAD_DIM)).reshape(1, LANES)
    kg = k_norm.reshape(HEAD_DIM, 1)
    q, kt, v = _gqa_project(h, g, w_in, tabs, HEAD_DIM // 4, AXIAL_KPERM, True, qg, kg)
    o = _flash(q, kt, v, units=4, q_per_unit=4, k_per_unit=1, v_per_unit=1,
               streams=tuple((j, 0, 0) for j in range(4)), split=4,
               tq=1024, tk=1024, nv=LANES, dv=HEAD_DIM, out_w=4 * HEAD_DIM, mode="gqa", name="flash_axial")
    return _outproj(h, o, w_out)


def kernel(x, p, norm_mix, norm_ffn, norm_ple, norm_final, a_w_in, a_w_out, a_lam_q1, a_lam_k1, a_lam_q2, a_lam_k2, a_subln, b_w_in, b_w_out, c_w_in, c_w_out, c_sink, d_w_in, d_w_out, d_q_norm, d_k_norm, moe_w_group, moe_b_group, moe_w_expert, moe_b_expert, moe_w_gate, moe_w_up, moe_w_down, ple_w_gate, ple_w_proj):
    bn, s, d = x.shape
    assert bn == 1
    depth = p.shape[0]
    h = x[0]
    ptabs = _partial_tables(s)
    atabs = _axial_tables(s)
    row = lambda a: a.reshape(1, -1)
    for i in range(depth):
        r, kind = divmod(i, 4)
        g = row(norm_mix[i])
        if kind == 0:
            lam_init = 0.8 - 0.6 * math.exp(-0.3 * i)
            lam_rows = jnp.stack([a_lam_q1[r], a_lam_k1[r], a_lam_q2[r], a_lam_k2[r]])
            h = _mixer_diff(h, g, a_w_in[r], a_w_out[r], lam_rows, row(a_subln[r]), lam_init, ptabs)
        elif kind == 1:
            h = _mixer_dilated(h, g, b_w_in[r], b_w_out[r], ptabs)
        elif kind == 2:
            h = _mixer_window(h, g, c_w_in[r], c_w_out[r], c_sink[r], ptabs)
        else:
            h = _mixer_axial(h, g, d_w_in[r], d_w_out[r], d_q_norm[r], d_k_norm[r], atabs)
        y_sorted, pos = _moe_sparse(h, row(norm_ffn[i]), moe_w_group[i], moe_b_group[i], moe_w_expert[i],
                                    moe_b_expert[i], moe_w_gate[i], moe_w_up[i], moe_w_down[i])
        h = _ple(h, y_sorted, pos, row(norm_ple[i]), ple_w_gate[i], p[i, 0], ple_w_proj[i], row(norm_final), final=(i == depth - 1))
    return h[None]
```

```python
import functools
import math

import jax
import jax.numpy as jnp
from jax import lax
from jax.experimental import pallas as pl
from jax.experimental.pallas import tpu as pltpu

F32 = jnp.float32
BF16 = jnp.bfloat16

HEAD_DIM = 64
LANES = 128
EPS = 1e-6
LOG2E = 1.4426950408889634
NEG_INF = -1e30
ROPE_THETA = 500000.0
ROT_DIM = HEAD_DIM // 4
AXIAL_THETA = 10000.0
GRID_W = 64
MOE_GROUPS = 4
MOE_PER_GROUP = 4
MOE_EXPERTS = 16
VMEM_LIMIT = 56 * 1024 * 1024


def _cparams(sem):
    return pltpu.CompilerParams(dimension_semantics=sem, vmem_limit_bytes=VMEM_LIMIT)


def _rms(x, g):
    return x * lax.rsqrt(jnp.mean(x * x, axis=-1, keepdims=True) + EPS) * g


def _proj_kernel(h_ref, g_ref, wq_ref, wkt_ref, wv_ref, qc_ref, qs1_ref, qs2_ref, kc_ref, ks_ref,
                 qg_ref, kg_ref, q_ref, kt_ref, v_ref, *, hq, hk, nv, dv, shift, kperm, qk_norm):
    xn = _rms(h_ref[...], g_ref[...]).astype(BF16)
    qf = jnp.dot(xn, wq_ref[...], preferred_element_type=F32)
    qc, qs1, qs2 = qc_ref[...], qs1_ref[...], qs2_ref[...]
    for h in range(hq):
        s = qf[:, h * LANES:(h + 1) * LANES]
        if qk_norm:
            ms = jnp.sum(s * s, axis=-1, keepdims=True) * (1.0 / HEAD_DIM)
            s = s * lax.rsqrt(ms + EPS) * qg_ref[...]
        r = s * qc + pltpu.roll(s, LANES - shift, 1) * qs1 + pltpu.roll(s, shift, 1) * qs2
        q_ref[h] = r[:, :HEAD_DIM].astype(BF16)
    kf = lax.dot_general(wkt_ref[...], xn, (((1,), (1,)), ((), ())), preferred_element_type=F32)
    kc, ks = kc_ref[...], ks_ref[...]
    for h in range(hk):
        s = kf[h * HEAD_DIM:(h + 1) * HEAD_DIM, :]
        if qk_norm:
            ms = jnp.sum(s * s, axis=0, keepdims=True) * (1.0 / HEAD_DIM)
            s = s * lax.rsqrt(ms + EPS) * kg_ref[...]
        partner = jnp.concatenate([s[a:b] for a, b in kperm], axis=0)
        kt_ref[h] = (s * kc + partner * ks).astype(BF16)
    vf = jnp.dot(xn, wv_ref[...], preferred_element_type=F32)
    lane = lax.broadcasted_iota(jnp.int32, vf.shape, 1)
    v_ref[...] = jnp.where((lane & (nv - 1)) == dv, 1.0, vf).astype(BF16)


def _project(h, g, wq, wk, wv, tabs, *, hq, hk, hv, dv, nv, shift, kperm, qk_norm, qg=None, kg=None, tm=512):
    s, d = h.shape
    tm = min(tm, s)
    wq_p = jnp.pad(wq.reshape(d, hq, HEAD_DIM), ((0, 0), (0, 0), (0, LANES - HEAD_DIM))).reshape(d, hq * LANES).astype(BF16)
    wkt = wk.T.astype(BF16)
    wv_p = jnp.pad(wv.reshape(d, hv, dv), ((0, 0), (0, 0), (0, nv - dv))).reshape(d, hv * nv).astype(BF16)
    qc, qs1, qs2, kc, ks = tabs
    if qg is None:
        qg = jnp.ones((1, LANES), F32)
        kg = jnp.ones((HEAD_DIM, 1), F32)
    full = lambda a: pl.BlockSpec(a.shape, lambda i: (0,) * a.ndim)
    kern = functools.partial(_proj_kernel, hq=hq, hk=hk, nv=nv, dv=dv, shift=shift, kperm=kperm, qk_norm=qk_norm)
    return pl.pallas_call(
        kern,
        out_shape=(jax.ShapeDtypeStruct((hq, s, HEAD_DIM), BF16),
                   jax.ShapeDtypeStruct((hk, HEAD_DIM, s), BF16),
                   jax.ShapeDtypeStruct((s, hv * nv), BF16)),
        grid=(s // tm,),
        in_specs=[pl.BlockSpec((tm, d), lambda i: (i, 0)), full(g), full(wq_p), full(wkt), full(wv_p),
                  pl.BlockSpec((tm, LANES), lambda i: (i, 0)), pl.BlockSpec((tm, LANES), lambda i: (i, 0)),
                  pl.BlockSpec((tm, LANES), lambda i: (i, 0)),
                  pl.BlockSpec((HEAD_DIM, tm), lambda i: (0, i)), pl.BlockSpec((HEAD_DIM, tm), lambda i: (0, i)),
                  full(qg), full(kg)],
        out_specs=(pl.BlockSpec((hq, tm, HEAD_DIM), lambda i: (0, i, 0)),
                   pl.BlockSpec((hk, HEAD_DIM, tm), lambda i: (0, 0, i)),
                   pl.BlockSpec((tm, hv * nv), lambda i: (i, 0))),
        compiler_params=_cparams(("parallel",)),
        name="proj",
    )(h, g, wq_p, wkt, wv_p, qc, qs1, qs2, kc, ks, qg, kg)


def _rope_tables(ang_list, s, scale):
    cs, sn1, sn2, ksn = [], [], [], []
    used = 0
    for ang in ang_list:
        c, sn = jnp.cos(ang), jnp.sin(ang)
        z = jnp.zeros_like(sn)
        cs += [c, c]
        sn1 += [-sn, z]
        sn2 += [z, sn]
        ksn += [-sn, sn]
        used += 2 * ang.shape[1]
    rest = HEAD_DIM - used
    ones, zeros = jnp.ones((s, rest), F32), jnp.zeros((s, rest), F32)
    c64 = jnp.concatenate(cs + [ones], axis=1)
    pad = jnp.zeros((s, LANES - HEAD_DIM), F32)
    qc = jnp.concatenate([c64, pad], axis=1) * scale
    qs1 = jnp.concatenate(sn1 + [zeros, pad], axis=1) * scale
    qs2 = jnp.concatenate(sn2 + [zeros, pad], axis=1) * scale
    kc = c64.T
    ks = jnp.concatenate(ksn + [zeros], axis=1).T
    return qc, qs1, qs2, kc, ks


def _lane_tile(x, width):
    reps = width // LANES
    return x if reps == 1 else jnp.concatenate([x] * reps, axis=1)


def _flash_kernel(*refs, streams, split, tq, tk, nv, dv, band, n_kblocks, mode, lam_init, has_sink):
    it = iter(refs)
    q_ref, kt_ref, v_ref = next(it), next(it), next(it)
    sink_ref = next(it) if has_sink else None
    lam_ref = subln_ref = None
    if mode == "diff":
        lam_ref, subln_ref = next(it), next(it)
    o_ref = next(it)
    lse_ref = next(it) if mode == "lse" else None
    m_scr, acc_scr = next(it), next(it)

    qb, kb = pl.program_id(1), pl.program_id(2)
    nsteps = pl.num_programs(2)
    cr = tq // split

    @pl.when(kb == 0)
    def _init():
        for si, (qi, _, _) in enumerate(streams):
            if has_sink:
                m_scr[si] = jnp.full((tq, LANES), LOG2E, F32) * sink_ref[pl.program_id(0), qi]
                lane = lax.broadcasted_iota(jnp.int32, (tq, nv), 1)
                acc_scr[si] = jnp.where(lane == dv, 1.0, 0.0).astype(F32)
            else:
                m_scr[si] = jnp.full((tq, LANES), NEG_INF, F32)
                acc_scr[si] = jnp.zeros((tq, nv), F32)

    def _step():
        chunks = [(si, c * cr, qi, ki, vi) for si, (qi, ki, vi) in enumerate(streams) for c in range(split)]
        scores = [jnp.dot(q_ref[qi, r0:r0 + cr, :], kt_ref[ki], preferred_element_type=F32)
                  for _, r0, qi, ki, _ in chunks]
        for sc, (si, r0, _, _, vi) in zip(scores, chunks):
            if band is not None:
                halfw, dil, off = band
                row = lax.broadcasted_iota(jnp.int32, (cr, tk), 0) + r0
                col = lax.broadcasted_iota(jnp.int32, (cr, tk), 1)
                diff = col - row + ((kb - off) * tk)
                valid = (jnp.abs(diff) <= halfw) & ((diff & (dil - 1)) == 0)
                sc = jnp.where(valid, sc, NEG_INF)
            m_old = m_scr[si, r0:r0 + cr, :]
            m_new = jnp.maximum(m_old, jnp.broadcast_to(jnp.max(sc, axis=1, keepdims=True), (cr, LANES)))
            p = jnp.exp2(sc - _lane_tile(m_new, tk))
            if band is not None:
                p = jnp.where(valid, p, 0.0)
            alpha = jnp.exp2(m_old - m_new)
            pv = jnp.dot(p.astype(BF16), v_ref[:, vi * nv:(vi + 1) * nv], preferred_element_type=F32)
            acc_scr[si, r0:r0 + cr, :] = acc_scr[si, r0:r0 + cr, :] * _lane_tile(alpha, nv) + pv
            m_scr[si, r0:r0 + cr, :] = m_new

    if band is None:
        _step()
    else:
        _, _, off = band
        kabs = qb * (tq // tk) + kb - off
        pl.when((kabs >= 0) & (kabs < n_kblocks))(_step)

    @pl.when(kb == nsteps - 1)
    def _fin():
        if mode == "diff":
            a0, a1 = acc_scr[0], acc_scr[1]
            lam_rows = lam_ref[...]
            lam = (jnp.exp(jnp.sum(lam_rows[0:1] * lam_rows[1:2], axis=1, keepdims=True))
                   - jnp.exp(jnp.sum(lam_rows[2:3] * lam_rows[3:4], axis=1, keepdims=True)) + lam_init)
            o = a0[:, :dv] / a0[:, dv:dv + 1] - lam * (a1[:, :dv] / a1[:, dv:dv + 1])
            o = _rms(o, subln_ref[...]) * (1.0 - lam_init)
            o_ref[...] = o.astype(o_ref.dtype)
        else:
            for si in range(len(streams)):
                a = acc_scr[si]
                l = a[:, dv:dv + 1]
                o_ref[:, si * dv:(si + 1) * dv] = (a[:, :dv] / l).astype(o_ref.dtype)
                if mode == "lse":
                    lse_ref[:, si * dv:(si + 1) * dv] = m_scr[si][:, :dv] + jnp.log2(l)


def _flash(q, kt, v, *, units, q_per_unit, k_per_unit, v_per_unit, streams, tq, tk, nv, dv, out_w,
           split=1, unit0=0, band=None, mode="gqa", sink=None, lam=None, subln=None, lam_init=0.0, name="flash"):
    s = q.shape[1]
    tq, tk = min(tq, s), min(tk, s)
    assert s % tq == 0 and s % tk == 0 and (band is None or tq % tk == 0)
    n_kblocks = s // tk
    if band is None:
        nsteps = n_kblocks
        kmap = lambda u, i, j: (u + unit0, 0, j)
        vmap = lambda u, i, j: (j, u + unit0)
    else:
        halfw = band[0]
        off = -(-halfw // tk)
        nsteps = tq // tk + 2 * off
        band = (band[0], band[1], off)
        kidx = lambda i, j: jnp.clip(i * (tq // tk) + j - off, 0, n_kblocks - 1)
        kmap = lambda u, i, j: (u + unit0, 0, kidx(i, j))
        vmap = lambda u, i, j: (kidx(i, j), u + unit0)
    in_specs = [pl.BlockSpec((q_per_unit, tq, HEAD_DIM), lambda u, i, j: (u + unit0, i, 0)),
                pl.BlockSpec((k_per_unit, HEAD_DIM, tk), kmap),
                pl.BlockSpec((tk, v_per_unit * nv), vmap)]
    args = [q, kt, v]
    if sink is not None:
        in_specs.append(pl.BlockSpec(memory_space=pltpu.SMEM))
        args.append(sink)
    if mode == "diff":
        in_specs += [pl.BlockSpec(lam.shape, lambda u, i, j: (0, 0)), pl.BlockSpec(subln.shape, lambda u, i, j: (0, 0))]
        args += [lam, subln]
    out_shape = [jax.ShapeDtypeStruct((s, units * out_w), BF16)]
    out_specs = [pl.BlockSpec((tq, out_w), lambda u, i, j: (i, u))]
    if mode == "lse":
        out_shape.append(jax.ShapeDtypeStruct((s, units * out_w), F32))
        out_specs.append(pl.BlockSpec((tq, out_w), lambda u, i, j: (i, u)))
    assert tq % split == 0
    kern = functools.partial(_flash_kernel, streams=streams, split=split, tq=tq, tk=tk, nv=nv, dv=dv, band=band,
                             n_kblocks=n_kblocks, mode=mode, lam_init=lam_init, has_sink=sink is not None)
    res = pl.pallas_call(
        kern,
        out_shape=tuple(out_shape),
        grid=(units, s // tq, nsteps),
        in_specs=in_specs,
        out_specs=tuple(out_specs),
        scratch_shapes=[pltpu.VMEM((len(streams), tq, LANES), F32), pltpu.VMEM((len(streams), tq, nv), F32)],
        compiler_params=_cparams(("parallel", "parallel", "arbitrary")),
        name=name,
    )(*args)
    return res if mode == "lse" else res[0]


def _outproj_kernel(h_ref, o_ref, w_ref, out_ref):
    out_ref[...] = h_ref[...] + jnp.dot(o_ref[...], w_ref[...], preferred_element_type=F32)


def _outproj(h, o, w, tm=512):
    s, d = h.shape
    tm = min(tm, s)
    w = w.astype(BF16)
    return pl.pallas_call(
        _outproj_kernel,
        out_shape=jax.ShapeDtypeStruct((s, d), F32),
        grid=(s // tm,),
        in_specs=[pl.BlockSpec((tm, d), lambda i: (i, 0)), pl.BlockSpec((tm, o.shape[1]), lambda i: (i, 0)),
                  pl.BlockSpec(w.shape, lambda i: (0, 0))],
        out_specs=pl.BlockSpec((tm, d), lambda i: (i, 0)),
        compiler_params=_cparams(("parallel",)),
        name="outproj",
    )(h, o, w)


def _outproj_groups_kernel(h_ref, o0_ref, o1_ref, o2_ref, l0_ref, l1_ref, l2_ref, w_ref, out_ref):
    l0, l1, l2 = l0_ref[...], l1_ref[...], l2_ref[...]
    mx = jnp.maximum(jnp.maximum(l0, l1), l2)
    e0, e1, e2 = jnp.exp2(l0 - mx), jnp.exp2(l1 - mx), jnp.exp2(l2 - mx)
    tot = e0 + e1 + e2
    acc = h_ref[...]
    gw = o0_ref.shape[1]
    for g, (o_ref, e) in enumerate(((o0_ref, e0), (o1_ref, e1), (o2_ref, e2))):
        og = (o_ref[...].astype(F32) * (e / tot)).astype(BF16)
        acc = acc + jnp.dot(og, w_ref[g * gw:(g + 1) * gw, :], preferred_element_type=F32)
    out_ref[...] = acc


def _outproj_groups(h, os_, ls_, w, tm=512):
    s, d = h.shape
    tm = min(tm, s)
    w = w.astype(BF16)
    gw = os_[0].shape[1]
    row = lambda width: pl.BlockSpec((tm, width), lambda i: (i, 0))
    return pl.pallas_call(
        _outproj_groups_kernel,
        out_shape=jax.ShapeDtypeStruct((s, d), F32),
        grid=(s // tm,),
        in_specs=[row(d)] + [row(gw)] * 6 + [pl.BlockSpec(w.shape, lambda i: (0, 0))],
        out_specs=row(d),
        compiler_params=_cparams(("parallel",)),
        name="outproj_groups",
    )(h, *os_, *ls_, w)


def _route(logits):
    lane = lax.broadcasted_iota(jnp.int32, logits.shape, 1).astype(F32)
    big = 1e6
    gl = jnp.where(lane < MOE_GROUPS, logits, NEG_INF)
    gmax = jnp.max(gl, axis=1, keepdims=True)
    gidx = jnp.min(jnp.where(gl == gmax, lane, big), axis=1, keepdims=True)
    gw = 1.0 / jnp.sum(jnp.exp(gl - gmax), axis=1, keepdims=True)
    lo = MOE_GROUPS + gidx * MOE_PER_GROUP
    el = jnp.where((lane >= lo) & (lane < lo + MOE_PER_GROUP), logits, NEG_INF)
    v1 = jnp.max(el, axis=1, keepdims=True)
    i1 = jnp.min(jnp.where(el == v1, lane, big), axis=1, keepdims=True)
    el2 = jnp.where(lane == i1, NEG_INF, el)
    v2 = jnp.max(el2, axis=1, keepdims=True)
    i2 = jnp.min(jnp.where(el2 == v2, lane, big), axis=1, keepdims=True)
    e2 = jnp.exp(v2 - v1)
    w1 = gw / (1.0 + e2)
    w2 = w1 * e2
    return jnp.where(lane == i1, w1, 0.0) + jnp.where(lane == i2, w2, 0.0), gidx


MOE_ROW_TILE = 1024
INFO_GROUP_LANE = 0
INFO_RANK_LANE = 1
SUBLANES = 8


def _to_slabs(ref, x):
    for c in range(SUBLANES):
        ref[:, c, :] = x[:, c * LANES:(c + 1) * LANES]


def _from_slabs(ref):
    return jnp.concatenate([ref[:, c, :] for c in range(SUBLANES)], axis=1)


def _moe_route_kernel(h_ref, g_ref, wr_ref, br_ref, tri_ref, x3_ref, info_ref, cnt_ref, run_scr):
    @pl.when(pl.program_id(0) == 0)
    def _init():
        run_scr[...] = jnp.zeros_like(run_scr)

    xn = _rms(h_ref[...], g_ref[...])
    logits = jnp.dot(xn, wr_ref[...], preferred_element_type=F32, precision=lax.Precision.HIGHEST) + br_ref[...]
    comb, gidx = _route(logits)
    lane = lax.broadcasted_iota(jnp.int32, comb.shape, 1).astype(F32)
    onehot = lane == gidx
    before = jnp.dot(tri_ref[...], onehot.astype(BF16), preferred_element_type=F32)
    run = run_scr[...]
    rank = jnp.sum(jnp.where(onehot, before + run, 0.0), axis=1, keepdims=True)
    run = run + jnp.sum(onehot.astype(F32), axis=0, keepdims=True)
    run_scr[...] = run
    cnt_ref[...] = run
    _to_slabs(x3_ref, xn)
    info_ref[...] = jnp.where(lane == INFO_GROUP_LANE, gidx, jnp.where(lane == INFO_RANK_LANE, rank, comb))


def _moe_dispatch_kernel(pos_ref, x3_ref, info_ref, xs_in_ref, infos_in_ref, xs_ref, infos_ref, sem):
    del xs_in_ref, infos_in_ref
    tm = x3_ref.shape[0]

    def issue(t, c):
        p = pos_ref[0, t]
        pltpu.make_async_copy(x3_ref.at[pl.ds(t, 1)], xs_ref.at[pl.ds(p, 1)], sem).start()
        pltpu.make_async_copy(info_ref.at[pl.ds(t, 1)], infos_ref.at[pl.ds(p, 1)], sem).start()
        return c

    lax.fori_loop(0, tm, issue, 0, unroll=8)
    pltpu.make_async_copy(x3_ref, xs_ref.at[pl.ds(0, tm)], sem).wait()
    pltpu.make_async_copy(info_ref, infos_ref.at[pl.ds(0, tm)], sem).wait()


def _moe_expert_kernel(tg_ref, nv_ref, xs_ref, infos_ref, wg_ref, wu_ref, wd_ref, y_ref, x_scr, acc_scr):
    j, e = pl.program_id(0), pl.program_id(1)
    last = pl.num_programs(1) - 1

    @pl.when(j < nv_ref[0])
    def _live():
        @pl.when(e == 0)
        def _first():
            acc_scr[...] = jnp.zeros_like(acc_scr)
            x_scr[...] = _from_slabs(xs_ref).astype(BF16)

        x = x_scr[...]
        info = infos_ref[...]
        lane = lax.broadcasted_iota(jnp.int32, info.shape, 1)
        eid = tg_ref[j] * MOE_PER_GROUP + e + MOE_GROUPS
        we = jnp.sum(jnp.where(lane == eid, info, 0.0), axis=1, keepdims=True)
        gate = jnp.dot(x, wg_ref[0], preferred_element_type=F32)
        up = jnp.dot(x, wu_ref[0], preferred_element_type=F32)
        act = gate * jax.nn.sigmoid(gate) * up * we
        acc_scr[...] += jnp.dot(act.astype(BF16), wd_ref[0], preferred_element_type=F32)

        @pl.when(e == last)
        def _out():
            _to_slabs(y_ref, acc_scr[...])

    @pl.when((j >= nv_ref[0]) & (e == last))
    def _dead():
        y_ref[...] = jnp.zeros_like(y_ref)


def _moe_sparse(h, g, w_group, b_group, w_expert, b_expert, w_gate, w_up, w_down, tm=1024, tmd=512):
    s, d = h.shape
    assert d == SUBLANES * LANES
    tm, tmd, tb = min(tm, s), min(tmd, s), min(MOE_ROW_TILE, s)
    ne, _, ff = w_gate.shape
    nr = MOE_GROUPS + MOE_EXPERTS
    wr = jnp.pad(jnp.concatenate([w_group, w_expert], axis=1), ((0, 0), (0, LANES - nr)))
    br = jnp.pad(jnp.concatenate([b_group, b_expert]), (0, LANES - nr)).reshape(1, LANES)
    tri = (lax.broadcasted_iota(jnp.int32, (tm, tm), 0) > lax.broadcasted_iota(jnp.int32, (tm, tm), 1)).astype(BF16)
    const = lambda a: pl.BlockSpec(a.shape, lambda i: (0, 0))
    slab = lambda rows: pl.BlockSpec((rows, SUBLANES, LANES), lambda i: (i, 0, 0))
    x3, info, cnt = pl.pallas_call(
        _moe_route_kernel,
        out_shape=(jax.ShapeDtypeStruct((s, SUBLANES, LANES), F32), jax.ShapeDtypeStruct((s, LANES), F32),
                   jax.ShapeDtypeStruct((1, LANES), F32)),
        grid=(s // tm,),
        in_specs=[pl.BlockSpec((tm, d), lambda i: (i, 0)), const(g), const(wr), const(br), const(tri)],
        out_specs=(slab(tm), pl.BlockSpec((tm, LANES), lambda i: (i, 0)), pl.BlockSpec((1, LANES), lambda i: (0, 0))),
        scratch_shapes=[pltpu.VMEM((1, LANES), F32)],
        compiler_params=_cparams(("arbitrary",)),
        name="moe_route",
    )(h, g, wr, br, tri)

    counts = cnt[0, :MOE_GROUPS].astype(jnp.int32)
    padded = ((counts + tb - 1) // tb) * tb
    ends = jnp.cumsum(padded)
    starts = ends - padded
    tok_group = info[:, INFO_GROUP_LANE].astype(jnp.int32)
    tok_rank = info[:, INFO_RANK_LANE].astype(jnp.int32)
    pos = starts[tok_group] + tok_rank
    n_tiles = s // tb + MOE_GROUPS
    p_rows = n_tiles * tb
    tile_group = jnp.minimum(jnp.sum((jnp.arange(n_tiles) * tb)[:, None] >= ends[None, :], axis=1), MOE_GROUPS - 1)
    n_valid = (ends[-1] // tb).reshape(1)

    anyspec = pl.BlockSpec(memory_space=pl.ANY)
    xs, infos = pl.pallas_call(
        _moe_dispatch_kernel,
        out_shape=(jax.ShapeDtypeStruct((p_rows, SUBLANES, LANES), F32), jax.ShapeDtypeStruct((p_rows, LANES), F32)),
        grid=(s // tmd,),
        in_specs=[pl.BlockSpec((None, 1, tmd), lambda i: (i, 0, 0), memory_space=pltpu.SMEM),
                  slab(tmd), pl.BlockSpec((tmd, LANES), lambda i: (i, 0)), anyspec, anyspec],
        out_specs=(anyspec, anyspec),
        scratch_shapes=[pltpu.SemaphoreType.DMA],
        input_output_aliases={3: 0, 4: 1},
        compiler_params=_cparams(("arbitrary",)),
        name="moe_dispatch",
    )(pos.reshape(s // tmd, 1, tmd), x3, info, jnp.zeros((p_rows, SUBLANES, LANES), F32),
      jnp.zeros((p_rows, LANES), F32))

    wg, wu, wd = w_gate.astype(BF16), w_up.astype(BF16), w_down.astype(BF16)
    wsel = lambda j, e, tg, nv: (jnp.where(j < nv[0], tg[j] * MOE_PER_GROUP + e, ne - 1), 0, 0)
    y = pl.pallas_call(
        _moe_expert_kernel,
        out_shape=jax.ShapeDtypeStruct((p_rows, SUBLANES, LANES), F32),
        grid_spec=pltpu.PrefetchScalarGridSpec(
            num_scalar_prefetch=2,
            grid=(n_tiles, MOE_PER_GROUP),
            in_specs=[pl.BlockSpec((tb, SUBLANES, LANES), lambda j, e, tg, nv: (j, 0, 0)),
                      pl.BlockSpec((tb, LANES), lambda j, e, tg, nv: (j, 0)),
                      pl.BlockSpec((1, d, ff), wsel), pl.BlockSpec((1, d, ff), wsel), pl.BlockSpec((1, ff, d), wsel)],
            out_specs=pl.BlockSpec((tb, SUBLANES, LANES), lambda j, e, tg, nv: (j, 0, 0)),
            scratch_shapes=[pltpu.VMEM((tb, d), BF16), pltpu.VMEM((tb, d), F32)]),
        compiler_params=_cparams(("arbitrary", "arbitrary")),
        name="moe_expert",
    )(tile_group.astype(jnp.int32), n_valid.astype(jnp.int32), xs, infos, wg, wu, wd)
    return y, pos


def _ple_kernel(pos_ref, h_ref, y_hbm, g_ref, wg_ref, p_ref, wp_ref, gf_ref, out_ref, ybuf, sem, *, final):
    tm = h_ref.shape[0]

    def issue(t, c):
        pltpu.make_async_copy(y_hbm.at[pl.ds(pos_ref[0, t], 1)], ybuf.at[pl.ds(t, 1)], sem).start()
        return c

    lax.fori_loop(0, tm, issue, 0, unroll=8)
    pltpu.make_async_copy(y_hbm.at[pl.ds(0, tm)], ybuf, sem).wait()
    x = h_ref[...] + _from_slabs(ybuf)
    xn = _rms(x, g_ref[...]).astype(BF16)
    gate = jax.nn.sigmoid(jnp.dot(xn, wg_ref[...], preferred_element_type=F32))
    proj = jnp.dot(p_ref[...].astype(BF16), wp_ref[...], preferred_element_type=F32)
    y = x + gate * proj
    if final:
        y = _rms(y, gf_ref[...])
    out_ref[...] = y


def _ple(h, y_sorted, pos, g, wg, p, wp, gf, final, tm=512):
    s, d = h.shape
    tm = min(tm, s)
    wg, wp = wg.astype(BF16), wp.astype(BF16)
    const = lambda a: pl.BlockSpec(a.shape, lambda i: (0, 0))
    return pl.pallas_call(
        functools.partial(_ple_kernel, final=final),
        out_shape=jax.ShapeDtypeStruct((s, d), F32),
        grid=(s // tm,),
        in_specs=[pl.BlockSpec((None, 1, tm), lambda i: (i, 0, 0), memory_space=pltpu.SMEM),
                  pl.BlockSpec((tm, d), lambda i: (i, 0)), pl.BlockSpec(memory_space=pl.ANY), const(g), const(wg),
                  pl.BlockSpec((tm, p.shape[1]), lambda i: (i, 0)), const(wp), const(gf)],
        out_specs=pl.BlockSpec((tm, d), lambda i: (i, 0)),
        scratch_shapes=[pltpu.VMEM((tm, SUBLANES, LANES), F32), pltpu.SemaphoreType.DMA],
        compiler_params=_cparams(("arbitrary",)),
        name="ple",
    )(pos.reshape(s // tm, 1, tm), h, y_sorted, g, wg, p, wp, gf)


PARTIAL_KPERM = ((8, 16), (0, 8), (16, 64))
AXIAL_KPERM = ((16, 32), (0, 16), (48, 64), (32, 48))
B_PAIRS = ((128, 1), (512, 4), (2048, 16))


def _partial_tables(s):
    inv = ROPE_THETA ** (-jnp.arange(0, ROT_DIM, 2, dtype=F32) / ROT_DIM)
    ang = jnp.arange(s).astype(F32)[:, None] * inv[None, :]
    return _rope_tables([ang], s, LOG2E * HEAD_DIM ** -0.5)


def _axial_tables(s):
    half = HEAD_DIM // 2
    inv = AXIAL_THETA ** (-jnp.arange(0, half, 2, dtype=F32) / half)
    t = jnp.arange(s)
    ang_r = (t // GRID_W).astype(F32)[:, None] * inv[None, :]
    ang_c = (t % GRID_W).astype(F32)[:, None] * inv[None, :]
    return _rope_tables([ang_r, ang_c], s, LOG2E * HEAD_DIM ** -0.5)


def _mixer_diff(h, g, w_in, w_out, lam_rows, subln, lam_init, tabs):
    d = h.shape[1]
    heads = d // (2 * HEAD_DIM)
    aw = 2 * heads * HEAD_DIM
    q, kt, v = _project(h, g, w_in[:, :aw], w_in[:, aw:2 * aw], w_in[:, 2 * aw:], tabs,
                        hq=2 * heads, hk=2 * heads, hv=heads, dv=2 * HEAD_DIM, nv=4 * HEAD_DIM,
                        shift=ROT_DIM // 2, kperm=PARTIAL_KPERM, qk_norm=False)
    o = _flash(q, kt, v, units=heads, q_per_unit=2, k_per_unit=2, v_per_unit=1,
               streams=((0, 0, 0), (1, 1, 0)), split=4, tq=1024, tk=1024, nv=4 * HEAD_DIM, dv=2 * HEAD_DIM,
               out_w=2 * HEAD_DIM, mode="diff", lam=lam_rows, subln=subln, lam_init=lam_init, name="flash_diff")
    return _outproj(h, o, w_out)


def _mixer_dilated(h, g, w_in, w_out, tabs):
    nh, hg = 12, 4
    bw = nh * HEAD_DIM
    q, kt, v = _project(h, g, w_in[:, :bw], w_in[:, bw:2 * bw], w_in[:, 2 * bw:], tabs,
                        hq=nh, hk=nh, hv=nh, dv=HEAD_DIM, nv=LANES,
                        shift=ROT_DIM // 2, kperm=PARTIAL_KPERM, qk_norm=False)
    os_, ls_ = [], []
    for gi, (win, dil) in enumerate(B_PAIRS):
        halfw = (win // (2 * dil)) * dil
        o, lse = _flash(q, kt, v, units=1, unit0=gi, q_per_unit=hg, k_per_unit=hg, v_per_unit=hg,
                        streams=tuple((j, j, j) for j in range(hg)), tq=1024, tk=max(halfw, 256),
                        nv=LANES, dv=HEAD_DIM, out_w=hg * HEAD_DIM, band=(halfw, dil), mode="lse",
                        name=f"flash_dilated{gi}")
        os_.append(o)
        ls_.append(lse)
    return _outproj_groups(h, os_, ls_, w_out)


def _gqa_project(h, g, w_in, tabs, shift, kperm, qk_norm, qg=None, kg=None):
    nq, nkv = 16, 4
    qd, kvd = nq * HEAD_DIM, nkv * HEAD_DIM
    return _project(h, g, w_in[:, :qd], w_in[:, qd:qd + kvd], w_in[:, qd + kvd:], tabs,
                    hq=nq, hk=nkv, hv=nkv, dv=HEAD_DIM, nv=LANES, shift=shift, kperm=kperm,
                    qk_norm=qk_norm, qg=qg, kg=kg)


def _mixer_window(h, g, w_in, w_out, sink, tabs):
    q, kt, v = _gqa_project(h, g, w_in, tabs, ROT_DIM // 2, PARTIAL_KPERM, False)
    o = _flash(q, kt, v, units=4, q_per_unit=4, k_per_unit=1, v_per_unit=1,
               streams=tuple((j, 0, 0) for j in range(4)), tq=256, tk=256, nv=LANES, dv=HEAD_DIM, out_w=4 * HEAD_DIM, band=(128, 1), mode="gqa",
               sink=sink.reshape(4, 4), name="flash_window")
    return _outproj(h, o, w_out)


def _mixer_axial(h, g, w_in, w_out, q_norm, k_norm, tabs):
    qg = jnp.pad(q_norm, (0, LANES - HEAD_DIM)).reshape(1, LANES)
    kg = k_norm.reshape(HEAD_DIM, 1)
    q, kt, v = _gqa_project(h, g, w_in, tabs, HEAD_DIM // 4, AXIAL_KPERM, True, qg, kg)
    o = _flash(q, kt, v, units=4, q_per_unit=4, k_per_unit=1, v_per_unit=1,
               streams=tuple((j, 0, 0) for j in range(4)), split=4,
               tq=1024, tk=1024, nv=LANES, dv=HEAD_DIM, out_w=4 * HEAD_DIM, mode="gqa", name="flash_axial")
    return _outproj(h, o, w_out)


def kernel(x, p, norm_mix, norm_ffn, norm_ple, norm_final, a_w_in, a_w_out, a_lam_q1, a_lam_k1, a_lam_q2, a_lam_k2, a_subln, b_w_in, b_w_out, c_w_in, c_w_out, c_sink, d_w_in, d_w_out, d_q_norm, d_k_norm, moe_w_group, moe_b_group, moe_w_expert, moe_b_expert, moe_w_gate, moe_w_up, moe_w_down, ple_w_gate, ple_w_proj):
    bn, s, d = x.shape
    assert bn == 1
    depth = p.shape[0]
    h = x[0]
    ptabs = _partial_tables(s)
    atabs = _axial_tables(s)
    row = lambda a: a.reshape(1, -1)
    for i in range(depth):
        r, kind = divmod(i, 4)
        g = row(norm_mix[i])
        if kind == 0:
            lam_init = 0.8 - 0.6 * math.exp(-0.3 * i)
            lam_rows = jnp.stack([a_lam_q1[r], a_lam_k1[r], a_lam_q2[r], a_lam_k2[r]])
            h = _mixer_diff(h, g, a_w_in[r], a_w_out[r], lam_rows, row(a_subln[r]), lam_init, ptabs)
        elif kind == 1:
            h = _mixer_dilated(h, g, b_w_in[r], b_w_out[r], ptabs)
        elif kind == 2:
            h = _mixer_window(h, g, c_w_in[r], c_w_out[r], c_sink[r], ptabs)
        else:
            h = _mixer_axial(h, g, d_w_in[r], d_w_out[r], d_q_norm[r], d_k_norm[r], atabs)
        y_sorted, pos = _moe_sparse(h, row(norm_ffn[i]), moe_w_group[i], moe_b_group[i], moe_w_expert[i],
                                    moe_b_expert[i], moe_w_gate[i], moe_w_up[i], moe_w_down[i])
        h = _ple(h, y_sorted, pos, row(norm_ple[i]), ple_w_gate[i], p[i, 0], ple_w_proj[i], row(norm_final), final=(i == depth - 1))
    return h[None]
```

```python
import functools
import math

import jax
import jax.numpy as jnp
from jax import lax
from jax.experimental import pallas as pl
from jax.experimental.pallas import tpu as pltpu

F32 = jnp.float32
BF16 = jnp.bfloat16

HEAD_DIM = 64
LANES = 128
EPS = 1e-6
LOG2E = 1.4426950408889634
NEG_INF = -1e30
ROPE_THETA = 500000.0
ROT_DIM = HEAD_DIM // 4
AXIAL_THETA = 10000.0
GRID_W = 64
MOE_GROUPS = 4
MOE_PER_GROUP = 4
MOE_EXPERTS = 16
VMEM_LIMIT = 56 * 1024 * 1024


def _cparams(sem):
    return pltpu.CompilerParams(dimension_semantics=sem, vmem_limit_bytes=VMEM_LIMIT)


def _rms(x, g):
    return x * lax.rsqrt(jnp.mean(x * x, axis=-1, keepdims=True) + EPS) * g


def _proj_kernel(h_ref, g_ref, wr_ref, wtt_ref, wv_ref, rc_ref, rs1_ref, rs2_ref, tc_ref, ts_ref,
                 rg_ref, tg_ref, r_ref, t_ref, v_ref, *, hr, ht, hv, nv, dv, v_t, shift, tperm, qk_norm):
    xn = _rms(h_ref[...], g_ref[...]).astype(BF16)
    rf = jnp.dot(xn, wr_ref[...], preferred_element_type=F32)
    rc, rs1, rs2 = rc_ref[...], rs1_ref[...], rs2_ref[...]
    for h in range(hr):
        s = rf[:, h * LANES:(h + 1) * LANES]
        if qk_norm:
            ms = jnp.sum(s * s, axis=-1, keepdims=True) * (1.0 / HEAD_DIM)
            s = s * lax.rsqrt(ms + EPS) * rg_ref[...]
        r = s * rc + pltpu.roll(s, LANES - shift, 1) * rs1 + pltpu.roll(s, shift, 1) * rs2
        r_ref[h] = r[:, :HEAD_DIM].astype(BF16)
    nt = (((1,), (1,)), ((), ()))
    tf = lax.dot_general(wtt_ref[...], xn, nt, preferred_element_type=F32)
    tc, ts = tc_ref[...], ts_ref[...]
    for h in range(ht):
        s = tf[h * HEAD_DIM:(h + 1) * HEAD_DIM, :]
        if qk_norm:
            ms = jnp.sum(s * s, axis=0, keepdims=True) * (1.0 / HEAD_DIM)
            s = s * lax.rsqrt(ms + EPS) * tg_ref[...]
        partner = jnp.concatenate([s[a:b] for a, b in tperm], axis=0)
        t_ref[h] = (s * tc + partner * ts).astype(BF16)
    if v_t:
        vf = lax.dot_general(wv_ref[...], xn, nt, preferred_element_type=F32)
        row = lax.broadcasted_iota(jnp.int32, (nv, vf.shape[1]), 0)
        for h in range(hv):
            v_ref[h] = jnp.where(row == dv, 1.0, vf[h * nv:(h + 1) * nv, :]).astype(BF16)
    else:
        vf = jnp.dot(xn, wv_ref[...], preferred_element_type=F32)
        lane = lax.broadcasted_iota(jnp.int32, vf.shape, 1)
        v_ref[...] = jnp.where((lane & (nv - 1)) == dv, 1.0, vf).astype(BF16)


def _project(h, g, w_row, w_t, wv, row_tabs, t_tabs, *, hr, ht, hv, dv, nv, v_t, shift, tperm, qk_norm,
             rg=None, tg=None, tm=512):
    s, d = h.shape
    tm = min(tm, s)
    wr_p = jnp.pad(w_row.reshape(d, hr, HEAD_DIM), ((0, 0), (0, 0), (0, LANES - HEAD_DIM)))
    wr_p = wr_p.reshape(d, hr * LANES).astype(BF16)
    wtt = w_t.T.astype(BF16)
    wv_p = jnp.pad(wv.reshape(d, hv, dv), ((0, 0), (0, 0), (0, nv - dv))).reshape(d, hv * nv).astype(BF16)
    if v_t:
        wv_p = wv_p.T
        v_shape, v_spec = (hv, nv, s), pl.BlockSpec((hv, nv, tm), lambda i: (0, 0, i))
    else:
        v_shape, v_spec = (s, hv * nv), pl.BlockSpec((tm, hv * nv), lambda i: (i, 0))
    rc, rs1, rs2 = row_tabs
    tc, ts = t_tabs
    if rg is None:
        rg = jnp.ones((1, LANES), F32)
        tg = jnp.ones((HEAD_DIM, 1), F32)
    full = lambda a: pl.BlockSpec(a.shape, lambda i: (0,) * a.ndim)
    kern = functools.partial(_proj_kernel, hr=hr, ht=ht, hv=hv, nv=nv, dv=dv, v_t=v_t, shift=shift, tperm=tperm,
                             qk_norm=qk_norm)
    return pl.pallas_call(
        kern,
        out_shape=(jax.ShapeDtypeStruct((hr, s, HEAD_DIM), BF16),
                   jax.ShapeDtypeStruct((ht, HEAD_DIM, s), BF16),
                   jax.ShapeDtypeStruct(v_shape, BF16)),
        grid=(s // tm,),
        in_specs=[pl.BlockSpec((tm, d), lambda i: (i, 0)), full(g), full(wr_p), full(wtt), full(wv_p),
                  pl.BlockSpec((tm, LANES), lambda i: (i, 0)), pl.BlockSpec((tm, LANES), lambda i: (i, 0)),
                  pl.BlockSpec((tm, LANES), lambda i: (i, 0)),
                  pl.BlockSpec((HEAD_DIM, tm), lambda i: (0, i)), pl.BlockSpec((HEAD_DIM, tm), lambda i: (0, i)),
                  full(rg), full(tg)],
        out_specs=(pl.BlockSpec((hr, tm, HEAD_DIM), lambda i: (0, i, 0)),
                   pl.BlockSpec((ht, HEAD_DIM, tm), lambda i: (0, 0, i)),
                   v_spec),
        compiler_params=_cparams(("parallel",)),
        name="proj",
    )(h, g, wr_p, wtt, wv_p, rc, rs1, rs2, tc, ts, rg, tg)


def _rope_tables(ang_list, s, scale):
    cs, sn1, sn2, ksn = [], [], [], []
    used = 0
    for ang in ang_list:
        c, sn = jnp.cos(ang), jnp.sin(ang)
        z = jnp.zeros_like(sn)
        cs += [c, c]
        sn1 += [-sn, z]
        sn2 += [z, sn]
        ksn += [-sn, sn]
        used += 2 * ang.shape[1]
    rest = HEAD_DIM - used
    ones, zeros = jnp.ones((s, rest), F32), jnp.zeros((s, rest), F32)
    c64 = jnp.concatenate(cs + [ones], axis=1)
    pad = jnp.zeros((s, LANES - HEAD_DIM), F32)
    row = (jnp.concatenate([c64, pad], axis=1), jnp.concatenate(sn1 + [zeros, pad], axis=1),
           jnp.concatenate(sn2 + [zeros, pad], axis=1))
    tr = (c64.T, jnp.concatenate(ksn + [zeros], axis=1).T)
    scaled = lambda tabs: tuple(t * scale for t in tabs)
    return {"row": row, "t": tr, "row_scaled": scaled(row), "t_scaled": scaled(tr)}


def _lane_tile(x, width):
    reps = width // LANES
    return x if reps == 1 else jnp.concatenate([x] * reps, axis=1)


def _flash_kernel(*refs, streams, split, tq, tk, nv, dv, band, n_kblocks, mode, lam_init, has_sink):
    it = iter(refs)
    q_ref, kt_ref, v_ref = next(it), next(it), next(it)
    sink_ref = next(it) if has_sink else None
    lam_ref = subln_ref = None
    if mode == "diff":
        lam_ref, subln_ref = next(it), next(it)
    o_ref = next(it)
    lse_ref = next(it) if mode == "lse" else None
    m_scr, acc_scr = next(it), next(it)

    qb, kb = pl.program_id(1), pl.program_id(2)
    nsteps = pl.num_programs(2)
    cr = tq // split

    @pl.when(kb == 0)
    def _init():
        for si, (qi, _, _) in enumerate(streams):
            if has_sink:
                m_scr[si] = jnp.full((tq, LANES), LOG2E, F32) * sink_ref[pl.program_id(0), qi]
                lane = lax.broadcasted_iota(jnp.int32, (tq, nv), 1)
                acc_scr[si] = jnp.where(lane == dv, 1.0, 0.0).astype(F32)
            else:
                m_scr[si] = jnp.full((tq, LANES), NEG_INF, F32)
                acc_scr[si] = jnp.zeros((tq, nv), F32)

    def _step():
        chunks = [(si, c * cr, qi, ki, vi) for si, (qi, ki, vi) in enumerate(streams) for c in range(split)]
        scores = [jnp.dot(q_ref[qi, r0:r0 + cr, :], kt_ref[ki], preferred_element_type=F32)
                  for _, r0, qi, ki, _ in chunks]
        for sc, (si, r0, _, _, vi) in zip(scores, chunks):
            if band is not None:
                halfw, dil, off = band
                row = lax.broadcasted_iota(jnp.int32, (cr, tk), 0) + r0
                col = lax.broadcasted_iota(jnp.int32, (cr, tk), 1)
                diff = col - row + ((kb - off) * tk)
                valid = (jnp.abs(diff) <= halfw) & ((diff & (dil - 1)) == 0)
                sc = jnp.where(valid, sc, NEG_INF)
            m_old = m_scr[si, r0:r0 + cr, :]
            m_new = jnp.maximum(m_old, jnp.broadcast_to(jnp.max(sc, axis=1, keepdims=True), (cr, LANES)))
            p = jnp.exp2(sc - _lane_tile(m_new, tk))
            if band is not None:
                p = jnp.where(valid, p, 0.0)
            alpha = jnp.exp2(m_old - m_new)
            pv = jnp.dot(p.astype(BF16), v_ref[:, vi * nv:(vi + 1) * nv], preferred_element_type=F32)
            acc_scr[si, r0:r0 + cr, :] = acc_scr[si, r0:r0 + cr, :] * _lane_tile(alpha, nv) + pv
            m_scr[si, r0:r0 + cr, :] = m_new

    if band is None:
        _step()
    else:
        _, _, off = band
        kabs = qb * (tq // tk) + kb - off
        pl.when((kabs >= 0) & (kabs < n_kblocks))(_step)

    @pl.when(kb == nsteps - 1)
    def _fin():
        if mode == "diff":
            a0, a1 = acc_scr[0], acc_scr[1]
            lam_rows = lam_ref[...]
            lam = (jnp.exp(jnp.sum(lam_rows[0:1] * lam_rows[1:2], axis=1, keepdims=True))
                   - jnp.exp(jnp.sum(lam_rows[2:3] * lam_rows[3:4], axis=1, keepdims=True)) + lam_init)
            o = a0[:, :dv] / a0[:, dv:dv + 1] - lam * (a1[:, :dv] / a1[:, dv:dv + 1])
            o = _rms(o, subln_ref[...]) * (1.0 - lam_init)
            o_ref[...] = o.astype(o_ref.dtype)
        else:
            for si in range(len(streams)):
                a = acc_scr[si]
                l = a[:, dv:dv + 1]
                o_ref[:, si * dv:(si + 1) * dv] = (a[:, :dv] / l).astype(o_ref.dtype)
                if mode == "lse":
                    lse_ref[:, si * dv:(si + 1) * dv] = m_scr[si][:, :dv] + jnp.log2(l)


def _flash(q, kt, v, *, units, q_per_unit, k_per_unit, v_per_unit, streams, tq, tk, nv, dv, out_w,
           split=1, unit0=0, band=None, mode="gqa", sink=None, lam=None, subln=None, lam_init=0.0, name="flash"):
    s = q.shape[1]
    tq, tk = min(tq, s), min(tk, s)
    assert s % tq == 0 and s % tk == 0 and (band is None or tq % tk == 0)
    n_kblocks = s // tk
    if band is None:
        nsteps = n_kblocks
        kmap = lambda u, i, j: (u + unit0, 0, j)
        vmap = lambda u, i, j: (j, u + unit0)
    else:
        halfw = band[0]
        off = -(-halfw // tk)
        nsteps = tq // tk + 2 * off
        band = (band[0], band[1], off)
        kidx = lambda i, j: jnp.clip(i * (tq // tk) + j - off, 0, n_kblocks - 1)
        kmap = lambda u, i, j: (u + unit0, 0, kidx(i, j))
        vmap = lambda u, i, j: (kidx(i, j), u + unit0)
    in_specs = [pl.BlockSpec((q_per_unit, tq, HEAD_DIM), lambda u, i, j: (u + unit0, i, 0)),
                pl.BlockSpec((k_per_unit, HEAD_DIM, tk), kmap),
                pl.BlockSpec((tk, v_per_unit * nv), vmap)]
    args = [q, kt, v]
    if sink is not None:
        in_specs.append(pl.BlockSpec(memory_space=pltpu.SMEM))
        args.append(sink)
    if mode == "diff":
        in_specs += [pl.BlockSpec(lam.shape, lambda u, i, j: (0, 0)), pl.BlockSpec(subln.shape, lambda u, i, j: (0, 0))]
        args += [lam, subln]
    out_shape = [jax.ShapeDtypeStruct((s, units * out_w), BF16)]
    out_specs = [pl.BlockSpec((tq, out_w), lambda u, i, j: (i, u))]
    if mode == "lse":
        out_shape.append(jax.ShapeDtypeStruct((s, units * out_w), F32))
        out_specs.append(pl.BlockSpec((tq, out_w), lambda u, i, j: (i, u)))
    assert tq % split == 0
    kern = functools.partial(_flash_kernel, streams=streams, split=split, tq=tq, tk=tk, nv=nv, dv=dv, band=band,
                             n_kblocks=n_kblocks, mode=mode, lam_init=lam_init, has_sink=sink is not None)
    res = pl.pallas_call(
        kern,
        out_shape=tuple(out_shape),
        grid=(units, s // tq, nsteps),
        in_specs=in_specs,
        out_specs=tuple(out_specs),
        scratch_shapes=[pltpu.VMEM((len(streams), tq, LANES), F32), pltpu.VMEM((len(streams), tq, nv), F32)],
        compiler_params=_cparams(("parallel", "parallel", "arbitrary")),
        name=name,
    )(*args)
    return res if mode == "lse" else res[0]


def _flash_t_kernel(*refs, streams, split, tq, vr, dv, mode, lam_init):
    it = iter(refs)
    qt_ref, k_ref, vt_ref = next(it), next(it), next(it)
    lam_ref = subln_ref = None
    if mode == "diff":
        lam_ref, subln_ref = next(it), next(it)
    o_ref, m_scr, acc_scr = next(it), next(it), next(it)
    kb = pl.program_id(2)
    cr = tq // split

    @pl.when(kb == 0)
    def _init():
        m_scr[...] = jnp.full(m_scr.shape, NEG_INF, F32)
        acc_scr[...] = jnp.zeros(acc_scr.shape, F32)

    chunks = [(si, c * cr, qi, ki, vi) for si, (qi, ki, vi) in enumerate(streams) for c in range(split)]
    scores = [jnp.dot(k_ref[ki], qt_ref[qi, :, c0:c0 + cr], preferred_element_type=F32)
              for _, c0, qi, ki, _ in chunks]
    for st, (si, c0, _, _, vi) in zip(scores, chunks):
        m_old = m_scr[si, :, c0:c0 + cr]
        m_new = jnp.maximum(m_old, jnp.max(st, axis=0, keepdims=True))
        p = jnp.exp2(st - m_new).astype(BF16)
        alpha = jnp.exp2(m_old - m_new)
        pv = jnp.dot(vt_ref[vi], p, preferred_element_type=F32)
        acc_scr[si, :, c0:c0 + cr] = acc_scr[si, :, c0:c0 + cr] * alpha + pv
        m_scr[si, :, c0:c0 + cr] = m_new

    @pl.when(kb == pl.num_programs(2) - 1)
    def _fin():
        if mode == "diff":
            a0, a1 = acc_scr[0], acc_scr[1]
            lam_rows = lam_ref[...]
            lam = (jnp.exp(jnp.sum(lam_rows[0:1] * lam_rows[1:2], axis=1, keepdims=True))
                   - jnp.exp(jnp.sum(lam_rows[2:3] * lam_rows[3:4], axis=1, keepdims=True)) + lam_init)
            o = a0[:dv] / a0[dv:dv + 1] - lam * (a1[:dv] / a1[dv:dv + 1])
            ms = jnp.mean(o * o, axis=0, keepdims=True)
            o = o * lax.rsqrt(ms + EPS) * subln_ref[...] * (1.0 - lam_init)
            o_ref[...] = o.T.astype(o_ref.dtype)
        else:
            for si in range(len(streams)):
                a = acc_scr[si]
                o_ref[:, si * dv:(si + 1) * dv] = (a[:dv] / a[dv:dv + 1]).T.astype(o_ref.dtype)


def _flash_t(qt, k, vt, *, units, q_per_unit, k_per_unit, v_per_unit, streams, split, tq, tk, dv, out_w,
             mode="gqa", lam=None, subln=None, lam_init=0.0, name="flash_t"):
    s, vr = qt.shape[2], vt.shape[1]
    tq, tk = min(tq, s), min(tk, s)
    assert s % tq == 0 and s % tk == 0 and tq % split == 0
    in_specs = [pl.BlockSpec((q_per_unit, HEAD_DIM, tq), lambda u, i, j: (u, 0, i)),
                pl.BlockSpec((k_per_unit, tk, HEAD_DIM), lambda u, i, j: (u, j, 0)),
                pl.BlockSpec((v_per_unit, vr, tk), lambda u, i, j: (u, 0, j))]
    args = [qt, k, vt]
    if mode == "diff":
        in_specs += [pl.BlockSpec(lam.shape, lambda u, i, j: (0, 0)), pl.BlockSpec(subln.shape, lambda u, i, j: (0, 0))]
        args += [lam, subln]
    kern = functools.partial(_flash_t_kernel, streams=streams, split=split, tq=tq, vr=vr, dv=dv, mode=mode,
                             lam_init=lam_init)
    return pl.pallas_call(
        kern,
        out_shape=jax.ShapeDtypeStruct((s, units * out_w), BF16),
        grid=(units, s // tq, s // tk),
        in_specs=in_specs,
        out_specs=pl.BlockSpec((tq, out_w), lambda u, i, j: (i, u)),
        scratch_shapes=[pltpu.VMEM((len(streams), 1, tq), F32), pltpu.VMEM((len(streams), vr, tq), F32)],
        compiler_params=_cparams(("parallel", "parallel", "arbitrary")),
        name=name,
    )(*args)


def _outproj_kernel(h_ref, o_ref, w_ref, out_ref):
    out_ref[...] = h_ref[...] + jnp.dot(o_ref[...], w_ref[...], preferred_element_type=F32)


def _outproj(h, o, w, tm=512):
    s, d = h.shape
    tm = min(tm, s)
    w = w.astype(BF16)
    return pl.pallas_call(
        _outproj_kernel,
        out_shape=jax.ShapeDtypeStruct((s, d), F32),
        grid=(s // tm,),
        in_specs=[pl.BlockSpec((tm, d), lambda i: (i, 0)), pl.BlockSpec((tm, o.shape[1]), lambda i: (i, 0)),
                  pl.BlockSpec(w.shape, lambda i: (0, 0))],
        out_specs=pl.BlockSpec((tm, d), lambda i: (i, 0)),
        compiler_params=_cparams(("parallel",)),
        name="outproj",
    )(h, o, w)


def _outproj_groups_kernel(h_ref, o0_ref, o1_ref, o2_ref, l0_ref, l1_ref, l2_ref, w_ref, out_ref):
    l0, l1, l2 = l0_ref[...], l1_ref[...], l2_ref[...]
    mx = jnp.maximum(jnp.maximum(l0, l1), l2)
    e0, e1, e2 = jnp.exp2(l0 - mx), jnp.exp2(l1 - mx), jnp.exp2(l2 - mx)
    tot = e0 + e1 + e2
    acc = h_ref[...]
    gw = o0_ref.shape[1]
    for g, (o_ref, e) in enumerate(((o0_ref, e0), (o1_ref, e1), (o2_ref, e2))):
        og = (o_ref[...].astype(F32) * (e / tot)).astype(BF16)
        acc = acc + jnp.dot(og, w_ref[g * gw:(g + 1) * gw, :], preferred_element_type=F32)
    out_ref[...] = acc


def _outproj_groups(h, os_, ls_, w, tm=512):
    s, d = h.shape
    tm = min(tm, s)
    w = w.astype(BF16)
    gw = os_[0].shape[1]
    row = lambda width: pl.BlockSpec((tm, width), lambda i: (i, 0))
    return pl.pallas_call(
        _outproj_groups_kernel,
        out_shape=jax.ShapeDtypeStruct((s, d), F32),
        grid=(s // tm,),
        in_specs=[row(d)] + [row(gw)] * 6 + [pl.BlockSpec(w.shape, lambda i: (0, 0))],
        out_specs=row(d),
        compiler_params=_cparams(("parallel",)),
        name="outproj_groups",
    )(h, *os_, *ls_, w)


def _route(logits):
    lane = lax.broadcasted_iota(jnp.int32, logits.shape, 1).astype(F32)
    big = 1e6
    gl = jnp.where(lane < MOE_GROUPS, logits, NEG_INF)
    gmax = jnp.max(gl, axis=1, keepdims=True)
    gidx = jnp.min(jnp.where(gl == gmax, lane, big), axis=1, keepdims=True)
    gw = 1.0 / jnp.sum(jnp.exp(gl - gmax), axis=1, keepdims=True)
    lo = MOE_GROUPS + gidx * MOE_PER_GROUP
    el = jnp.where((lane >= lo) & (lane < lo + MOE_PER_GROUP), logits, NEG_INF)
    v1 = jnp.max(el, axis=1, keepdims=True)
    i1 = jnp.min(jnp.where(el == v1, lane, big), axis=1, keepdims=True)
    el2 = jnp.where(lane == i1, NEG_INF, el)
    v2 = jnp.max(el2, axis=1, keepdims=True)
    i2 = jnp.min(jnp.where(el2 == v2, lane, big), axis=1, keepdims=True)
    e2 = jnp.exp(v2 - v1)
    w1 = gw / (1.0 + e2)
    w2 = w1 * e2
    return jnp.where(lane == i1, w1, 0.0) + jnp.where(lane == i2, w2, 0.0), gidx


MOE_ROW_TILE = 1024
INFO_GROUP_LANE = 0
INFO_RANK_LANE = 1
SUBLANES = 8


def _to_slabs(ref, x):
    for c in range(SUBLANES):
        ref[:, c, :] = x[:, c * LANES:(c + 1) * LANES]


def _from_slabs(ref):
    return jnp.concatenate([ref[:, c, :] for c in range(SUBLANES)], axis=1)


def _moe_route_kernel(h_ref, g_ref, wr_ref, br_ref, tri_ref, x3_ref, info_ref, cnt_ref, run_scr):
    @pl.when(pl.program_id(0) == 0)
    def _init():
        run_scr[...] = jnp.zeros_like(run_scr)

    xn = _rms(h_ref[...], g_ref[...])
    logits = jnp.dot(xn, wr_ref[...], preferred_element_type=F32, precision=lax.Precision.HIGHEST) + br_ref[...]
    comb, gidx = _route(logits)
    lane = lax.broadcasted_iota(jnp.int32, comb.shape, 1).astype(F32)
    onehot = lane == gidx
    before = jnp.dot(tri_ref[...], onehot.astype(BF16), preferred_element_type=F32)
    run = run_scr[...]
    rank = jnp.sum(jnp.where(onehot, before + run, 0.0), axis=1, keepdims=True)
    run = run + jnp.sum(onehot.astype(F32), axis=0, keepdims=True)
    run_scr[...] = run
    cnt_ref[...] = run
    _to_slabs(x3_ref, xn)
    info_ref[...] = jnp.where(lane == INFO_GROUP_LANE, gidx, jnp.where(lane == INFO_RANK_LANE, rank, comb))


def _moe_dispatch_kernel(pos_ref, x3_ref, info_ref, xs_in_ref, infos_in_ref, xs_ref, infos_ref, sem):
    del xs_in_ref, infos_in_ref
    tm = x3_ref.shape[0]

    def issue(t, c):
        p = pos_ref[0, t]
        pltpu.make_async_copy(x3_ref.at[pl.ds(t, 1)], xs_ref.at[pl.ds(p, 1)], sem).start()
        pltpu.make_async_copy(info_ref.at[pl.ds(t, 1)], infos_ref.at[pl.ds(p, 1)], sem).start()
        return c

    lax.fori_loop(0, tm, issue, 0, unroll=8)
    pltpu.make_async_copy(x3_ref, xs_ref.at[pl.ds(0, tm)], sem).wait()
    pltpu.make_async_copy(info_ref, infos_ref.at[pl.ds(0, tm)], sem).wait()


def _moe_expert_kernel(tg_ref, nv_ref, xs_ref, infos_ref, wg_ref, wu_ref, wd_ref, y_ref, x_scr, acc_scr):
    j, e = pl.program_id(0), pl.program_id(1)
    last = pl.num_programs(1) - 1

    @pl.when(j < nv_ref[0])
    def _live():
        @pl.when(e == 0)
        def _first():
            acc_scr[...] = jnp.zeros_like(acc_scr)
            x_scr[...] = _from_slabs(xs_ref).astype(BF16)

        x = x_scr[...]
        info = infos_ref[...]
        lane = lax.broadcasted_iota(jnp.int32, info.shape, 1)
        eid = tg_ref[j] * MOE_PER_GROUP + e + MOE_GROUPS
        we = jnp.sum(jnp.where(lane == eid, info, 0.0), axis=1, keepdims=True)
        gate = jnp.dot(x, wg_ref[0], preferred_element_type=F32)
        up = jnp.dot(x, wu_ref[0], preferred_element_type=F32)
        act = gate * jax.nn.sigmoid(gate) * up * we
        acc_scr[...] += jnp.dot(act.astype(BF16), wd_ref[0], preferred_element_type=F32)

        @pl.when(e == last)
        def _out():
            _to_slabs(y_ref, acc_scr[...])

    @pl.when((j >= nv_ref[0]) & (e == last))
    def _dead():
        y_ref[...] = jnp.zeros_like(y_ref)


def _moe_sparse(h, g, w_group, b_group, w_expert, b_expert, w_gate, w_up, w_down, tm=1024, tmd=512):
    s, d = h.shape
    assert d == SUBLANES * LANES
    tm, tmd, tb = min(tm, s), min(tmd, s), min(MOE_ROW_TILE, s)
    ne, _, ff = w_gate.shape
    nr = MOE_GROUPS + MOE_EXPERTS
    wr = jnp.pad(jnp.concatenate([w_group, w_expert], axis=1), ((0, 0), (0, LANES - nr)))
    br = jnp.pad(jnp.concatenate([b_group, b_expert]), (0, LANES - nr)).reshape(1, LANES)
    tri = (lax.broadcasted_iota(jnp.int32, (tm, tm), 0) > lax.broadcasted_iota(jnp.int32, (tm, tm), 1)).astype(BF16)
    const = lambda a: pl.BlockSpec(a.shape, lambda i: (0, 0))
    slab = lambda rows: pl.BlockSpec((rows, SUBLANES, LANES), lambda i: (i, 0, 0))
    x3, info, cnt = pl.pallas_call(
        _moe_route_kernel,
        out_shape=(jax.ShapeDtypeStruct((s, SUBLANES, LANES), F32), jax.ShapeDtypeStruct((s, LANES), F32),
                   jax.ShapeDtypeStruct((1, LANES), F32)),
        grid=(s // tm,),
        in_specs=[pl.BlockSpec((tm, d), lambda i: (i, 0)), const(g), const(wr), const(br), const(tri)],
        out_specs=(slab(tm), pl.BlockSpec((tm, LANES), lambda i: (i, 0)), pl.BlockSpec((1, LANES), lambda i: (0, 0))),
        scratch_shapes=[pltpu.VMEM((1, LANES), F32)],
        compiler_params=_cparams(("arbitrary",)),
        name="moe_route",
    )(h, g, wr, br, tri)

    counts = cnt[0, :MOE_GROUPS].astype(jnp.int32)
    padded = ((counts + tb - 1) // tb) * tb
    ends = jnp.cumsum(padded)
    starts = ends - padded
    tok_group = info[:, INFO_GROUP_LANE].astype(jnp.int32)
    tok_rank = info[:, INFO_RANK_LANE].astype(jnp.int32)
    pos = starts[tok_group] + tok_rank
    n_tiles = s // tb + MOE_GROUPS
    p_rows = n_tiles * tb
    tile_group = jnp.minimum(jnp.sum((jnp.arange(n_tiles) * tb)[:, None] >= ends[None, :], axis=1), MOE_GROUPS - 1)
    n_valid = (ends[-1] // tb).reshape(1)

    anyspec = pl.BlockSpec(memory_space=pl.ANY)
    xs, infos = pl.pallas_call(
        _moe_dispatch_kernel,
        out_shape=(jax.ShapeDtypeStruct((p_rows, SUBLANES, LANES), F32), jax.ShapeDtypeStruct((p_rows, LANES), F32)),
        grid=(s // tmd,),
        in_specs=[pl.BlockSpec((None, 1, tmd), lambda i: (i, 0, 0), memory_space=pltpu.SMEM),
                  slab(tmd), pl.BlockSpec((tmd, LANES), lambda i: (i, 0)), anyspec, anyspec],
        out_specs=(anyspec, anyspec),
        scratch_shapes=[pltpu.SemaphoreType.DMA],
        input_output_aliases={3: 0, 4: 1},
        compiler_params=_cparams(("arbitrary",)),
        name="moe_dispatch",
    )(pos.reshape(s // tmd, 1, tmd), x3, info, jnp.zeros((p_rows, SUBLANES, LANES), F32),
      jnp.zeros((p_rows, LANES), F32))

    wg, wu, wd = w_gate.astype(BF16), w_up.astype(BF16), w_down.astype(BF16)
    wsel = lambda j, e, tg, nv: (jnp.where(j < nv[0], tg[j] * MOE_PER_GROUP + e, ne - 1), 0, 0)
    y = pl.pallas_call(
        _moe_expert_kernel,
        out_shape=jax.ShapeDtypeStruct((p_rows, SUBLANES, LANES), F32),
        grid_spec=pltpu.PrefetchScalarGridSpec(
            num_scalar_prefetch=2,
            grid=(n_tiles, MOE_PER_GROUP),
            in_specs=[pl.BlockSpec((tb, SUBLANES, LANES), lambda j, e, tg, nv: (j, 0, 0)),
                      pl.BlockSpec((tb, LANES), lambda j, e, tg, nv: (j, 0)),
                      pl.BlockSpec((1, d, ff), wsel), pl.BlockSpec((1, d, ff), wsel), pl.BlockSpec((1, ff, d), wsel)],
            out_specs=pl.BlockSpec((tb, SUBLANES, LANES), lambda j, e, tg, nv: (j, 0, 0)),
            scratch_shapes=[pltpu.VMEM((tb, d), BF16), pltpu.VMEM((tb, d), F32)]),
        compiler_params=_cparams(("arbitrary", "arbitrary")),
        name="moe_expert",
    )(tile_group.astype(jnp.int32), n_valid.astype(jnp.int32), xs, infos, wg, wu, wd)
    return y, pos


def _ple_kernel(pos_ref, h_ref, y_hbm, g_ref, wg_ref, p_ref, wp_ref, gf_ref, out_ref, ybuf, sem, *, final):
    tm = h_ref.shape[0]

    def issue(t, c):
        pltpu.make_async_copy(y_hbm.at[pl.ds(pos_ref[0, t], 1)], ybuf.at[pl.ds(t, 1)], sem).start()
        return c

    lax.fori_loop(0, tm, issue, 0, unroll=8)
    pltpu.make_async_copy(y_hbm.at[pl.ds(0, tm)], ybuf, sem).wait()
    x = h_ref[...] + _from_slabs(ybuf)
    xn = _rms(x, g_ref[...]).astype(BF16)
    gate = jax.nn.sigmoid(jnp.dot(xn, wg_ref[...], preferred_element_type=F32))
    proj = jnp.dot(p_ref[...].astype(BF16), wp_ref[...], preferred_element_type=F32)
    y = x + gate * proj
    if final:
        y = _rms(y, gf_ref[...])
    out_ref[...] = y


def _ple(h, y_sorted, pos, g, wg, p, wp, gf, final, tm=512):
    s, d = h.shape
    tm = min(tm, s)
    wg, wp = wg.astype(BF16), wp.astype(BF16)
    const = lambda a: pl.BlockSpec(a.shape, lambda i: (0, 0))
    return pl.pallas_call(
        functools.partial(_ple_kernel, final=final),
        out_shape=jax.ShapeDtypeStruct((s, d), F32),
        grid=(s // tm,),
        in_specs=[pl.BlockSpec((None, 1, tm), lambda i: (i, 0, 0), memory_space=pltpu.SMEM),
                  pl.BlockSpec((tm, d), lambda i: (i, 0)), pl.BlockSpec(memory_space=pl.ANY), const(g), const(wg),
                  pl.BlockSpec((tm, p.shape[1]), lambda i: (i, 0)), const(wp), const(gf)],
        out_specs=pl.BlockSpec((tm, d), lambda i: (i, 0)),
        scratch_shapes=[pltpu.VMEM((tm, SUBLANES, LANES), F32), pltpu.SemaphoreType.DMA],
        compiler_params=_cparams(("arbitrary",)),
        name="ple",
    )(pos.reshape(s // tm, 1, tm), h, y_sorted, g, wg, p, wp, gf)


PARTIAL_KPERM = ((8, 16), (0, 8), (16, 64))
AXIAL_KPERM = ((16, 32), (0, 16), (48, 64), (32, 48))
B_PAIRS = ((128, 1), (512, 4), (2048, 16))


def _partial_tables(s):
    inv = ROPE_THETA ** (-jnp.arange(0, ROT_DIM, 2, dtype=F32) / ROT_DIM)
    ang = jnp.arange(s).astype(F32)[:, None] * inv[None, :]
    return _rope_tables([ang], s, LOG2E * HEAD_DIM ** -0.5)


def _axial_tables(s):
    half = HEAD_DIM // 2
    inv = AXIAL_THETA ** (-jnp.arange(0, half, 2, dtype=F32) / half)
    t = jnp.arange(s)
    ang_r = (t // GRID_W).astype(F32)[:, None] * inv[None, :]
    ang_c = (t % GRID_W).astype(F32)[:, None] * inv[None, :]
    return _rope_tables([ang_r, ang_c], s, LOG2E * HEAD_DIM ** -0.5)


def _mixer_diff(h, g, w_in, w_out, lam_rows, subln, lam_init, tabs):
    d = h.shape[1]
    heads = d // (2 * HEAD_DIM)
    aw = 2 * heads * HEAD_DIM
    dv = 2 * HEAD_DIM
    k, qt, vt = _project(h, g, w_in[:, aw:2 * aw], w_in[:, :aw], w_in[:, 2 * aw:], tabs["row"], tabs["t_scaled"],
                         hr=2 * heads, ht=2 * heads, hv=heads, dv=dv, nv=dv + 16, v_t=True,
                         shift=ROT_DIM // 2, tperm=PARTIAL_KPERM, qk_norm=False)
    o = _flash_t(qt, k, vt, units=heads, q_per_unit=2, k_per_unit=2, v_per_unit=1,
                 streams=((0, 0, 0), (1, 1, 0)), split=4, tq=1024, tk=1024, dv=dv, out_w=dv,
                 mode="diff", lam=lam_rows, subln=subln.reshape(dv, 1), lam_init=lam_init, name="flash_diff")
    return _outproj(h, o, w_out)


def _mixer_dilated(h, g, w_in, w_out, tabs):
    nh, hg = 12, 4
    bw = nh * HEAD_DIM
    q, kt, v = _project(h, g, w_in[:, :bw], w_in[:, bw:2 * bw], w_in[:, 2 * bw:], tabs["row_scaled"], tabs["t"],
                        hr=nh, ht=nh, hv=nh, dv=HEAD_DIM, nv=LANES, v_t=False,
                        shift=ROT_DIM // 2, tperm=PARTIAL_KPERM, qk_norm=False)
    os_, ls_ = [], []
    for gi, (win, dil) in enumerate(B_PAIRS):
        halfw = (win // (2 * dil)) * dil
        o, lse = _flash(q, kt, v, units=1, unit0=gi, q_per_unit=hg, k_per_unit=hg, v_per_unit=hg,
                        streams=tuple((j, j, j) for j in range(hg)), tq=1024, tk=max(halfw, 256),
                        nv=LANES, dv=HEAD_DIM, out_w=hg * HEAD_DIM, band=(halfw, dil), mode="lse",
                        name=f"flash_dilated{gi}")
        os_.append(o)
        ls_.append(lse)
    return _outproj_groups(h, os_, ls_, w_out)


GQA_HEADS, GQA_KV_HEADS = 16, 4


def _mixer_window(h, g, w_in, w_out, sink, tabs):
    qd, kvd = GQA_HEADS * HEAD_DIM, GQA_KV_HEADS * HEAD_DIM
    grp = GQA_HEADS // GQA_KV_HEADS
    q, kt, v = _project(h, g, w_in[:, :qd], w_in[:, qd:qd + kvd], w_in[:, qd + kvd:], tabs["row_scaled"], tabs["t"],
                        hr=GQA_HEADS, ht=GQA_KV_HEADS, hv=GQA_KV_HEADS, dv=HEAD_DIM, nv=LANES, v_t=False,
                        shift=ROT_DIM // 2, tperm=PARTIAL_KPERM, qk_norm=False)
    o = _flash(q, kt, v, units=GQA_KV_HEADS, q_per_unit=grp, k_per_unit=1, v_per_unit=1,
               streams=tuple((j, 0, 0) for j in range(grp)), tq=256, tk=256, nv=LANES, dv=HEAD_DIM,
               out_w=grp * HEAD_DIM, band=(128, 1), mode="gqa", sink=sink.reshape(GQA_KV_HEADS, grp),
               name="flash_window")
    return _outproj(h, o, w_out)


def _mixer_axial(h, g, w_in, w_out, q_norm, k_norm, tabs):
    qd, kvd = GQA_HEADS * HEAD_DIM, GQA_KV_HEADS * HEAD_DIM
    grp = GQA_HEADS // GQA_KV_HEADS
    kg = jnp.pad(k_norm, (0, LANES - HEAD_DIM)).reshape(1, LANES)
    qg = q_norm.reshape(HEAD_DIM, 1)
    k, qt, vt = _project(h, g, w_in[:, qd:qd + kvd], w_in[:, :qd], w_in[:, qd + kvd:], tabs["row"], tabs["t_scaled"],
                         hr=GQA_KV_HEADS, ht=GQA_HEADS, hv=GQA_KV_HEADS, dv=HEAD_DIM, nv=HEAD_DIM + 16, v_t=True,
                         shift=HEAD_DIM // 4, tperm=AXIAL_KPERM, qk_norm=True, rg=kg, tg=qg)
    o = _flash_t(qt, k, vt, units=GQA_KV_HEADS, q_per_unit=grp, k_per_unit=1, v_per_unit=1,
                 streams=tuple((j, 0, 0) for j in range(grp)), split=4, tq=1024, tk=1024, dv=HEAD_DIM,
                 out_w=grp * HEAD_DIM, mode="gqa", name="flash_axial")
    return _outproj(h, o, w_out)


def kernel(x, p, norm_mix, norm_ffn, norm_ple, norm_final, a_w_in, a_w_out, a_lam_q1, a_lam_k1, a_lam_q2, a_lam_k2, a_subln, b_w_in, b_w_out, c_w_in, c_w_out, c_sink, d_w_in, d_w_out, d_q_norm, d_k_norm, moe_w_group, moe_b_group, moe_w_expert, moe_b_expert, moe_w_gate, moe_w_up, moe_w_down, ple_w_gate, ple_w_proj):
    bn, s, d = x.shape
    assert bn == 1
    depth = p.shape[0]
    h = x[0]
    ptabs = _partial_tables(s)
    atabs = _axial_tables(s)
    row = lambda a: a.reshape(1, -1)
    for i in range(depth):
        r, kind = divmod(i, 4)
        g = row(norm_mix[i])
        if kind == 0:
            lam_init = 0.8 - 0.6 * math.exp(-0.3 * i)
            lam_rows = jnp.stack([a_lam_q1[r], a_lam_k1[r], a_lam_q2[r], a_lam_k2[r]])
            h = _mixer_diff(h, g, a_w_in[r], a_w_out[r], lam_rows, row(a_subln[r]), lam_init, ptabs)
        elif kind == 1:
            h = _mixer_dilated(h, g, b_w_in[r], b_w_out[r], ptabs)
        elif kind == 2:
            h = _mixer_window(h, g, c_w_in[r], c_w_out[r], c_sink[r], ptabs)
        else:
            h = _mixer_axial(h, g, d_w_in[r], d_w_out[r], d_q_norm[r], d_k_norm[r], atabs)
        y_sorted, pos = _moe_sparse(h, row(norm_ffn[i]), moe_w_group[i], moe_b_group[i], moe_w_expert[i],
                                    moe_b_expert[i], moe_w_gate[i], moe_w_up[i], moe_w_down[i])
        h = _ple(h, y_sorted, pos, row(norm_ple[i]), ple_w_gate[i], p[i, 0], ple_w_proj[i], row(norm_final), final=(i == depth - 1))
    return h[None]
```

```python
import functools
import math

import jax
import jax.numpy as jnp
from jax import lax
from jax.experimental import pallas as pl
from jax.experimental.pallas import tpu as pltpu

F32 = jnp.float32
BF16 = jnp.bfloat16

HEAD_DIM = 64
LANES = 128
EPS = 1e-6
LOG2E = 1.4426950408889634
NEG_INF = -1e30
M_FLOOR = -1e29
ROPE_THETA = 500000.0
ROT_DIM = HEAD_DIM // 4
AXIAL_THETA = 10000.0
GRID_W = 64
MOE_GROUPS = 4
MOE_PER_GROUP = 4
MOE_EXPERTS = 16
VMEM_LIMIT = 56 * 1024 * 1024


def _cparams(sem):
    return pltpu.CompilerParams(dimension_semantics=sem, vmem_limit_bytes=VMEM_LIMIT)


def _rms(x, g):
    return x * lax.rsqrt(jnp.mean(x * x, axis=-1, keepdims=True) + EPS) * g


def _proj_kernel(h_ref, g_ref, wr_ref, wtt_ref, wv_ref, rc_ref, rs1_ref, rs2_ref, tc_ref, ts_ref,
                 rg_ref, tg_ref, r_ref, t_ref, v_ref, *, hr, ht, hv, nv, dv, v_t, shift, tperm, qk_norm):
    xn = _rms(h_ref[...], g_ref[...]).astype(BF16)
    rf = jnp.dot(xn, wr_ref[...], preferred_element_type=F32)
    rc, rs1, rs2 = rc_ref[...], rs1_ref[...], rs2_ref[...]
    for h in range(hr):
        s = rf[:, h * LANES:(h + 1) * LANES]
        if qk_norm:
            ms = jnp.sum(s * s, axis=-1, keepdims=True) * (1.0 / HEAD_DIM)
            s = s * lax.rsqrt(ms + EPS) * rg_ref[...]
        r = s * rc + pltpu.roll(s, LANES - shift, 1) * rs1 + pltpu.roll(s, shift, 1) * rs2
        r_ref[h] = r[:, :HEAD_DIM].astype(BF16)
    nt = (((1,), (1,)), ((), ()))
    tf = lax.dot_general(wtt_ref[...], xn, nt, preferred_element_type=F32)
    tc, ts = tc_ref[...], ts_ref[...]
    for h in range(ht):
        s = tf[h * HEAD_DIM:(h + 1) * HEAD_DIM, :]
        if qk_norm:
            ms = jnp.sum(s * s, axis=0, keepdims=True) * (1.0 / HEAD_DIM)
            s = s * lax.rsqrt(ms + EPS) * tg_ref[...]
        partner = jnp.concatenate([s[a:b] for a, b in tperm], axis=0)
        t_ref[h] = (s * tc + partner * ts).astype(BF16)
    if v_t:
        vf = lax.dot_general(wv_ref[...], xn, nt, preferred_element_type=F32)
        row = lax.broadcasted_iota(jnp.int32, (nv, vf.shape[1]), 0)
        for h in range(hv):
            v_ref[h] = jnp.where(row == dv, 1.0, vf[h * nv:(h + 1) * nv, :]).astype(BF16)
    else:
        vf = jnp.dot(xn, wv_ref[...], preferred_element_type=F32)
        lane = lax.broadcasted_iota(jnp.int32, vf.shape, 1)
        v_ref[...] = jnp.where((lane & (nv - 1)) == dv, 1.0, vf).astype(BF16)


def _project(h, g, w_row, w_t, wv, row_tabs, t_tabs, *, hr, ht, hv, dv, nv, v_t, shift, tperm, qk_norm,
             rg=None, tg=None, tm=512):
    s, d = h.shape
    tm = min(tm, s)
    wr_p = jnp.pad(w_row.reshape(d, hr, HEAD_DIM), ((0, 0), (0, 0), (0, LANES - HEAD_DIM)))
    wr_p = wr_p.reshape(d, hr * LANES).astype(BF16)
    wtt = w_t.T.astype(BF16)
    wv_p = jnp.pad(wv.reshape(d, hv, dv), ((0, 0), (0, 0), (0, nv - dv))).reshape(d, hv * nv).astype(BF16)
    if v_t:
        wv_p = wv_p.T
        v_shape, v_spec = (hv, nv, s), pl.BlockSpec((hv, nv, tm), lambda i: (0, 0, i))
    else:
        v_shape, v_spec = (s, hv * nv), pl.BlockSpec((tm, hv * nv), lambda i: (i, 0))
    rc, rs1, rs2 = row_tabs
    tc, ts = t_tabs
    if rg is None:
        rg = jnp.ones((1, LANES), F32)
        tg = jnp.ones((HEAD_DIM, 1), F32)
    full = lambda a: pl.BlockSpec(a.shape, lambda i: (0,) * a.ndim)
    kern = functools.partial(_proj_kernel, hr=hr, ht=ht, hv=hv, nv=nv, dv=dv, v_t=v_t, shift=shift, tperm=tperm,
                             qk_norm=qk_norm)
    return pl.pallas_call(
        kern,
        out_shape=(jax.ShapeDtypeStruct((hr, s, HEAD_DIM), BF16),
                   jax.ShapeDtypeStruct((ht, HEAD_DIM, s), BF16),
                   jax.ShapeDtypeStruct(v_shape, BF16)),
        grid=(s // tm,),
        in_specs=[pl.BlockSpec((tm, d), lambda i: (i, 0)), full(g), full(wr_p), full(wtt), full(wv_p),
                  pl.BlockSpec((tm, LANES), lambda i: (i, 0)), pl.BlockSpec((tm, LANES), lambda i: (i, 0)),
                  pl.BlockSpec((tm, LANES), lambda i: (i, 0)),
                  pl.BlockSpec((HEAD_DIM, tm), lambda i: (0, i)), pl.BlockSpec((HEAD_DIM, tm), lambda i: (0, i)),
                  full(rg), full(tg)],
        out_specs=(pl.BlockSpec((hr, tm, HEAD_DIM), lambda i: (0, i, 0)),
                   pl.BlockSpec((ht, HEAD_DIM, tm), lambda i: (0, 0, i)),
                   v_spec),
        compiler_params=_cparams(("parallel",)),
        name="proj",
    )(h, g, wr_p, wtt, wv_p, rc, rs1, rs2, tc, ts, rg, tg)


def _rope_tables(ang_list, s, scale):
    cs, sn1, sn2, ksn = [], [], [], []
    used = 0
    for ang in ang_list:
        c, sn = jnp.cos(ang), jnp.sin(ang)
        z = jnp.zeros_like(sn)
        cs += [c, c]
        sn1 += [-sn, z]
        sn2 += [z, sn]
        ksn += [-sn, sn]
        used += 2 * ang.shape[1]
    rest = HEAD_DIM - used
    ones, zeros = jnp.ones((s, rest), F32), jnp.zeros((s, rest), F32)
    c64 = jnp.concatenate(cs + [ones], axis=1)
    pad = jnp.zeros((s, LANES - HEAD_DIM), F32)
    row = (jnp.concatenate([c64, pad], axis=1), jnp.concatenate(sn1 + [zeros, pad], axis=1),
           jnp.concatenate(sn2 + [zeros, pad], axis=1))
    tr = (c64.T, jnp.concatenate(ksn + [zeros], axis=1).T)
    scaled = lambda tabs: tuple(t * scale for t in tabs)
    return {"row": row, "t": tr, "row_scaled": scaled(row), "t_scaled": scaled(tr)}


def _lane_tile(x, width):
    reps = width // LANES
    return x if reps == 1 else jnp.concatenate([x] * reps, axis=1)


def _flash_kernel(*refs, streams, split, tq, tk, nv, dv, band, n_kblocks, mode, lam_init, has_sink):
    it = iter(refs)
    q_ref, kt_ref, v_ref = next(it), next(it), next(it)
    sink_ref = next(it) if has_sink else None
    lam_ref = subln_ref = None
    if mode == "diff":
        lam_ref, subln_ref = next(it), next(it)
    o_ref = next(it)
    lse_ref = next(it) if mode == "lse" else None
    m_scr, acc_scr = next(it), next(it)

    qb, kb = pl.program_id(1), pl.program_id(2)
    nsteps = pl.num_programs(2)
    cr = tq // split

    @pl.when(kb == 0)
    def _init():
        for si, (qi, _, _) in enumerate(streams):
            if has_sink:
                sink2 = jnp.full((tq, LANES), LOG2E, F32) * sink_ref[pl.program_id(0), qi]
                m0 = jnp.maximum(sink2, M_FLOOR)
                m_scr[si] = m0
                lane = lax.broadcasted_iota(jnp.int32, (tq, nv), 1)
                acc_scr[si] = jnp.where(lane == dv, _lane_tile(jnp.exp2(sink2 - m0), nv), 0.0)
            else:
                m_scr[si] = jnp.full((tq, LANES), M_FLOOR, F32)
                acc_scr[si] = jnp.zeros((tq, nv), F32)

    def _step():
        chunks = [(si, c * cr, qi, ki, vi) for si, (qi, ki, vi) in enumerate(streams) for c in range(split)]
        scores = [jnp.dot(q_ref[qi, r0:r0 + cr, :], kt_ref[ki], preferred_element_type=F32)
                  for _, r0, qi, ki, _ in chunks]
        for sc, (si, r0, _, _, vi) in zip(scores, chunks):
            if band is not None:
                halfw, dil, off = band
                row = lax.broadcasted_iota(jnp.int32, (cr, tk), 0) + r0
                col = lax.broadcasted_iota(jnp.int32, (cr, tk), 1)
                diff = col - row + ((kb - off) * tk)
                valid = jnp.abs(diff) <= halfw
                if dil > 1:
                    valid = valid & ((diff & (dil - 1)) == 0)
                sc = jnp.where(valid, sc, NEG_INF)
            m_old = m_scr[si, r0:r0 + cr, :]
            m_new = jnp.maximum(m_old, jnp.broadcast_to(jnp.max(sc, axis=1, keepdims=True), (cr, LANES)))
            p = jnp.exp2(sc - _lane_tile(m_new, tk))
            alpha = jnp.exp2(m_old - m_new)
            pv = jnp.dot(p.astype(BF16), v_ref[:, vi * nv:(vi + 1) * nv], preferred_element_type=F32)
            acc_scr[si, r0:r0 + cr, :] = acc_scr[si, r0:r0 + cr, :] * _lane_tile(alpha, nv) + pv
            m_scr[si, r0:r0 + cr, :] = m_new

    if band is None:
        _step()
    else:
        _, _, off = band
        kabs = qb * (tq // tk) + kb - off
        pl.when((kabs >= 0) & (kabs < n_kblocks))(_step)

    @pl.when(kb == nsteps - 1)
    def _fin():
        if mode == "diff":
            a0, a1 = acc_scr[0], acc_scr[1]
            lam_rows = lam_ref[...]
            lam = (jnp.exp(jnp.sum(lam_rows[0:1] * lam_rows[1:2], axis=1, keepdims=True))
                   - jnp.exp(jnp.sum(lam_rows[2:3] * lam_rows[3:4], axis=1, keepdims=True)) + lam_init)
            o = a0[:, :dv] / a0[:, dv:dv + 1] - lam * (a1[:, :dv] / a1[:, dv:dv + 1])
            o = _rms(o, subln_ref[...]) * (1.0 - lam_init)
            o_ref[...] = o.astype(o_ref.dtype)
        else:
            for si in range(len(streams)):
                a = acc_scr[si]
                l = a[:, dv:dv + 1]
                o_ref[:, si * dv:(si + 1) * dv] = (a[:, :dv] / l).astype(o_ref.dtype)
                if mode == "lse":
                    lse_ref[:, si * dv:(si + 1) * dv] = m_scr[si][:, :dv] + jnp.log2(l)


def _flash(q, kt, v, *, units, q_per_unit, k_per_unit, v_per_unit, streams, tq, tk, nv, dv, out_w,
           split=1, unit0=0, band=None, mode="gqa", sink=None, lam=None, subln=None, lam_init=0.0, name="flash"):
    s = q.shape[1]
    tq, tk = min(tq, s), min(tk, s)
    assert s % tq == 0 and s % tk == 0 and (band is None or tq % tk == 0)
    n_kblocks = s // tk
    if band is None:
        nsteps = n_kblocks
        kmap = lambda u, i, j: (u + unit0, 0, j)
        vmap = lambda u, i, j: (j, u + unit0)
    else:
        halfw = band[0]
        off = -(-halfw // tk)
        nsteps = tq // tk + 2 * off
        band = (band[0], band[1], off)
        kidx = lambda i, j: jnp.clip(i * (tq // tk) + j - off, 0, n_kblocks - 1)
        kmap = lambda u, i, j: (u + unit0, 0, kidx(i, j))
        vmap = lambda u, i, j: (kidx(i, j), u + unit0)
    in_specs = [pl.BlockSpec((q_per_unit, tq, HEAD_DIM), lambda u, i, j: (u + unit0, i, 0)),
                pl.BlockSpec((k_per_unit, HEAD_DIM, tk), kmap),
                pl.BlockSpec((tk, v_per_unit * nv), vmap)]
    args = [q, kt, v]
    if sink is not None:
        in_specs.append(pl.BlockSpec(memory_space=pltpu.SMEM))
        args.append(sink)
    if mode == "diff":
        in_specs += [pl.BlockSpec(lam.shape, lambda u, i, j: (0, 0)), pl.BlockSpec(subln.shape, lambda u, i, j: (0, 0))]
        args += [lam, subln]
    out_shape = [jax.ShapeDtypeStruct((s, units * out_w), BF16)]
    out_specs = [pl.BlockSpec((tq, out_w), lambda u, i, j: (i, u))]
    if mode == "lse":
        out_shape.append(jax.ShapeDtypeStruct((s, units * out_w), F32))
        out_specs.append(pl.BlockSpec((tq, out_w), lambda u, i, j: (i, u)))
    assert tq % split == 0
    kern = functools.partial(_flash_kernel, streams=streams, split=split, tq=tq, tk=tk, nv=nv, dv=dv, band=band,
                             n_kblocks=n_kblocks, mode=mode, lam_init=lam_init, has_sink=sink is not None)
    res = pl.pallas_call(
        kern,
        out_shape=tuple(out_shape),
        grid=(units, s // tq, nsteps),
        in_specs=in_specs,
        out_specs=tuple(out_specs),
        scratch_shapes=[pltpu.VMEM((len(streams), tq, LANES), F32), pltpu.VMEM((len(streams), tq, nv), F32)],
        compiler_params=_cparams(("parallel", "parallel", "arbitrary")),
        name=name,
    )(*args)
    return res if mode == "lse" else res[0]


def _outproj_kernel(h_ref, o_ref, w_ref, out_ref):
    out_ref[...] = h_ref[...] + jnp.dot(o_ref[...], w_ref[...], preferred_element_type=F32)


def _outproj(h, o, w, tm=512):
    s, d = h.shape
    tm = min(tm, s)
    w = w.astype(BF16)
    return pl.pallas_call(
        _outproj_kernel,
        out_shape=jax.ShapeDtypeStruct((s, d), F32),
        grid=(s // tm,),
        in_specs=[pl.BlockSpec((tm, d), lambda i: (i, 0)), pl.BlockSpec((tm, o.shape[1]), lambda i: (i, 0)),
                  pl.BlockSpec(w.shape, lambda i: (0, 0))],
        out_specs=pl.BlockSpec((tm, d), lambda i: (i, 0)),
        compiler_params=_cparams(("parallel",)),
        name="outproj",
    )(h, o, w)


def _outproj_groups_kernel(h_ref, o0_ref, o1_ref, o2_ref, l0_ref, l1_ref, l2_ref, w_ref, out_ref):
    l0, l1, l2 = l0_ref[...], l1_ref[...], l2_ref[...]
    mx = jnp.maximum(jnp.maximum(l0, l1), l2)
    e0, e1, e2 = jnp.exp2(l0 - mx), jnp.exp2(l1 - mx), jnp.exp2(l2 - mx)
    tot = e0 + e1 + e2
    acc = h_ref[...]
    gw = o0_ref.shape[1]
    for g, (o_ref, e) in enumerate(((o0_ref, e0), (o1_ref, e1), (o2_ref, e2))):
        og = (o_ref[...].astype(F32) * (e / tot)).astype(BF16)
        acc = acc + jnp.dot(og, w_ref[g * gw:(g + 1) * gw, :], preferred_element_type=F32)
    out_ref[...] = acc


def _outproj_groups(h, os_, ls_, w, tm=512):
    s, d = h.shape
    tm = min(tm, s)
    w = w.astype(BF16)
    gw = os_[0].shape[1]
    row = lambda width: pl.BlockSpec((tm, width), lambda i: (i, 0))
    return pl.pallas_call(
        _outproj_groups_kernel,
        out_shape=jax.ShapeDtypeStruct((s, d), F32),
        grid=(s // tm,),
        in_specs=[row(d)] + [row(gw)] * 6 + [pl.BlockSpec(w.shape, lambda i: (0, 0))],
        out_specs=row(d),
        compiler_params=_cparams(("parallel",)),
        name="outproj_groups",
    )(h, *os_, *ls_, w)


def _route(logits):
    lane = lax.broadcasted_iota(jnp.int32, logits.shape, 1).astype(F32)
    big = 1e6
    gl = jnp.where(lane < MOE_GROUPS, logits, NEG_INF)
    gmax = jnp.max(gl, axis=1, keepdims=True)
    gidx = jnp.min(jnp.where(gl == gmax, lane, big), axis=1, keepdims=True)
    gw = 1.0 / jnp.sum(jnp.exp(gl - gmax), axis=1, keepdims=True)
    lo = MOE_GROUPS + gidx * MOE_PER_GROUP
    el = jnp.where((lane >= lo) & (lane < lo + MOE_PER_GROUP), logits, NEG_INF)
    v1 = jnp.max(el, axis=1, keepdims=True)
    i1 = jnp.min(jnp.where(el == v1, lane, big), axis=1, keepdims=True)
    el2 = jnp.where(lane == i1, NEG_INF, el)
    v2 = jnp.max(el2, axis=1, keepdims=True)
    i2 = jnp.min(jnp.where(el2 == v2, lane, big), axis=1, keepdims=True)
    e2 = jnp.exp(v2 - v1)
    w1 = gw / (1.0 + e2)
    w2 = w1 * e2
    return jnp.where(lane == i1, w1, 0.0) + jnp.where(lane == i2, w2, 0.0), gidx


MOE_ROW_TILE = 1024
INFO_GROUP_LANE = 0
INFO_RANK_LANE = 1
SUBLANES = 8


def _to_slabs(ref, x):
    for c in range(SUBLANES):
        ref[:, c, :] = x[:, c * LANES:(c + 1) * LANES]


def _from_slabs(ref):
    return jnp.concatenate([ref[:, c, :] for c in range(SUBLANES)], axis=1)


def _moe_route_kernel(h_ref, g_ref, wr_ref, br_ref, tri_ref, x3_ref, info_ref, cnt_ref, run_scr):
    @pl.when(pl.program_id(0) == 0)
    def _init():
        run_scr[...] = jnp.zeros_like(run_scr)

    xn = _rms(h_ref[...], g_ref[...])
    logits = jnp.dot(xn, wr_ref[...], preferred_element_type=F32, precision=lax.Precision.HIGHEST) + br_ref[...]
    comb, gidx = _route(logits)
    lane = lax.broadcasted_iota(jnp.int32, comb.shape, 1).astype(F32)
    onehot = lane == gidx
    before = jnp.dot(tri_ref[...], onehot.astype(BF16), preferred_element_type=F32)
    run = run_scr[...]
    rank = jnp.sum(jnp.where(onehot, before + run, 0.0), axis=1, keepdims=True)
    run = run + jnp.sum(onehot.astype(F32), axis=0, keepdims=True)
    run_scr[...] = run
    cnt_ref[...] = run
    _to_slabs(x3_ref, xn)
    info_ref[...] = jnp.where(lane == INFO_GROUP_LANE, gidx, jnp.where(lane == INFO_RANK_LANE, rank, comb))


def _moe_dispatch_kernel(pos_ref, x3_ref, info_ref, xs_in_ref, infos_in_ref, xs_ref, infos_ref, sem):
    del xs_in_ref, infos_in_ref
    tm = x3_ref.shape[0]

    def issue(t, c):
        p = pos_ref[0, t]
        pltpu.make_async_copy(x3_ref.at[pl.ds(t, 1)], xs_ref.at[pl.ds(p, 1)], sem).start()
        pltpu.make_async_copy(info_ref.at[pl.ds(t, 1)], infos_ref.at[pl.ds(p, 1)], sem).start()
        return c

    lax.fori_loop(0, tm, issue, 0, unroll=8)
    pltpu.make_async_copy(x3_ref, xs_ref.at[pl.ds(0, tm)], sem).wait()
    pltpu.make_async_copy(info_ref, infos_ref.at[pl.ds(0, tm)], sem).wait()


def _moe_expert_kernel(tg_ref, nv_ref, xs_ref, infos_ref, wg_ref, wu_ref, wd_ref, y_ref, x_scr, acc_scr):
    j, e = pl.program_id(0), pl.program_id(1)
    last = pl.num_programs(1) - 1

    @pl.when(j < nv_ref[0])
    def _live():
        @pl.when(e == 0)
        def _first():
            acc_scr[...] = jnp.zeros_like(acc_scr)
            x_scr[...] = _from_slabs(xs_ref).astype(BF16)

        x = x_scr[...]
        info = infos_ref[...]
        lane = lax.broadcasted_iota(jnp.int32, info.shape, 1)
        eid = tg_ref[j] * MOE_PER_GROUP + e + MOE_GROUPS
        we = jnp.sum(jnp.where(lane == eid, info, 0.0), axis=1, keepdims=True)
        gate = jnp.dot(x, wg_ref[0], preferred_element_type=F32)
        up = jnp.dot(x, wu_ref[0], preferred_element_type=F32)
        act = gate * jax.nn.sigmoid(gate) * up * we
        acc_scr[...] += jnp.dot(act.astype(BF16), wd_ref[0], preferred_element_type=F32)

        @pl.when(e == last)
        def _out():
            _to_slabs(y_ref, acc_scr[...])

    @pl.when((j >= nv_ref[0]) & (e == last))
    def _dead():
        y_ref[...] = jnp.zeros_like(y_ref)


def _moe_sparse(h, g, w_group, b_group, w_expert, b_expert, w_gate, w_up, w_down, tm=1024, tmd=512):
    s, d = h.shape
    assert d == SUBLANES * LANES
    tm, tmd, tb = min(tm, s), min(tmd, s), min(MOE_ROW_TILE, s)
    ne, _, ff = w_gate.shape
    nr = MOE_GROUPS + MOE_EXPERTS
    wr = jnp.pad(jnp.concatenate([w_group, w_expert], axis=1), ((0, 0), (0, LANES - nr)))
    br = jnp.pad(jnp.concatenate([b_group, b_expert]), (0, LANES - nr)).reshape(1, LANES)
    tri = (lax.broadcasted_iota(jnp.int32, (tm, tm), 0) > lax.broadcasted_iota(jnp.int32, (tm, tm), 1)).astype(BF16)
    const = lambda a: pl.BlockSpec(a.shape, lambda i: (0, 0))
    slab = lambda rows: pl.BlockSpec((rows, SUBLANES, LANES), lambda i: (i, 0, 0))
    x3, info, cnt = pl.pallas_call(
        _moe_route_kernel,
        out_shape=(jax.ShapeDtypeStruct((s, SUBLANES, LANES), F32), jax.ShapeDtypeStruct((s, LANES), F32),
                   jax.ShapeDtypeStruct((1, LANES), F32)),
        grid=(s // tm,),
        in_specs=[pl.BlockSpec((tm, d), lambda i: (i, 0)), const(g), const(wr), const(br), const(tri)],
        out_specs=(slab(tm), pl.BlockSpec((tm, LANES), lambda i: (i, 0)), pl.BlockSpec((1, LANES), lambda i: (0, 0))),
        scratch_shapes=[pltpu.VMEM((1, LANES), F32)],
        compiler_params=_cparams(("arbitrary",)),
        name="moe_route",
    )(h, g, wr, br, tri)

    counts = cnt[0, :MOE_GROUPS].astype(jnp.int32)
    padded = ((counts + tb - 1) // tb) * tb
    ends = jnp.cumsum(padded)
    starts = ends - padded
    tok_group = info[:, INFO_GROUP_LANE].astype(jnp.int32)
    tok_rank = info[:, INFO_RANK_LANE].astype(jnp.int32)
    pos = starts[tok_group] + tok_rank
    n_tiles = s // tb + MOE_GROUPS
    p_rows = n_tiles * tb
    tile_group = jnp.minimum(jnp.sum((jnp.arange(n_tiles) * tb)[:, None] >= ends[None, :], axis=1), MOE_GROUPS - 1)
    n_valid = (ends[-1] // tb).reshape(1)

    anyspec = pl.BlockSpec(memory_space=pl.ANY)
    xs, infos = pl.pallas_call(
        _moe_dispatch_kernel,
        out_shape=(jax.ShapeDtypeStruct((p_rows, SUBLANES, LANES), F32), jax.ShapeDtypeStruct((p_rows, LANES), F32)),
        grid=(s // tmd,),
        in_specs=[pl.BlockSpec((None, 1, tmd), lambda i: (i, 0, 0), memory_space=pltpu.SMEM),
                  slab(tmd), pl.BlockSpec((tmd, LANES), lambda i: (i, 0)), anyspec, anyspec],
        out_specs=(anyspec, anyspec),
        scratch_shapes=[pltpu.SemaphoreType.DMA],
        input_output_aliases={3: 0, 4: 1},
        compiler_params=_cparams(("arbitrary",)),
        name="moe_dispatch",
    )(pos.reshape(s // tmd, 1, tmd), x3, info, jnp.zeros((p_rows, SUBLANES, LANES), F32),
      jnp.zeros((p_rows, LANES), F32))

    wg, wu, wd = w_gate.astype(BF16), w_up.astype(BF16), w_down.astype(BF16)
    wsel = lambda j, e, tg, nv: (jnp.where(j < nv[0], tg[j] * MOE_PER_GROUP + e, ne - 1), 0, 0)
    y = pl.pallas_call(
        _moe_expert_kernel,
        out_shape=jax.ShapeDtypeStruct((p_rows, SUBLANES, LANES), F32),
        grid_spec=pltpu.PrefetchScalarGridSpec(
            num_scalar_prefetch=2,
            grid=(n_tiles, MOE_PER_GROUP),
            in_specs=[pl.BlockSpec((tb, SUBLANES, LANES), lambda j, e, tg, nv: (j, 0, 0)),
                      pl.BlockSpec((tb, LANES), lambda j, e, tg, nv: (j, 0)),
                      pl.BlockSpec((1, d, ff), wsel), pl.BlockSpec((1, d, ff), wsel), pl.BlockSpec((1, ff, d), wsel)],
            out_specs=pl.BlockSpec((tb, SUBLANES, LANES), lambda j, e, tg, nv: (j, 0, 0)),
            scratch_shapes=[pltpu.VMEM((tb, d), BF16), pltpu.VMEM((tb, d), F32)]),
        compiler_params=_cparams(("arbitrary", "arbitrary")),
        name="moe_expert",
    )(tile_group.astype(jnp.int32), n_valid.astype(jnp.int32), xs, infos, wg, wu, wd)
    return y, pos


def _ple_kernel(pos_ref, h_ref, y_hbm, g_ref, wg_ref, p_ref, wp_ref, gf_ref, out_ref, ybuf, sem, *, final):
    tm = h_ref.shape[0]

    def issue(t, c):
        pltpu.make_async_copy(y_hbm.at[pl.ds(pos_ref[0, t], 1)], ybuf.at[pl.ds(t, 1)], sem).start()
        return c

    lax.fori_loop(0, tm, issue, 0, unroll=8)
    pltpu.make_async_copy(y_hbm.at[pl.ds(0, tm)], ybuf, sem).wait()
    x = h_ref[...] + _from_slabs(ybuf)
    xn = _rms(x, g_ref[...]).astype(BF16)
    gate = jax.nn.sigmoid(jnp.dot(xn, wg_ref[...], preferred_element_type=F32))
    proj = jnp.dot(p_ref[...].astype(BF16), wp_ref[...], preferred_element_type=F32)
    y = x + gate * proj
    if final:
        y = _rms(y, gf_ref[...])
    out_ref[...] = y


def _ple(h, y_sorted, pos, g, wg, p, wp, gf, final, tm=512):
    s, d = h.shape
    tm = min(tm, s)
    wg, wp = wg.astype(BF16), wp.astype(BF16)
    const = lambda a: pl.BlockSpec(a.shape, lambda i: (0, 0))
    return pl.pallas_call(
        functools.partial(_ple_kernel, final=final),
        out_shape=jax.ShapeDtypeStruct((s, d), F32),
        grid=(s // tm,),
        in_specs=[pl.BlockSpec((None, 1, tm), lambda i: (i, 0, 0), memory_space=pltpu.SMEM),
                  pl.BlockSpec((tm, d), lambda i: (i, 0)), pl.BlockSpec(memory_space=pl.ANY), const(g), const(wg),
                  pl.BlockSpec((tm, p.shape[1]), lambda i: (i, 0)), const(wp), const(gf)],
        out_specs=pl.BlockSpec((tm, d), lambda i: (i, 0)),
        scratch_shapes=[pltpu.VMEM((tm, SUBLANES, LANES), F32), pltpu.SemaphoreType.DMA],
        compiler_params=_cparams(("arbitrary",)),
        name="ple",
    )(pos.reshape(s // tm, 1, tm), h, y_sorted, g, wg, p, wp, gf)


PARTIAL_KPERM = ((8, 16), (0, 8), (16, 64))
AXIAL_KPERM = ((16, 32), (0, 16), (48, 64), (32, 48))
B_PAIRS = ((128, 1), (512, 4), (2048, 16))


def _partial_tables(s):
    inv = ROPE_THETA ** (-jnp.arange(0, ROT_DIM, 2, dtype=F32) / ROT_DIM)
    ang = jnp.arange(s).astype(F32)[:, None] * inv[None, :]
    return _rope_tables([ang], s, LOG2E * HEAD_DIM ** -0.5)


def _axial_tables(s):
    half = HEAD_DIM // 2
    inv = AXIAL_THETA ** (-jnp.arange(0, half, 2, dtype=F32) / half)
    t = jnp.arange(s)
    ang_r = (t // GRID_W).astype(F32)[:, None] * inv[None, :]
    ang_c = (t % GRID_W).astype(F32)[:, None] * inv[None, :]
    return _rope_tables([ang_r, ang_c], s, LOG2E * HEAD_DIM ** -0.5)


def _mixer_diff(h, g, w_in, w_out, lam_rows, subln, lam_init, tabs):
    d = h.shape[1]
    heads = d // (2 * HEAD_DIM)
    aw = 2 * heads * HEAD_DIM
    dv = 2 * HEAD_DIM
    q, kt, v = _project(h, g, w_in[:, :aw], w_in[:, aw:2 * aw], w_in[:, 2 * aw:], tabs["row_scaled"], tabs["t"],
                        hr=2 * heads, ht=2 * heads, hv=heads, dv=dv, nv=2 * dv, v_t=False,
                        shift=ROT_DIM // 2, tperm=PARTIAL_KPERM, qk_norm=False)
    o = _flash(q, kt, v, units=heads, q_per_unit=2, k_per_unit=2, v_per_unit=1,
               streams=((0, 0, 0), (1, 1, 0)), split=4, tq=1024, tk=2048, nv=2 * dv, dv=dv, out_w=dv,
               mode="diff", lam=lam_rows, subln=subln, lam_init=lam_init, name="flash_diff")
    return _outproj(h, o, w_out)


def _mixer_dilated(h, g, w_in, w_out, tabs):
    nh, hg = 12, 4
    bw = nh * HEAD_DIM
    q, kt, v = _project(h, g, w_in[:, :bw], w_in[:, bw:2 * bw], w_in[:, 2 * bw:], tabs["row_scaled"], tabs["t"],
                        hr=nh, ht=nh, hv=nh, dv=HEAD_DIM, nv=LANES, v_t=False,
                        shift=ROT_DIM // 2, tperm=PARTIAL_KPERM, qk_norm=False)
    os_, ls_ = [], []
    for gi, (win, dil) in enumerate(B_PAIRS):
        halfw = (win // (2 * dil)) * dil
        o, lse = _flash(q, kt, v, units=1, unit0=gi, q_per_unit=hg, k_per_unit=hg, v_per_unit=hg,
                        streams=tuple((j, j, j) for j in range(hg)), tq=1024, tk=max(halfw, 256),
                        nv=LANES, dv=HEAD_DIM, out_w=hg * HEAD_DIM, band=(halfw, dil), mode="lse",
                        name=f"flash_dilated{gi}")
        os_.append(o)
        ls_.append(lse)
    return _outproj_groups(h, os_, ls_, w_out)


GQA_HEADS, GQA_KV_HEADS = 16, 4


def _mixer_window(h, g, w_in, w_out, sink, tabs):
    qd, kvd = GQA_HEADS * HEAD_DIM, GQA_KV_HEADS * HEAD_DIM
    grp = GQA_HEADS // GQA_KV_HEADS
    q, kt, v = _project(h, g, w_in[:, :qd], w_in[:, qd:qd + kvd], w_in[:, qd + kvd:], tabs["row_scaled"], tabs["t"],
                        hr=GQA_HEADS, ht=GQA_KV_HEADS, hv=GQA_KV_HEADS, dv=HEAD_DIM, nv=LANES, v_t=False,
                        shift=ROT_DIM // 2, tperm=PARTIAL_KPERM, qk_norm=False)
    o = _flash(q, kt, v, units=GQA_KV_HEADS, q_per_unit=grp, k_per_unit=1, v_per_unit=1,
               streams=tuple((j, 0, 0) for j in range(grp)), tq=256, tk=256, nv=LANES, dv=HEAD_DIM,
               out_w=grp * HEAD_DIM, band=(128, 1), mode="gqa", sink=sink.reshape(GQA_KV_HEADS, grp),
               name="flash_window")
    return _outproj(h, o, w_out)


def _mixer_axial(h, g, w_in, w_out, q_norm, k_norm, tabs):
    qd, kvd = GQA_HEADS * HEAD_DIM, GQA_KV_HEADS * HEAD_DIM
    grp = GQA_HEADS // GQA_KV_HEADS
    qg = jnp.pad(q_norm, (0, LANES - HEAD_DIM)).reshape(1, LANES)
    kg = k_norm.reshape(HEAD_DIM, 1)
    q, kt, v = _project(h, g, w_in[:, :qd], w_in[:, qd:qd + kvd], w_in[:, qd + kvd:], tabs["row_scaled"], tabs["t"],
                        hr=GQA_HEADS, ht=GQA_KV_HEADS, hv=GQA_KV_HEADS, dv=HEAD_DIM, nv=LANES, v_t=False,
                        shift=HEAD_DIM // 4, tperm=AXIAL_KPERM, qk_norm=True, rg=qg, tg=kg)
    o = _flash(q, kt, v, units=GQA_KV_HEADS, q_per_unit=grp, k_per_unit=1, v_per_unit=1,
               streams=tuple((j, 0, 0) for j in range(grp)), split=4, tq=1024, tk=2048, nv=LANES, dv=HEAD_DIM,
               out_w=grp * HEAD_DIM, mode="gqa", name="flash_axial")
    return _outproj(h, o, w_out)


def kernel(x, p, norm_mix, norm_ffn, norm_ple, norm_final, a_w_in, a_w_out, a_lam_q1, a_lam_k1, a_lam_q2, a_lam_k2, a_subln, b_w_in, b_w_out, c_w_in, c_w_out, c_sink, d_w_in, d_w_out, d_q_norm, d_k_norm, moe_w_group, moe_b_group, moe_w_expert, moe_b_expert, moe_w_gate, moe_w_up, moe_w_down, ple_w_gate, ple_w_proj):
    bn, s, d = x.shape
    assert bn == 1
    depth = p.shape[0]
    h = x[0]
    ptabs = _partial_tables(s)
    atabs = _axial_tables(s)
    row = lambda a: a.reshape(1, -1)
    for i in range(depth):
        r, kind = divmod(i, 4)
        g = row(norm_mix[i])
        if kind == 0:
            lam_init = 0.8 - 0.6 * math.exp(-0.3 * i)
            lam_rows = jnp.stack([a_lam_q1[r], a_lam_k1[r], a_lam_q2[r], a_lam_k2[r]])
            h = _mixer_diff(h, g, a_w_in[r], a_w_out[r], lam_rows, row(a_subln[r]), lam_init, ptabs)
        elif kind == 1:
            h = _mixer_dilated(h, g, b_w_in[r], b_w_out[r], ptabs)
        elif kind == 2:
            h = _mixer_window(h, g, c_w_in[r], c_w_out[r], c_sink[r], ptabs)
        else:
            h = _mixer_axial(h, g, d_w_in[r], d_w_out[r], d_q_norm[r], d_k_norm[r], atabs)
        y_sorted, pos = _moe_sparse(h, row(norm_ffn[i]), moe_w_group[i], moe_b_group[i], moe_w_expert[i],
                                    moe_b_expert[i], moe_w_gate[i], moe_w_up[i], moe_w_down[i])
        h = _ple(h, y_sorted, pos, row(norm_ple[i]), ple_w_gate[i], p[i, 0], ple_w_proj[i], row(norm_final), final=(i == depth - 1))
    return h[None]
```

```python
import functools
import math

import jax
import jax.numpy as jnp
from jax import lax
from jax.experimental import pallas as pl
from jax.experimental.pallas import tpu as pltpu

F32 = jnp.float32
BF16 = jnp.bfloat16

HEAD_DIM = 64
LANES = 128
EPS = 1e-6
LOG2E = 1.4426950408889634
NEG_INF = -1e30
M_FLOOR = -1e29
ROPE_THETA = 500000.0
ROT_DIM = HEAD_DIM // 4
AXIAL_THETA = 10000.0
GRID_W = 64
MOE_GROUPS = 4
MOE_PER_GROUP = 4
MOE_EXPERTS = 16
VMEM_LIMIT = 56 * 1024 * 1024


def _cparams(sem):
    return pltpu.CompilerParams(dimension_semantics=sem, vmem_limit_bytes=VMEM_LIMIT)


def _rms(x, g):
    return x * lax.rsqrt(jnp.mean(x * x, axis=-1, keepdims=True) + EPS) * g


def _proj_kernel(h_ref, g_ref, wr_ref, wtt_ref, wv_ref, rc_ref, rs1_ref, rs2_ref, tc_ref, ts_ref,
                 rg_ref, tg_ref, r_ref, t_ref, v_ref, *, hr, ht, hv, nv, dv, v_t, shift, tperm, qk_norm):
    xn = _rms(h_ref[...], g_ref[...]).astype(BF16)
    rf = jnp.dot(xn, wr_ref[...], preferred_element_type=F32)
    rc, rs1, rs2 = rc_ref[...], rs1_ref[...], rs2_ref[...]
    for h in range(hr):
        s = rf[:, h * LANES:(h + 1) * LANES]
        if qk_norm:
            ms = jnp.sum(s * s, axis=-1, keepdims=True) * (1.0 / HEAD_DIM)
            s = s * lax.rsqrt(ms + EPS) * rg_ref[...]
        r = s * rc + pltpu.roll(s, LANES - shift, 1) * rs1 + pltpu.roll(s, shift, 1) * rs2
        r_ref[h] = r[:, :HEAD_DIM].astype(BF16)
    nt = (((1,), (1,)), ((), ()))
    tf = lax.dot_general(wtt_ref[...], xn, nt, preferred_element_type=F32)
    tc, ts = tc_ref[...], ts_ref[...]
    for h in range(ht):
        s = tf[h * HEAD_DIM:(h + 1) * HEAD_DIM, :]
        if qk_norm:
            ms = jnp.sum(s * s, axis=0, keepdims=True) * (1.0 / HEAD_DIM)
            s = s * lax.rsqrt(ms + EPS) * tg_ref[...]
        partner = jnp.concatenate([s[a:b] for a, b in tperm], axis=0)
        t_ref[h] = (s * tc + partner * ts).astype(BF16)
    if v_t:
        vf = lax.dot_general(wv_ref[...], xn, nt, preferred_element_type=F32)
        row = lax.broadcasted_iota(jnp.int32, (nv, vf.shape[1]), 0)
        for h in range(hv):
            v_ref[h] = jnp.where(row == dv, 1.0, vf[h * nv:(h + 1) * nv, :]).astype(BF16)
    else:
        vf = jnp.dot(xn, wv_ref[...], preferred_element_type=F32)
        lane = lax.broadcasted_iota(jnp.int32, vf.shape, 1)
        v_ref[...] = jnp.where((lane & (nv - 1)) == dv, 1.0, vf).astype(BF16)


def _project(h, g, w_row, w_t, wv, row_tabs, t_tabs, *, hr, ht, hv, dv, nv, v_t, shift, tperm, qk_norm,
             rg=None, tg=None, tm=512):
    s, d = h.shape
    tm = min(tm, s)
    wr_p = jnp.pad(w_row.reshape(d, hr, HEAD_DIM), ((0, 0), (0, 0), (0, LANES - HEAD_DIM)))
    wr_p = wr_p.reshape(d, hr * LANES).astype(BF16)
    wtt = w_t.T.astype(BF16)
    wv_p = jnp.pad(wv.reshape(d, hv, dv), ((0, 0), (0, 0), (0, nv - dv))).reshape(d, hv * nv).astype(BF16)
    if v_t:
        wv_p = wv_p.T
        v_shape, v_spec = (hv, nv, s), pl.BlockSpec((hv, nv, tm), lambda i: (0, 0, i))
    else:
        v_shape, v_spec = (s, hv * nv), pl.BlockSpec((tm, hv * nv), lambda i: (i, 0))
    rc, rs1, rs2 = row_tabs
    tc, ts = t_tabs
    if rg is None:
        rg = jnp.ones((1, LANES), F32)
        tg = jnp.ones((HEAD_DIM, 1), F32)
    full = lambda a: pl.BlockSpec(a.shape, lambda i: (0,) * a.ndim)
    kern = functools.partial(_proj_kernel, hr=hr, ht=ht, hv=hv, nv=nv, dv=dv, v_t=v_t, shift=shift, tperm=tperm,
                             qk_norm=qk_norm)
    return pl.pallas_call(
        kern,
        out_shape=(jax.ShapeDtypeStruct((hr, s, HEAD_DIM), BF16),
                   jax.ShapeDtypeStruct((ht, HEAD_DIM, s), BF16),
                   jax.ShapeDtypeStruct(v_shape, BF16)),
        grid=(s // tm,),
        in_specs=[pl.BlockSpec((tm, d), lambda i: (i, 0)), full(g), full(wr_p), full(wtt), full(wv_p),
                  pl.BlockSpec((tm, LANES), lambda i: (i, 0)), pl.BlockSpec((tm, LANES), lambda i: (i, 0)),
                  pl.BlockSpec((tm, LANES), lambda i: (i, 0)),
                  pl.BlockSpec((HEAD_DIM, tm), lambda i: (0, i)), pl.BlockSpec((HEAD_DIM, tm), lambda i: (0, i)),
                  full(rg), full(tg)],
        out_specs=(pl.BlockSpec((hr, tm, HEAD_DIM), lambda i: (0, i, 0)),
                   pl.BlockSpec((ht, HEAD_DIM, tm), lambda i: (0, 0, i)),
                   v_spec),
        compiler_params=_cparams(("parallel",)),
        name="proj",
    )(h, g, wr_p, wtt, wv_p, rc, rs1, rs2, tc, ts, rg, tg)


def _rope_tables(ang_list, s, scale):
    cs, sn1, sn2, ksn = [], [], [], []
    used = 0
    for ang in ang_list:
        c, sn = jnp.cos(ang), jnp.sin(ang)
        z = jnp.zeros_like(sn)
        cs += [c, c]
        sn1 += [-sn, z]
        sn2 += [z, sn]
        ksn += [-sn, sn]
        used += 2 * ang.shape[1]
    rest = HEAD_DIM - used
    ones, zeros = jnp.ones((s, rest), F32), jnp.zeros((s, rest), F32)
    c64 = jnp.concatenate(cs + [ones], axis=1)
    pad = jnp.zeros((s, LANES - HEAD_DIM), F32)
    row = (jnp.concatenate([c64, pad], axis=1), jnp.concatenate(sn1 + [zeros, pad], axis=1),
           jnp.concatenate(sn2 + [zeros, pad], axis=1))
    tr = (c64.T, jnp.concatenate(ksn + [zeros], axis=1).T)
    scaled = lambda tabs: tuple(t * scale for t in tabs)
    return {"row": row, "t": tr, "row_scaled": scaled(row), "t_scaled": scaled(tr)}


def _lane_tile(x, width):
    reps = width // LANES
    return x if reps == 1 else jnp.concatenate([x] * reps, axis=1)


def _flash_kernel(*refs, streams, split, tq, tk, nv, dv, band, halo, seq, mode, lam_init, has_sink):
    it = iter(refs)
    q_ref, kt_ref, v_ref = next(it), next(it), next(it)
    sink_ref = next(it) if has_sink else None
    lam_ref = subln_ref = None
    if mode == "diff":
        lam_ref, subln_ref = next(it), next(it)
    o_ref = next(it)
    lse_ref = next(it) if mode == "lse" else None
    m_scr, acc_scr = next(it), next(it)

    qb, kb = pl.program_id(1), pl.program_id(2)
    nsteps = pl.num_programs(2)
    cr = tq // split

    @pl.when(kb == 0)
    def _init():
        for si, (qi, _, _) in enumerate(streams):
            if has_sink:
                sink2 = jnp.full((tq, LANES), LOG2E, F32) * sink_ref[pl.program_id(0), qi]
                m0 = jnp.maximum(sink2, M_FLOOR)
                m_scr[si] = m0
                lane = lax.broadcasted_iota(jnp.int32, (tq, nv), 1)
                acc_scr[si] = jnp.where(lane == dv, _lane_tile(jnp.exp2(sink2 - m0), nv), 0.0)
            else:
                m_scr[si] = jnp.full((tq, LANES), M_FLOOR, F32)
                acc_scr[si] = jnp.zeros((tq, nv), F32)

    def _step():
        chunks = [(si, c * cr, qi, ki, vi) for si, (qi, ki, vi) in enumerate(streams) for c in range(split)]
        scores = [jnp.dot(q_ref[qi, r0:r0 + cr, :], kt_ref[ki], preferred_element_type=F32)
                  for _, r0, qi, ki, _ in chunks]
        for sc, (si, r0, _, _, vi) in zip(scores, chunks):
            if band is not None:
                halfw, dil, off = band
                row = lax.broadcasted_iota(jnp.int32, (cr, tk), 0) + r0
                col = lax.broadcasted_iota(jnp.int32, (cr, tk), 1)
                diff = col - row + kshift
                valid = jnp.abs(diff) <= halfw
                if dil > 1:
                    valid = valid & ((diff & (dil - 1)) == 0)
                sc = jnp.where(valid, sc, NEG_INF)
            m_old = m_scr[si, r0:r0 + cr, :]
            m_new = jnp.maximum(m_old, jnp.broadcast_to(jnp.max(sc, axis=1, keepdims=True), (cr, LANES)))
            p = jnp.exp2(sc - _lane_tile(m_new, tk))
            alpha = jnp.exp2(m_old - m_new)
            pv = jnp.dot(p.astype(BF16), v_ref[:, vi * nv:(vi + 1) * nv], preferred_element_type=F32)
            acc_scr[si, r0:r0 + cr, :] = acc_scr[si, r0:r0 + cr, :] * _lane_tile(alpha, nv) + pv
            m_scr[si, r0:r0 + cr, :] = m_new

    if band is None:
        _step()
    elif halo is not None:
        kshift = jnp.clip(qb * tq - halo, 0, seq - tk) - qb * tq
        _step()
    else:
        _, _, off = band
        kshift = (kb - off) * tk
        kabs = qb * (tq // tk) + kb - off
        pl.when((kabs >= 0) & (kabs < seq // tk))(_step)

    @pl.when(kb == nsteps - 1)
    def _fin():
        if mode == "diff":
            a0, a1 = acc_scr[0], acc_scr[1]
            lam_rows = lam_ref[...]
            lam = (jnp.exp(jnp.sum(lam_rows[0:1] * lam_rows[1:2], axis=1, keepdims=True))
                   - jnp.exp(jnp.sum(lam_rows[2:3] * lam_rows[3:4], axis=1, keepdims=True)) + lam_init)
            o = a0[:, :dv] / a0[:, dv:dv + 1] - lam * (a1[:, :dv] / a1[:, dv:dv + 1])
            o = _rms(o, subln_ref[...]) * (1.0 - lam_init)
            o_ref[...] = o.astype(o_ref.dtype)
        else:
            for si in range(len(streams)):
                a = acc_scr[si]
                l = a[:, dv:dv + 1]
                o_ref[:, si * dv:(si + 1) * dv] = (a[:, :dv] / l).astype(o_ref.dtype)
                if mode == "lse":
                    lse_ref[:, si * dv:(si + 1) * dv] = m_scr[si][:, :dv] + jnp.log2(l)


def _flash(q, kt, v, *, units, q_per_unit, k_per_unit, v_per_unit, streams, tq, tk, nv, dv, out_w,
           split=1, unit0=0, band=None, halo=None, mode="gqa", sink=None, lam=None, subln=None, lam_init=0.0,
           name="flash"):
    s = q.shape[1]
    tq, tk = min(tq, s), min(tk, s)
    if halo is not None and tq + 2 * halo > s:
        halo, tq = None, max(tq, tk)
    if halo is not None:
        tk = tq + 2 * halo
    assert s % tq == 0 and (halo is not None or (s % tk == 0 and (band is None or tq % tk == 0)))
    n_kblocks = s // tk
    kspec = vspec = None
    if band is None:
        nsteps = n_kblocks
        kmap = lambda u, i, j: (u + unit0, 0, j)
        vmap = lambda u, i, j: (j, u + unit0)
    elif halo is not None:
        assert tq % LANES == 0 and halo % LANES == 0
        nsteps = 1
        band = (band[0], band[1], 0)
        koff = lambda i: jnp.clip(i * (tq // LANES) - halo // LANES, 0, (s - tk) // LANES) * LANES
        kspec = pl.BlockSpec((pl.Element(k_per_unit), pl.Element(HEAD_DIM), pl.Element(tk)),
                             lambda u, i, j: ((u + unit0) * k_per_unit, 0, koff(i)))
        vspec = pl.BlockSpec((pl.Element(tk), pl.Element(v_per_unit * nv)),
                             lambda u, i, j: (koff(i), (u + unit0) * v_per_unit * nv))
    else:
        halfw = band[0]
        off = -(-halfw // tk)
        nsteps = tq // tk + 2 * off
        band = (band[0], band[1], off)
        kidx = lambda i, j: jnp.clip(i * (tq // tk) + j - off, 0, n_kblocks - 1)
        kmap = lambda u, i, j: (u + unit0, 0, kidx(i, j))
        vmap = lambda u, i, j: (kidx(i, j), u + unit0)
    in_specs = [pl.BlockSpec((q_per_unit, tq, HEAD_DIM), lambda u, i, j: (u + unit0, i, 0)),
                kspec or pl.BlockSpec((k_per_unit, HEAD_DIM, tk), kmap),
                vspec or pl.BlockSpec((tk, v_per_unit * nv), vmap)]
    args = [q, kt, v]
    if sink is not None:
        in_specs.append(pl.BlockSpec(memory_space=pltpu.SMEM))
        args.append(sink)
    if mode == "diff":
        in_specs += [pl.BlockSpec(lam.shape, lambda u, i, j: (0, 0)), pl.BlockSpec(subln.shape, lambda u, i, j: (0, 0))]
        args += [lam, subln]
    out_shape = [jax.ShapeDtypeStruct((s, units * out_w), BF16)]
    out_specs = [pl.BlockSpec((tq, out_w), lambda u, i, j: (i, u))]
    if mode == "lse":
        out_shape.append(jax.ShapeDtypeStruct((s, units * out_w), F32))
        out_specs.append(pl.BlockSpec((tq, out_w), lambda u, i, j: (i, u)))
    assert tq % split == 0
    kern = functools.partial(_flash_kernel, streams=streams, split=split, tq=tq, tk=tk, nv=nv, dv=dv, band=band,
                             halo=halo, seq=s, mode=mode, lam_init=lam_init, has_sink=sink is not None)
    res = pl.pallas_call(
        kern,
        out_shape=tuple(out_shape),
        grid=(units, s // tq, nsteps),
        in_specs=in_specs,
        out_specs=tuple(out_specs),
        scratch_shapes=[pltpu.VMEM((len(streams), tq, LANES), F32), pltpu.VMEM((len(streams), tq, nv), F32)],
        compiler_params=_cparams(("parallel", "parallel", "arbitrary")),
        name=name,
    )(*args)
    return res if mode == "lse" else res[0]


def _outproj_kernel(h_ref, o_ref, w_ref, out_ref):
    out_ref[...] = h_ref[...] + jnp.dot(o_ref[...], w_ref[...], preferred_element_type=F32)


def _outproj(h, o, w, tm=512):
    s, d = h.shape
    tm = min(tm, s)
    w = w.astype(BF16)
    return pl.pallas_call(
        _outproj_kernel,
        out_shape=jax.ShapeDtypeStruct((s, d), F32),
        grid=(s // tm,),
        in_specs=[pl.BlockSpec((tm, d), lambda i: (i, 0)), pl.BlockSpec((tm, o.shape[1]), lambda i: (i, 0)),
                  pl.BlockSpec(w.shape, lambda i: (0, 0))],
        out_specs=pl.BlockSpec((tm, d), lambda i: (i, 0)),
        compiler_params=_cparams(("parallel",)),
        name="outproj",
    )(h, o, w)


def _outproj_groups_kernel(h_ref, o0_ref, o1_ref, o2_ref, l0_ref, l1_ref, l2_ref, w_ref, out_ref):
    l0, l1, l2 = l0_ref[...], l1_ref[...], l2_ref[...]
    mx = jnp.maximum(jnp.maximum(l0, l1), l2)
    e0, e1, e2 = jnp.exp2(l0 - mx), jnp.exp2(l1 - mx), jnp.exp2(l2 - mx)
    tot = e0 + e1 + e2
    acc = h_ref[...]
    gw = o0_ref.shape[1]
    for g, (o_ref, e) in enumerate(((o0_ref, e0), (o1_ref, e1), (o2_ref, e2))):
        og = (o_ref[...].astype(F32) * (e / tot)).astype(BF16)
        acc = acc + jnp.dot(og, w_ref[g * gw:(g + 1) * gw, :], preferred_element_type=F32)
    out_ref[...] = acc


def _outproj_groups(h, os_, ls_, w, tm=512):
    s, d = h.shape
    tm = min(tm, s)
    w = w.astype(BF16)
    gw = os_[0].shape[1]
    row = lambda width: pl.BlockSpec((tm, width), lambda i: (i, 0))
    return pl.pallas_call(
        _outproj_groups_kernel,
        out_shape=jax.ShapeDtypeStruct((s, d), F32),
        grid=(s // tm,),
        in_specs=[row(d)] + [row(gw)] * 6 + [pl.BlockSpec(w.shape, lambda i: (0, 0))],
        out_specs=row(d),
        compiler_params=_cparams(("parallel",)),
        name="outproj_groups",
    )(h, *os_, *ls_, w)


def _route(logits):
    lane = lax.broadcasted_iota(jnp.int32, logits.shape, 1).astype(F32)
    big = 1e6
    gl = jnp.where(lane < MOE_GROUPS, logits, NEG_INF)
    gmax = jnp.max(gl, axis=1, keepdims=True)
    gidx = jnp.min(jnp.where(gl == gmax, lane, big), axis=1, keepdims=True)
    gw = 1.0 / jnp.sum(jnp.exp(gl - gmax), axis=1, keepdims=True)
    lo = MOE_GROUPS + gidx * MOE_PER_GROUP
    el = jnp.where((lane >= lo) & (lane < lo + MOE_PER_GROUP), logits, NEG_INF)
    v1 = jnp.max(el, axis=1, keepdims=True)
    i1 = jnp.min(jnp.where(el == v1, lane, big), axis=1, keepdims=True)
    el2 = jnp.where(lane == i1, NEG_INF, el)
    v2 = jnp.max(el2, axis=1, keepdims=True)
    i2 = jnp.min(jnp.where(el2 == v2, lane, big), axis=1, keepdims=True)
    e2 = jnp.exp(v2 - v1)
    w1 = gw / (1.0 + e2)
    w2 = w1 * e2
    return jnp.where(lane == i1, w1, 0.0) + jnp.where(lane == i2, w2, 0.0), gidx


MOE_ROW_TILE = 1024
INFO_GROUP_LANE = 0
INFO_RANK_LANE = 1
SUBLANES = 8


def _to_slabs(ref, x):
    for c in range(SUBLANES):
        ref[:, c, :] = x[:, c * LANES:(c + 1) * LANES]


def _from_slabs(ref):
    return jnp.concatenate([ref[:, c, :] for c in range(SUBLANES)], axis=1)


def _moe_route_kernel(h_ref, g_ref, wr_ref, br_ref, tri_ref, x3_ref, info_ref, cnt_ref, run_scr):
    @pl.when(pl.program_id(0) == 0)
    def _init():
        run_scr[...] = jnp.zeros_like(run_scr)

    xn = _rms(h_ref[...], g_ref[...])
    logits = jnp.dot(xn, wr_ref[...], preferred_element_type=F32, precision=lax.Precision.HIGHEST) + br_ref[...]
    comb, gidx = _route(logits)
    lane = lax.broadcasted_iota(jnp.int32, comb.shape, 1).astype(F32)
    onehot = lane == gidx
    before = jnp.dot(tri_ref[...], onehot.astype(BF16), preferred_element_type=F32)
    run = run_scr[...]
    rank = jnp.sum(jnp.where(onehot, before + run, 0.0), axis=1, keepdims=True)
    run = run + jnp.sum(onehot.astype(F32), axis=0, keepdims=True)
    run_scr[...] = run
    cnt_ref[...] = run
    _to_slabs(x3_ref, xn)
    info_ref[...] = jnp.where(lane == INFO_GROUP_LANE, gidx, jnp.where(lane == INFO_RANK_LANE, rank, comb))


def _moe_dispatch_kernel(pos_ref, x3_ref, info_ref, xs_in_ref, infos_in_ref, xs_ref, infos_ref, sem):
    del xs_in_ref, infos_in_ref
    tm = x3_ref.shape[0]

    def issue(t, c):
        p = pos_ref[0, t]
        pltpu.make_async_copy(x3_ref.at[pl.ds(t, 1)], xs_ref.at[pl.ds(p, 1)], sem).start()
        pltpu.make_async_copy(info_ref.at[pl.ds(t, 1)], infos_ref.at[pl.ds(p, 1)], sem).start()
        return c

    lax.fori_loop(0, tm, issue, 0, unroll=8)
    pltpu.make_async_copy(x3_ref, xs_ref.at[pl.ds(0, tm)], sem).wait()
    pltpu.make_async_copy(info_ref, infos_ref.at[pl.ds(0, tm)], sem).wait()


def _moe_expert_kernel(tg_ref, nv_ref, xs_ref, infos_ref, wg_ref, wu_ref, wd_ref, y_ref, x_scr, acc_scr):
    j, e = pl.program_id(0), pl.program_id(1)
    last = pl.num_programs(1) - 1

    @pl.when(j < nv_ref[0])
    def _live():
        @pl.when(e == 0)
        def _first():
            acc_scr[...] = jnp.zeros_like(acc_scr)
            x_scr[...] = _from_slabs(xs_ref).astype(BF16)

        x = x_scr[...]
        info = infos_ref[...]
        lane = lax.broadcasted_iota(jnp.int32, info.shape, 1)
        eid = tg_ref[j] * MOE_PER_GROUP + e + MOE_GROUPS
        we = jnp.sum(jnp.where(lane == eid, info, 0.0), axis=1, keepdims=True)
        gate = jnp.dot(x, wg_ref[0], preferred_element_type=F32)
        up = jnp.dot(x, wu_ref[0], preferred_element_type=F32)
        act = gate * jax.nn.sigmoid(gate) * up * we
        acc_scr[...] += jnp.dot(act.astype(BF16), wd_ref[0], preferred_element_type=F32)

        @pl.when(e == last)
        def _out():
            _to_slabs(y_ref, acc_scr[...])

    @pl.when((j >= nv_ref[0]) & (e == last))
    def _dead():
        y_ref[...] = jnp.zeros_like(y_ref)


def _moe_sparse(h, g, w_group, b_group, w_expert, b_expert, w_gate, w_up, w_down, tm=1024, tmd=512):
    s, d = h.shape
    assert d == SUBLANES * LANES
    tm, tmd, tb = min(tm, s), min(tmd, s), min(MOE_ROW_TILE, s)
    ne, _, ff = w_gate.shape
    nr = MOE_GROUPS + MOE_EXPERTS
    wr = jnp.pad(jnp.concatenate([w_group, w_expert], axis=1), ((0, 0), (0, LANES - nr)))
    br = jnp.pad(jnp.concatenate([b_group, b_expert]), (0, LANES - nr)).reshape(1, LANES)
    tri = (lax.broadcasted_iota(jnp.int32, (tm, tm), 0) > lax.broadcasted_iota(jnp.int32, (tm, tm), 1)).astype(BF16)
    const = lambda a: pl.BlockSpec(a.shape, lambda i: (0, 0))
    slab = lambda rows: pl.BlockSpec((rows, SUBLANES, LANES), lambda i: (i, 0, 0))
    x3, info, cnt = pl.pallas_call(
        _moe_route_kernel,
        out_shape=(jax.ShapeDtypeStruct((s, SUBLANES, LANES), F32), jax.ShapeDtypeStruct((s, LANES), F32),
                   jax.ShapeDtypeStruct((1, LANES), F32)),
        grid=(s // tm,),
        in_specs=[pl.BlockSpec((tm, d), lambda i: (i, 0)), const(g), const(wr), const(br), const(tri)],
        out_specs=(slab(tm), pl.BlockSpec((tm, LANES), lambda i: (i, 0)), pl.BlockSpec((1, LANES), lambda i: (0, 0))),
        scratch_shapes=[pltpu.VMEM((1, LANES), F32)],
        compiler_params=_cparams(("arbitrary",)),
        name="moe_route",
    )(h, g, wr, br, tri)

    counts = cnt[0, :MOE_GROUPS].astype(jnp.int32)
    padded = ((counts + tb - 1) // tb) * tb
    ends = jnp.cumsum(padded)
    starts = ends - padded
    tok_group = info[:, INFO_GROUP_LANE].astype(jnp.int32)
    tok_rank = info[:, INFO_RANK_LANE].astype(jnp.int32)
    pos = starts[tok_group] + tok_rank
    n_tiles = s // tb + MOE_GROUPS
    p_rows = n_tiles * tb
    tile_group = jnp.minimum(jnp.sum((jnp.arange(n_tiles) * tb)[:, None] >= ends[None, :], axis=1), MOE_GROUPS - 1)
    n_valid = (ends[-1] // tb).reshape(1)

    anyspec = pl.BlockSpec(memory_space=pl.ANY)
    xs, infos = pl.pallas_call(
        _moe_dispatch_kernel,
        out_shape=(jax.ShapeDtypeStruct((p_rows, SUBLANES, LANES), F32), jax.ShapeDtypeStruct((p_rows, LANES), F32)),
        grid=(s // tmd,),
        in_specs=[pl.BlockSpec((None, 1, tmd), lambda i: (i, 0, 0), memory_space=pltpu.SMEM),
                  slab(tmd), pl.BlockSpec((tmd, LANES), lambda i: (i, 0)), anyspec, anyspec],
        out_specs=(anyspec, anyspec),
        scratch_shapes=[pltpu.SemaphoreType.DMA],
        input_output_aliases={3: 0, 4: 1},
        compiler_params=_cparams(("arbitrary",)),
        name="moe_dispatch",
    )(pos.reshape(s // tmd, 1, tmd), x3, info, jnp.zeros((p_rows, SUBLANES, LANES), F32),
      jnp.zeros((p_rows, LANES), F32))

    wg, wu, wd = w_gate.astype(BF16), w_up.astype(BF16), w_down.astype(BF16)
    wsel = lambda j, e, tg, nv: (jnp.where(j < nv[0], tg[j] * MOE_PER_GROUP + e, ne - 1), 0, 0)
    y = pl.pallas_call(
        _moe_expert_kernel,
        out_shape=jax.ShapeDtypeStruct((p_rows, SUBLANES, LANES), F32),
        grid_spec=pltpu.PrefetchScalarGridSpec(
            num_scalar_prefetch=2,
            grid=(n_tiles, MOE_PER_GROUP),
            in_specs=[pl.BlockSpec((tb, SUBLANES, LANES), lambda j, e, tg, nv: (j, 0, 0)),
                      pl.BlockSpec((tb, LANES), lambda j, e, tg, nv: (j, 0)),
                      pl.BlockSpec((1, d, ff), wsel), pl.BlockSpec((1, d, ff), wsel), pl.BlockSpec((1, ff, d), wsel)],
            out_specs=pl.BlockSpec((tb, SUBLANES, LANES), lambda j, e, tg, nv: (j, 0, 0)),
            scratch_shapes=[pltpu.VMEM((tb, d), BF16), pltpu.VMEM((tb, d), F32)]),
        compiler_params=_cparams(("arbitrary", "arbitrary")),
        name="moe_expert",
    )(tile_group.astype(jnp.int32), n_valid.astype(jnp.int32), xs, infos, wg, wu, wd)
    return y, pos


def _ple_kernel(pos_ref, h_ref, y_hbm, g_ref, wg_ref, p_ref, wp_ref, gf_ref, out_ref, ybuf, sem, *, final):
    tm = h_ref.shape[0]

    def issue(t, c):
        pltpu.make_async_copy(y_hbm.at[pl.ds(pos_ref[0, t], 1)], ybuf.at[pl.ds(t, 1)], sem).start()
        return c

    lax.fori_loop(0, tm, issue, 0, unroll=8)
    pltpu.make_async_copy(y_hbm.at[pl.ds(0, tm)], ybuf, sem).wait()
    x = h_ref[...] + _from_slabs(ybuf)
    xn = _rms(x, g_ref[...]).astype(BF16)
    gate = jax.nn.sigmoid(jnp.dot(xn, wg_ref[...], preferred_element_type=F32))
    proj = jnp.dot(p_ref[...].astype(BF16), wp_ref[...], preferred_element_type=F32)
    y = x + gate * proj
    if final:
        y = _rms(y, gf_ref[...])
    out_ref[...] = y


def _ple(h, y_sorted, pos, g, wg, p, wp, gf, final, tm=512):
    s, d = h.shape
    tm = min(tm, s)
    wg, wp = wg.astype(BF16), wp.astype(BF16)
    const = lambda a: pl.BlockSpec(a.shape, lambda i: (0, 0))
    return pl.pallas_call(
        functools.partial(_ple_kernel, final=final),
        out_shape=jax.ShapeDtypeStruct((s, d), F32),
        grid=(s // tm,),
        in_specs=[pl.BlockSpec((None, 1, tm), lambda i: (i, 0, 0), memory_space=pltpu.SMEM),
                  pl.BlockSpec((tm, d), lambda i: (i, 0)), pl.BlockSpec(memory_space=pl.ANY), const(g), const(wg),
                  pl.BlockSpec((tm, p.shape[1]), lambda i: (i, 0)), const(wp), const(gf)],
        out_specs=pl.BlockSpec((tm, d), lambda i: (i, 0)),
        scratch_shapes=[pltpu.VMEM((tm, SUBLANES, LANES), F32), pltpu.SemaphoreType.DMA],
        compiler_params=_cparams(("arbitrary",)),
        name="ple",
    )(pos.reshape(s // tm, 1, tm), h, y_sorted, g, wg, p, wp, gf)


PARTIAL_KPERM = ((8, 16), (0, 8), (16, 64))
AXIAL_KPERM = ((16, 32), (0, 16), (48, 64), (32, 48))
B_PAIRS = ((128, 1), (512, 4), (2048, 16))


def _partial_tables(s):
    inv = ROPE_THETA ** (-jnp.arange(0, ROT_DIM, 2, dtype=F32) / ROT_DIM)
    ang = jnp.arange(s).astype(F32)[:, None] * inv[None, :]
    return _rope_tables([ang], s, LOG2E * HEAD_DIM ** -0.5)


def _axial_tables(s):
    half = HEAD_DIM // 2
    inv = AXIAL_THETA ** (-jnp.arange(0, half, 2, dtype=F32) / half)
    t = jnp.arange(s)
    ang_r = (t // GRID_W).astype(F32)[:, None] * inv[None, :]
    ang_c = (t % GRID_W).astype(F32)[:, None] * inv[None, :]
    return _rope_tables([ang_r, ang_c], s, LOG2E * HEAD_DIM ** -0.5)


def _mixer_diff(h, g, w_in, w_out, lam_rows, subln, lam_init, tabs):
    d = h.shape[1]
    heads = d // (2 * HEAD_DIM)
    aw = 2 * heads * HEAD_DIM
    dv = 2 * HEAD_DIM
    q, kt, v = _project(h, g, w_in[:, :aw], w_in[:, aw:2 * aw], w_in[:, 2 * aw:], tabs["row_scaled"], tabs["t"],
                        hr=2 * heads, ht=2 * heads, hv=heads, dv=dv, nv=2 * dv, v_t=False,
                        shift=ROT_DIM // 2, tperm=PARTIAL_KPERM, qk_norm=False)
    o = _flash(q, kt, v, units=heads, q_per_unit=2, k_per_unit=2, v_per_unit=1,
               streams=((0, 0, 0), (1, 1, 0)), split=4, tq=1024, tk=2048, nv=2 * dv, dv=dv, out_w=dv,
               mode="diff", lam=lam_rows, subln=subln, lam_init=lam_init, name="flash_diff")
    return _outproj(h, o, w_out)


def _mixer_dilated(h, g, w_in, w_out, tabs):
    nh, hg = 12, 4
    bw = nh * HEAD_DIM
    q, kt, v = _project(h, g, w_in[:, :bw], w_in[:, bw:2 * bw], w_in[:, 2 * bw:], tabs["row_scaled"], tabs["t"],
                        hr=nh, ht=nh, hv=nh, dv=HEAD_DIM, nv=LANES, v_t=False,
                        shift=ROT_DIM // 2, tperm=PARTIAL_KPERM, qk_norm=False)
    os_, ls_ = [], []
    for gi, (win, dil) in enumerate(B_PAIRS):
        halfw = (win // (2 * dil)) * dil
        o, lse = _flash(q, kt, v, units=1, unit0=gi, q_per_unit=hg, k_per_unit=hg, v_per_unit=hg,
                        streams=tuple((j, j, j) for j in range(hg)), tq=256, tk=max(halfw, 256),
                        nv=LANES, dv=HEAD_DIM, out_w=hg * HEAD_DIM, band=(halfw, dil),
                        halo=-(-halfw // LANES) * LANES, mode="lse",
                        name=f"flash_dilated{gi}")
        os_.append(o)
        ls_.append(lse)
    return _outproj_groups(h, os_, ls_, w_out)


GQA_HEADS, GQA_KV_HEADS = 16, 4


def _mixer_window(h, g, w_in, w_out, sink, tabs):
    qd, kvd = GQA_HEADS * HEAD_DIM, GQA_KV_HEADS * HEAD_DIM
    grp = GQA_HEADS // GQA_KV_HEADS
    q, kt, v = _project(h, g, w_in[:, :qd], w_in[:, qd:qd + kvd], w_in[:, qd + kvd:], tabs["row_scaled"], tabs["t"],
                        hr=GQA_HEADS, ht=GQA_KV_HEADS, hv=GQA_KV_HEADS, dv=HEAD_DIM, nv=LANES, v_t=False,
                        shift=ROT_DIM // 2, tperm=PARTIAL_KPERM, qk_norm=False)
    o = _flash(q, kt, v, units=GQA_KV_HEADS, q_per_unit=grp, k_per_unit=1, v_per_unit=1,
               streams=tuple((j, 0, 0) for j in range(grp)), tq=256, tk=256, nv=LANES, dv=HEAD_DIM,
               out_w=grp * HEAD_DIM, band=(128, 1), halo=128, mode="gqa", sink=sink.reshape(GQA_KV_HEADS, grp),
               name="flash_window")
    return _outproj(h, o, w_out)


def _mixer_axial(h, g, w_in, w_out, q_norm, k_norm, tabs):
    qd, kvd = GQA_HEADS * HEAD_DIM, GQA_KV_HEADS * HEAD_DIM
    grp = GQA_HEADS // GQA_KV_HEADS
    qg = jnp.pad(q_norm, (0, LANES - HEAD_DIM)).reshape(1, LANES)
    kg = k_norm.reshape(HEAD_DIM, 1)
    q, kt, v = _project(h, g, w_in[:, :qd], w_in[:, qd:qd + kvd], w_in[:, qd + kvd:], tabs["row_scaled"], tabs["t"],
                        hr=GQA_HEADS, ht=GQA_KV_HEADS, hv=GQA_KV_HEADS, dv=HEAD_DIM, nv=LANES, v_t=False,
                        shift=HEAD_DIM // 4, tperm=AXIAL_KPERM, qk_norm=True, rg=qg, tg=kg)
    o = _flash(q, kt, v, units=GQA_KV_HEADS, q_per_unit=grp, k_per_unit=1, v_per_unit=1,
               streams=tuple((j, 0, 0) for j in range(grp)), split=4, tq=1024, tk=2048, nv=LANES, dv=HEAD_DIM,
               out_w=grp * HEAD_DIM, mode="gqa", name="flash_axial")
    return _outproj(h, o, w_out)


def kernel(x, p, norm_mix, norm_ffn, norm_ple, norm_final, a_w_in, a_w_out, a_lam_q1, a_lam_k1, a_lam_q2, a_lam_k2, a_subln, b_w_in, b_w_out, c_w_in, c_w_out, c_sink, d_w_in, d_w_out, d_q_norm, d_k_norm, moe_w_group, moe_b_group, moe_w_expert, moe_b_expert, moe_w_gate, moe_w_up, moe_w_down, ple_w_gate, ple_w_proj):
    bn, s, d = x.shape
    assert bn == 1
    depth = p.shape[0]
    h = x[0]
    ptabs = _partial_tables(s)
    atabs = _axial_tables(s)
    row = lambda a: a.reshape(1, -1)
    for i in range(depth):
        r, kind = divmod(i, 4)
        g = row(norm_mix[i])
        if kind == 0:
            lam_init = 0.8 - 0.6 * math.exp(-0.3 * i)
            lam_rows = jnp.stack([a_lam_q1[r], a_lam_k1[r], a_lam_q2[r], a_lam_k2[r]])
            h = _mixer_diff(h, g, a_w_in[r], a_w_out[r], lam_rows, row(a_subln[r]), lam_init, ptabs)
        elif kind == 1:
            h = _mixer_dilated(h, g, b_w_in[r], b_w_out[r], ptabs)
        elif kind == 2:
            h = _mixer_window(h, g, c_w_in[r], c_w_out[r], c_sink[r], ptabs)
        else:
            h = _mixer_axial(h, g, d_w_in[r], d_w_out[r], d_q_norm[r], d_k_norm[r], atabs)
        y_sorted, pos = _moe_sparse(h, row(norm_ffn[i]), moe_w_group[i], moe_b_group[i], moe_w_expert[i],
                                    moe_b_expert[i], moe_w_gate[i], moe_w_up[i], moe_w_down[i])
        h = _ple(h, y_sorted, pos, row(norm_ple[i]), ple_w_gate[i], p[i, 0], ple_w_proj[i], row(norm_final), final=(i == depth - 1))
    return h[None]
```

```python
import functools
import math

import jax
import jax.numpy as jnp
from jax import lax
from jax.experimental import pallas as pl
from jax.experimental.pallas import tpu as pltpu

F32 = jnp.float32
BF16 = jnp.bfloat16

HEAD_DIM = 64
LANES = 128
EPS = 1e-6
LOG2E = 1.4426950408889634
NEG_INF = -1e30
M_FLOOR = -1e29
ROPE_THETA = 500000.0
ROT_DIM = HEAD_DIM // 4
AXIAL_THETA = 10000.0
GRID_W = 64
MOE_GROUPS = 4
MOE_PER_GROUP = 4
MOE_EXPERTS = 16
VMEM_LIMIT = 56 * 1024 * 1024


def _cparams(sem):
    return pltpu.CompilerParams(dimension_semantics=sem, vmem_limit_bytes=VMEM_LIMIT)


def _rms(x, g):
    return x * lax.rsqrt(jnp.mean(x * x, axis=-1, keepdims=True) + EPS) * g


def _proj_kernel(h_ref, g_ref, wr_ref, wtt_ref, wv_ref, rc_ref, rs1_ref, rs2_ref, tc_ref, ts_ref,
                 rg_ref, tg_ref, r_ref, t_ref, v_ref, *, hr, ht, nv, dv, shift, tperm, qk_norm):
    xn = _rms(h_ref[...], g_ref[...]).astype(BF16)
    rf = jnp.dot(xn, wr_ref[...], preferred_element_type=F32)
    rc, rs1, rs2 = rc_ref[...], rs1_ref[...], rs2_ref[...]
    low = lax.broadcasted_iota(jnp.int32, (rf.shape[0], LANES), 1) < HEAD_DIM
    for j in range(hr // 2):
        s = rf[:, j * LANES:(j + 1) * LANES]
        if qk_norm:
            sq = s * s
            ms = jnp.where(low, jnp.sum(jnp.where(low, sq, 0.0), axis=-1, keepdims=True),
                           jnp.sum(jnp.where(low, 0.0, sq), axis=-1, keepdims=True)) * (1.0 / HEAD_DIM)
            s = s * lax.rsqrt(ms + EPS) * rg_ref[...]
        r = s * rc + pltpu.roll(s, LANES - shift, 1) * rs1 + pltpu.roll(s, shift, 1) * rs2
        r_ref[2 * j] = r[:, :HEAD_DIM].astype(BF16)
        r_ref[2 * j + 1] = r[:, HEAD_DIM:].astype(BF16)
    nt = (((1,), (1,)), ((), ()))
    tf = lax.dot_general(wtt_ref[...], xn, nt, preferred_element_type=F32)
    tc, ts = tc_ref[...], ts_ref[...]
    for h in range(ht):
        s = tf[h * HEAD_DIM:(h + 1) * HEAD_DIM, :]
        if qk_norm:
            ms = jnp.sum(s * s, axis=0, keepdims=True) * (1.0 / HEAD_DIM)
            s = s * lax.rsqrt(ms + EPS) * tg_ref[...]
        partner = jnp.concatenate([s[a:b] for a, b in tperm], axis=0)
        t_ref[h] = (s * tc + partner * ts).astype(BF16)
    vf = jnp.dot(xn, wv_ref[...], preferred_element_type=F32)
    lane = lax.broadcasted_iota(jnp.int32, vf.shape, 1)
    v_ref[...] = jnp.where((lane & (nv - 1)) == dv, 1.0, vf).astype(BF16)


def _project(h, g, wq, wk, wv, row_tabs, t_tabs, *, hr, ht, hv, dv, nv, shift, tperm, qk_norm, rg=None, tg=None,
             tm=512):
    s, d = h.shape
    tm = min(tm, s)
    assert hr % 2 == 0
    wr = wq.astype(BF16)
    wtt = wk.T.astype(BF16)
    wv_p = jnp.pad(wv.reshape(d, hv, dv), ((0, 0), (0, 0), (0, nv - dv))).reshape(d, hv * nv).astype(BF16)
    rc, rs1, rs2 = row_tabs
    tc, ts = t_tabs
    if rg is None:
        rg = jnp.ones((1, LANES), F32)
        tg = jnp.ones((HEAD_DIM, 1), F32)
    full = lambda a: pl.BlockSpec(a.shape, lambda i: (0,) * a.ndim)
    kern = functools.partial(_proj_kernel, hr=hr, ht=ht, nv=nv, dv=dv, shift=shift, tperm=tperm, qk_norm=qk_norm)
    return pl.pallas_call(
        kern,
        out_shape=(jax.ShapeDtypeStruct((hr, s, HEAD_DIM), BF16),
                   jax.ShapeDtypeStruct((ht, HEAD_DIM, s), BF16),
                   jax.ShapeDtypeStruct((s, hv * nv), BF16)),
        grid=(s // tm,),
        in_specs=[pl.BlockSpec((tm, d), lambda i: (i, 0)), full(g), full(wr), full(wtt), full(wv_p),
                  pl.BlockSpec((tm, LANES), lambda i: (i, 0)), pl.BlockSpec((tm, LANES), lambda i: (i, 0)),
                  pl.BlockSpec((tm, LANES), lambda i: (i, 0)),
                  pl.BlockSpec((HEAD_DIM, tm), lambda i: (0, i)), pl.BlockSpec((HEAD_DIM, tm), lambda i: (0, i)),
                  full(rg), full(tg)],
        out_specs=(pl.BlockSpec((hr, tm, HEAD_DIM), lambda i: (0, i, 0)),
                   pl.BlockSpec((ht, HEAD_DIM, tm), lambda i: (0, 0, i)),
                   pl.BlockSpec((tm, hv * nv), lambda i: (i, 0))),
        compiler_params=_cparams(("parallel",)),
        name="proj",
    )(h, g, wr, wtt, wv_p, rc, rs1, rs2, tc, ts, rg, tg)


def _rope_tables(ang_list, s, scale):
    cs, sn1, sn2, ksn = [], [], [], []
    used = 0
    for ang in ang_list:
        c, sn = jnp.cos(ang), jnp.sin(ang)
        z = jnp.zeros_like(sn)
        cs += [c, c]
        sn1 += [-sn, z]
        sn2 += [z, sn]
        ksn += [-sn, sn]
        used += 2 * ang.shape[1]
    rest = HEAD_DIM - used
    ones, zeros = jnp.ones((s, rest), F32), jnp.zeros((s, rest), F32)
    c64 = jnp.concatenate(cs + [ones], axis=1)
    twice = lambda parts: jnp.concatenate(parts + parts, axis=1)
    row = (twice([c64]), twice(sn1 + [zeros]), twice(sn2 + [zeros]))
    tr = (c64.T, jnp.concatenate(ksn + [zeros], axis=1).T)
    return {"row_scaled": tuple(t * scale for t in row), "t": tr}


def _lane_tile(x, width):
    reps = width // LANES
    return x if reps == 1 else jnp.concatenate([x] * reps, axis=1)


def _flash_kernel(*refs, streams, split, tq, tk, nv, dv, band, halo, seq, mode, lam_init, has_sink):
    it = iter(refs)
    q_ref, kt_ref, v_ref = next(it), next(it), next(it)
    sink_ref = next(it) if has_sink else None
    lam_ref = subln_ref = None
    if mode == "diff":
        lam_ref, subln_ref = next(it), next(it)
    o_ref = next(it)
    lse_ref = next(it) if mode == "lse" else None
    m_scr, acc_scr = next(it), next(it)

    qb, kb = pl.program_id(1), pl.program_id(2)
    nsteps = pl.num_programs(2)
    cr = tq // split

    @pl.when(kb == 0)
    def _init():
        for si, (qi, _, _) in enumerate(streams):
            if has_sink:
                sink2 = jnp.full((tq, LANES), LOG2E, F32) * sink_ref[pl.program_id(0), qi]
                m0 = jnp.maximum(sink2, M_FLOOR)
                m_scr[si] = m0
                lane = lax.broadcasted_iota(jnp.int32, (tq, nv), 1)
                acc_scr[si] = jnp.where(lane == dv, _lane_tile(jnp.exp2(sink2 - m0), nv), 0.0)
            else:
                m_scr[si] = jnp.full((tq, LANES), M_FLOOR, F32)
                acc_scr[si] = jnp.zeros((tq, nv), F32)

    def _step():
        chunks = [(si, c * cr, qi, ki, vi) for si, (qi, ki, vi) in enumerate(streams) for c in range(split)]
        scores = [jnp.dot(q_ref[qi, r0:r0 + cr, :], kt_ref[ki], preferred_element_type=F32)
                  for _, r0, qi, ki, _ in chunks]
        for sc, (si, r0, _, _, vi) in zip(scores, chunks):
            if band is not None:
                halfw, dil, off = band
                row = lax.broadcasted_iota(jnp.int32, (cr, tk), 0) + r0
                col = lax.broadcasted_iota(jnp.int32, (cr, tk), 1)
                diff = col - row + kshift
                valid = jnp.abs(diff) <= halfw
                if dil > 1:
                    valid = valid & ((diff & (dil - 1)) == 0)
                sc = jnp.where(valid, sc, NEG_INF)
            m_old = m_scr[si, r0:r0 + cr, :]
            m_new = jnp.maximum(m_old, jnp.broadcast_to(jnp.max(sc, axis=1, keepdims=True), (cr, LANES)))
            p = jnp.exp2(sc - _lane_tile(m_new, tk))
            alpha = jnp.exp2(m_old - m_new)
            pv = jnp.dot(p.astype(BF16), v_ref[:, vi * nv:(vi + 1) * nv], preferred_element_type=F32)
            acc_scr[si, r0:r0 + cr, :] = acc_scr[si, r0:r0 + cr, :] * _lane_tile(alpha, nv) + pv
            m_scr[si, r0:r0 + cr, :] = m_new

    if band is None:
        _step()
    elif halo is not None:
        kshift = jnp.clip(qb * tq - halo, 0, seq - tk) - qb * tq
        _step()
    else:
        _, _, off = band
        kshift = (kb - off) * tk
        kabs = qb * (tq // tk) + kb - off
        pl.when((kabs >= 0) & (kabs < seq // tk))(_step)

    @pl.when(kb == nsteps - 1)
    def _fin():
        if mode == "diff":
            a0, a1 = acc_scr[0], acc_scr[1]
            lam_rows = lam_ref[...]
            lam = (jnp.exp(jnp.sum(lam_rows[0:1] * lam_rows[1:2], axis=1, keepdims=True))
                   - jnp.exp(jnp.sum(lam_rows[2:3] * lam_rows[3:4], axis=1, keepdims=True)) + lam_init)
            o = a0[:, :dv] / a0[:, dv:dv + 1] - lam * (a1[:, :dv] / a1[:, dv:dv + 1])
            o = _rms(o, subln_ref[...]) * (1.0 - lam_init)
            o_ref[...] = o.astype(o_ref.dtype)
        else:
            for si in range(len(streams)):
                a = acc_scr[si]
                l = a[:, dv:dv + 1]
                o_ref[:, si * dv:(si + 1) * dv] = (a[:, :dv] / l).astype(o_ref.dtype)
                if mode == "lse":
                    lse_ref[:, si * dv:(si + 1) * dv] = m_scr[si][:, :dv] + jnp.log2(l)


def _flash(q, kt, v, *, units, q_per_unit, k_per_unit, v_per_unit, streams, tq, tk, nv, dv, out_w,
           split=1, unit0=0, band=None, halo=None, mode="gqa", sink=None, lam=None, subln=None, lam_init=0.0,
           name="flash"):
    s = q.shape[1]
    tq, tk = min(tq, s), min(tk, s)
    if halo is not None and tq + 2 * halo > s:
        halo, tq = None, max(tq, tk)
    if halo is not None:
        tk = tq + 2 * halo
    assert s % tq == 0 and (halo is not None or (s % tk == 0 and (band is None or tq % tk == 0)))
    n_kblocks = s // tk
    kspec = vspec = None
    if band is None:
        nsteps = n_kblocks
        kmap = lambda u, i, j: (u + unit0, 0, j)
        vmap = lambda u, i, j: (j, u + unit0)
    elif halo is not None:
        assert tq % LANES == 0 and halo % LANES == 0
        nsteps = 1
        band = (band[0], band[1], 0)
        koff = lambda i: jnp.clip(i * (tq // LANES) - halo // LANES, 0, (s - tk) // LANES) * LANES
        kspec = pl.BlockSpec((pl.Element(k_per_unit), pl.Element(HEAD_DIM), pl.Element(tk)),
                             lambda u, i, j: ((u + unit0) * k_per_unit, 0, koff(i)))
        vspec = pl.BlockSpec((pl.Element(tk), pl.Element(v_per_unit * nv)),
                             lambda u, i, j: (koff(i), (u + unit0) * v_per_unit * nv))
    else:
        halfw = band[0]
        off = -(-halfw // tk)
        nsteps = tq // tk + 2 * off
        band = (band[0], band[1], off)
        kidx = lambda i, j: jnp.clip(i * (tq // tk) + j - off, 0, n_kblocks - 1)
        kmap = lambda u, i, j: (u + unit0, 0, kidx(i, j))
        vmap = lambda u, i, j: (kidx(i, j), u + unit0)
    in_specs = [pl.BlockSpec((q_per_unit, tq, HEAD_DIM), lambda u, i, j: (u + unit0, i, 0)),
                kspec or pl.BlockSpec((k_per_unit, HEAD_DIM, tk), kmap),
                vspec or pl.BlockSpec((tk, v_per_unit * nv), vmap)]
    args = [q, kt, v]
    if sink is not None:
        in_specs.append(pl.BlockSpec(memory_space=pltpu.SMEM))
        args.append(sink)
    if mode == "diff":
        in_specs += [pl.BlockSpec(lam.shape, lambda u, i, j: (0, 0)), pl.BlockSpec(subln.shape, lambda u, i, j: (0, 0))]
        args += [lam, subln]
    out_shape = [jax.ShapeDtypeStruct((s, units * out_w), BF16)]
    out_specs = [pl.BlockSpec((tq, out_w), lambda u, i, j: (i, u))]
    if mode == "lse":
        out_shape.append(jax.ShapeDtypeStruct((s, units * out_w), F32))
        out_specs.append(pl.BlockSpec((tq, out_w), lambda u, i, j: (i, u)))
    assert tq % split == 0
    kern = functools.partial(_flash_kernel, streams=streams, split=split, tq=tq, tk=tk, nv=nv, dv=dv, band=band,
                             halo=halo, seq=s, mode=mode, lam_init=lam_init, has_sink=sink is not None)
    res = pl.pallas_call(
        kern,
        out_shape=tuple(out_shape),
        grid=(units, s // tq, nsteps),
        in_specs=in_specs,
        out_specs=tuple(out_specs),
        scratch_shapes=[pltpu.VMEM((len(streams), tq, LANES), F32), pltpu.VMEM((len(streams), tq, nv), F32)],
        compiler_params=_cparams(("parallel", "parallel", "arbitrary")),
        name=name,
    )(*args)
    return res if mode == "lse" else res[0]


def _outproj_kernel(h_ref, o_ref, w_ref, out_ref):
    out_ref[...] = h_ref[...] + jnp.dot(o_ref[...], w_ref[...], preferred_element_type=F32)


def _outproj(h, o, w, tm=512):
    s, d = h.shape
    tm = min(tm, s)
    w = w.astype(BF16)
    return pl.pallas_call(
        _outproj_kernel,
        out_shape=jax.ShapeDtypeStruct((s, d), F32),
        grid=(s // tm,),
        in_specs=[pl.BlockSpec((tm, d), lambda i: (i, 0)), pl.BlockSpec((tm, o.shape[1]), lambda i: (i, 0)),
                  pl.BlockSpec(w.shape, lambda i: (0, 0))],
        out_specs=pl.BlockSpec((tm, d), lambda i: (i, 0)),
        compiler_params=_cparams(("parallel",)),
        name="outproj",
    )(h, o, w)


def _outproj_groups_kernel(h_ref, o0_ref, o1_ref, o2_ref, l0_ref, l1_ref, l2_ref, w_ref, out_ref):
    l0, l1, l2 = l0_ref[...], l1_ref[...], l2_ref[...]
    mx = jnp.maximum(jnp.maximum(l0, l1), l2)
    e0, e1, e2 = jnp.exp2(l0 - mx), jnp.exp2(l1 - mx), jnp.exp2(l2 - mx)
    tot = e0 + e1 + e2
    acc = h_ref[...]
    gw = o0_ref.shape[1]
    for g, (o_ref, e) in enumerate(((o0_ref, e0), (o1_ref, e1), (o2_ref, e2))):
        og = (o_ref[...].astype(F32) * (e / tot)).astype(BF16)
        acc = acc + jnp.dot(og, w_ref[g * gw:(g + 1) * gw, :], preferred_element_type=F32)
    out_ref[...] = acc


def _outproj_groups(h, os_, ls_, w, tm=512):
    s, d = h.shape
    tm = min(tm, s)
    w = w.astype(BF16)
    gw = os_[0].shape[1]
    row = lambda width: pl.BlockSpec((tm, width), lambda i: (i, 0))
    return pl.pallas_call(
        _outproj_groups_kernel,
        out_shape=jax.ShapeDtypeStruct((s, d), F32),
        grid=(s // tm,),
        in_specs=[row(d)] + [row(gw)] * 6 + [pl.BlockSpec(w.shape, lambda i: (0, 0))],
        out_specs=row(d),
        compiler_params=_cparams(("parallel",)),
        name="outproj_groups",
    )(h, *os_, *ls_, w)


def _route(logits):
    lane = lax.broadcasted_iota(jnp.int32, logits.shape, 1).astype(F32)
    big = 1e6
    gl = jnp.where(lane < MOE_GROUPS, logits, NEG_INF)
    gmax = jnp.max(gl, axis=1, keepdims=True)
    gidx = jnp.min(jnp.where(gl == gmax, lane, big), axis=1, keepdims=True)
    gw = 1.0 / jnp.sum(jnp.exp(gl - gmax), axis=1, keepdims=True)
    lo = MOE_GROUPS + gidx * MOE_PER_GROUP
    el = jnp.where((lane >= lo) & (lane < lo + MOE_PER_GROUP), logits, NEG_INF)
    v1 = jnp.max(el, axis=1, keepdims=True)
    i1 = jnp.min(jnp.where(el == v1, lane, big), axis=1, keepdims=True)
    el2 = jnp.where(lane == i1, NEG_INF, el)
    v2 = jnp.max(el2, axis=1, keepdims=True)
    i2 = jnp.min(jnp.where(el2 == v2, lane, big), axis=1, keepdims=True)
    e2 = jnp.exp(v2 - v1)
    w1 = gw / (1.0 + e2)
    w2 = w1 * e2
    return jnp.where(lane == i1, w1, 0.0) + jnp.where(lane == i2, w2, 0.0), gidx


MOE_ROW_TILE = 1024
INFO_GROUP_LANE = 0
INFO_RANK_LANE = 1
SUBLANES = 8


def _to_slabs(ref, x):
    for c in range(SUBLANES):
        ref[:, c, :] = x[:, c * LANES:(c + 1) * LANES]


def _from_slabs(ref):
    return jnp.concatenate([ref[:, c, :] for c in range(SUBLANES)], axis=1)


def _moe_route_kernel(h_ref, g_ref, wr_ref, br_ref, tri_ref, x3_ref, info_ref, cnt_ref, run_scr):
    @pl.when(pl.program_id(0) == 0)
    def _init():
        run_scr[...] = jnp.zeros_like(run_scr)

    xn = _rms(h_ref[...], g_ref[...])
    logits = jnp.dot(xn, wr_ref[...], preferred_element_type=F32, precision=lax.Precision.HIGHEST) + br_ref[...]
    comb, gidx = _route(logits)
    lane = lax.broadcasted_iota(jnp.int32, comb.shape, 1).astype(F32)
    onehot = lane == gidx
    before = jnp.dot(tri_ref[...], onehot.astype(BF16), preferred_element_type=F32)
    run = run_scr[...]
    rank = jnp.sum(jnp.where(onehot, before + run, 0.0), axis=1, keepdims=True)
    run = run + jnp.sum(onehot.astype(F32), axis=0, keepdims=True)
    run_scr[...] = run
    cnt_ref[...] = run
    _to_slabs(x3_ref, xn)
    info_ref[...] = jnp.where(lane == INFO_GROUP_LANE, gidx, jnp.where(lane == INFO_RANK_LANE, rank, comb))


def _moe_dispatch_kernel(pos_ref, x3_ref, info_ref, xs_in_ref, infos_in_ref, xs_ref, infos_ref, sem):
    del xs_in_ref, infos_in_ref
    tm = x3_ref.shape[0]

    def issue(t, c):
        p = pos_ref[0, t]
        pltpu.make_async_copy(x3_ref.at[pl.ds(t, 1)], xs_ref.at[pl.ds(p, 1)], sem).start()
        pltpu.make_async_copy(info_ref.at[pl.ds(t, 1)], infos_ref.at[pl.ds(p, 1)], sem).start()
        return c

    lax.fori_loop(0, tm, issue, 0, unroll=8)
    pltpu.make_async_copy(x3_ref, xs_ref.at[pl.ds(0, tm)], sem).wait()
    pltpu.make_async_copy(info_ref, infos_ref.at[pl.ds(0, tm)], sem).wait()


def _moe_expert_kernel(tg_ref, nv_ref, xs_ref, infos_ref, wg_ref, wu_ref, wd_ref, y_ref, x_scr, acc_scr):
    j, e = pl.program_id(0), pl.program_id(1)
    last = pl.num_programs(1) - 1

    @pl.when(j < nv_ref[0])
    def _live():
        @pl.when(e == 0)
        def _first():
            acc_scr[...] = jnp.zeros_like(acc_scr)
            x_scr[...] = _from_slabs(xs_ref).astype(BF16)

        x = x_scr[...]
        info = infos_ref[...]
        lane = lax.broadcasted_iota(jnp.int32, info.shape, 1)
        eid = tg_ref[j] * MOE_PER_GROUP + e + MOE_GROUPS
        we = jnp.sum(jnp.where(lane == eid, info, 0.0), axis=1, keepdims=True)
        gate = jnp.dot(x, wg_ref[0], preferred_element_type=F32)
        up = jnp.dot(x, wu_ref[0], preferred_element_type=F32)
        act = gate * jax.nn.sigmoid(gate) * up * we
        acc_scr[...] += jnp.dot(act.astype(BF16), wd_ref[0], preferred_element_type=F32)

        @pl.when(e == last)
        def _out():
            _to_slabs(y_ref, acc_scr[...])

    @pl.when((j >= nv_ref[0]) & (e == last))
    def _dead():
        y_ref[...] = jnp.zeros_like(y_ref)


def _moe_sparse(h, g, w_group, b_group, w_expert, b_expert, w_gate, w_up, w_down, tm=1024, tmd=512):
    s, d = h.shape
    assert d == SUBLANES * LANES
    tm, tmd, tb = min(tm, s), min(tmd, s), min(MOE_ROW_TILE, s)
    ne, _, ff = w_gate.shape
    nr = MOE_GROUPS + MOE_EXPERTS
    wr = jnp.pad(jnp.concatenate([w_group, w_expert], axis=1), ((0, 0), (0, LANES - nr)))
    br = jnp.pad(jnp.concatenate([b_group, b_expert]), (0, LANES - nr)).reshape(1, LANES)
    tri = (lax.broadcasted_iota(jnp.int32, (tm, tm), 0) > lax.broadcasted_iota(jnp.int32, (tm, tm), 1)).astype(BF16)
    const = lambda a: pl.BlockSpec(a.shape, lambda i: (0, 0))
    slab = lambda rows: pl.BlockSpec((rows, SUBLANES, LANES), lambda i: (i, 0, 0))
    x3, info, cnt = pl.pallas_call(
        _moe_route_kernel,
        out_shape=(jax.ShapeDtypeStruct((s, SUBLANES, LANES), F32), jax.ShapeDtypeStruct((s, LANES), F32),
                   jax.ShapeDtypeStruct((1, LANES), F32)),
        grid=(s // tm,),
        in_specs=[pl.BlockSpec((tm, d), lambda i: (i, 0)), const(g), const(wr), const(br), const(tri)],
        out_specs=(slab(tm), pl.BlockSpec((tm, LANES), lambda i: (i, 0)), pl.BlockSpec((1, LANES), lambda i: (0, 0))),
        scratch_shapes=[pltpu.VMEM((1, LANES), F32)],
        compiler_params=_cparams(("arbitrary",)),
        name="moe_route",
    )(h, g, wr, br, tri)

    counts = cnt[0, :MOE_GROUPS].astype(jnp.int32)
    padded = ((counts + tb - 1) // tb) * tb
    ends = jnp.cumsum(padded)
    starts = ends - padded
    tok_group = info[:, INFO_GROUP_LANE].astype(jnp.int32)
    tok_rank = info[:, INFO_RANK_LANE].astype(jnp.int32)
    pos = starts[tok_group] + tok_rank
    n_tiles = s // tb + MOE_GROUPS
    p_rows = n_tiles * tb
    tile_group = jnp.minimum(jnp.sum((jnp.arange(n_tiles) * tb)[:, None] >= ends[None, :], axis=1), MOE_GROUPS - 1)
    n_valid = (ends[-1] // tb).reshape(1)

    anyspec = pl.BlockSpec(memory_space=pl.ANY)
    xs, infos = pl.pallas_call(
        _moe_dispatch_kernel,
        out_shape=(jax.ShapeDtypeStruct((p_rows, SUBLANES, LANES), F32), jax.ShapeDtypeStruct((p_rows, LANES), F32)),
        grid=(s // tmd,),
        in_specs=[pl.BlockSpec((None, 1, tmd), lambda i: (i, 0, 0), memory_space=pltpu.SMEM),
                  slab(tmd), pl.BlockSpec((tmd, LANES), lambda i: (i, 0)), anyspec, anyspec],
        out_specs=(anyspec, anyspec),
        scratch_shapes=[pltpu.SemaphoreType.DMA],
        input_output_aliases={3: 0, 4: 1},
        compiler_params=_cparams(("arbitrary",)),
        name="moe_dispatch",
    )(pos.reshape(s // tmd, 1, tmd), x3, info, jnp.zeros((p_rows, SUBLANES, LANES), F32),
      jnp.zeros((p_rows, LANES), F32))

    wg, wu, wd = w_gate.astype(BF16), w_up.astype(BF16), w_down.astype(BF16)
    wsel = lambda j, e, tg, nv: (jnp.where(j < nv[0], tg[j] * MOE_PER_GROUP + e, ne - 1), 0, 0)
    y = pl.pallas_call(
        _moe_expert_kernel,
        out_shape=jax.ShapeDtypeStruct((p_rows, SUBLANES, LANES), F32),
        grid_spec=pltpu.PrefetchScalarGridSpec(
            num_scalar_prefetch=2,
            grid=(n_tiles, MOE_PER_GROUP),
            in_specs=[pl.BlockSpec((tb, SUBLANES, LANES), lambda j, e, tg, nv: (j, 0, 0)),
                      pl.BlockSpec((tb, LANES), lambda j, e, tg, nv: (j, 0)),
                      pl.BlockSpec((1, d, ff), wsel), pl.BlockSpec((1, d, ff), wsel), pl.BlockSpec((1, ff, d), wsel)],
            out_specs=pl.BlockSpec((tb, SUBLANES, LANES), lambda j, e, tg, nv: (j, 0, 0)),
            scratch_shapes=[pltpu.VMEM((tb, d), BF16), pltpu.VMEM((tb, d), F32)]),
        compiler_params=_cparams(("arbitrary", "arbitrary")),
        name="moe_expert",
    )(tile_group.astype(jnp.int32), n_valid.astype(jnp.int32), xs, infos, wg, wu, wd)
    return y, pos


def _ple_kernel(pos_ref, posn_ref, h_ref, y_hbm, g_ref, wg_ref, p_ref, wp_ref, gf_ref, out_ref, ybuf, sem, *, final):
    tm = h_ref.shape[0]
    i, n = pl.program_id(0), pl.num_programs(0)
    slot = lax.rem(i, 2)

    def gather(idx_ref, dst_slot):
        def issue(t, c):
            pltpu.make_async_copy(y_hbm.at[pl.ds(idx_ref[0, t], 1)], ybuf.at[dst_slot, pl.ds(t, 1)],
                                  sem.at[dst_slot]).start()
            return c
        lax.fori_loop(0, tm, issue, 0, unroll=8)

    @pl.when(i == 0)
    def _first():
        gather(pos_ref, slot)

    @pl.when(i + 1 < n)
    def _next():
        gather(posn_ref, 1 - slot)

    pltpu.make_async_copy(y_hbm.at[pl.ds(0, tm)], ybuf.at[slot], sem.at[slot]).wait()
    x = h_ref[...] + _from_slabs(ybuf.at[slot])
    xn = _rms(x, g_ref[...]).astype(BF16)
    gate = jax.nn.sigmoid(jnp.dot(xn, wg_ref[...], preferred_element_type=F32))
    proj = jnp.dot(p_ref[...].astype(BF16), wp_ref[...], preferred_element_type=F32)
    y = x + gate * proj
    if final:
        y = _rms(y, gf_ref[...])
    out_ref[...] = y


def _ple(h, y_sorted, pos, g, wg, p, wp, gf, final, tm=512):
    s, d = h.shape
    tm = min(tm, s)
    wg, wp = wg.astype(BF16), wp.astype(BF16)
    const = lambda a: pl.BlockSpec(a.shape, lambda i: (0, 0))
    n = s // tm
    pos3 = pos.reshape(n, 1, tm)
    return pl.pallas_call(
        functools.partial(_ple_kernel, final=final),
        out_shape=jax.ShapeDtypeStruct((s, d), F32),
        grid=(n,),
        in_specs=[pl.BlockSpec((None, 1, tm), lambda i: (i, 0, 0), memory_space=pltpu.SMEM),
                  pl.BlockSpec((None, 1, tm), lambda i: (jnp.minimum(i + 1, n - 1), 0, 0), memory_space=pltpu.SMEM),
                  pl.BlockSpec((tm, d), lambda i: (i, 0)), pl.BlockSpec(memory_space=pl.ANY), const(g), const(wg),
                  pl.BlockSpec((tm, p.shape[1]), lambda i: (i, 0)), const(wp), const(gf)],
        out_specs=pl.BlockSpec((tm, d), lambda i: (i, 0)),
        scratch_shapes=[pltpu.VMEM((2, tm, SUBLANES, LANES), F32), pltpu.SemaphoreType.DMA((2,))],
        compiler_params=_cparams(("arbitrary",)),
        name="ple",
    )(pos3, pos3, h, y_sorted, g, wg, p, wp, gf)


PARTIAL_KPERM = ((8, 16), (0, 8), (16, 64))
AXIAL_KPERM = ((16, 32), (0, 16), (48, 64), (32, 48))
B_PAIRS = ((128, 1), (512, 4), (2048, 16))


def _partial_tables(s):
    inv = ROPE_THETA ** (-jnp.arange(0, ROT_DIM, 2, dtype=F32) / ROT_DIM)
    ang = jnp.arange(s).astype(F32)[:, None] * inv[None, :]
    return _rope_tables([ang], s, LOG2E * HEAD_DIM ** -0.5)


def _axial_tables(s):
    half = HEAD_DIM // 2
    inv = AXIAL_THETA ** (-jnp.arange(0, half, 2, dtype=F32) / half)
    t = jnp.arange(s)
    ang_r = (t // GRID_W).astype(F32)[:, None] * inv[None, :]
    ang_c = (t % GRID_W).astype(F32)[:, None] * inv[None, :]
    return _rope_tables([ang_r, ang_c], s, LOG2E * HEAD_DIM ** -0.5)


def _mixer_diff(h, g, w_in, w_out, lam_rows, subln, lam_init, tabs):
    d = h.shape[1]
    heads = d // (2 * HEAD_DIM)
    aw = 2 * heads * HEAD_DIM
    dv = 2 * HEAD_DIM
    q, kt, v = _project(h, g, w_in[:, :aw], w_in[:, aw:2 * aw], w_in[:, 2 * aw:], tabs["row_scaled"], tabs["t"],
                        hr=2 * heads, ht=2 * heads, hv=heads, dv=dv, nv=2 * dv,
                        shift=ROT_DIM // 2, tperm=PARTIAL_KPERM, qk_norm=False)
    o = _flash(q, kt, v, units=heads, q_per_unit=2, k_per_unit=2, v_per_unit=1,
               streams=((0, 0, 0), (1, 1, 0)), split=4, tq=1024, tk=2048, nv=2 * dv, dv=dv, out_w=dv,
               mode="diff", lam=lam_rows, subln=subln, lam_init=lam_init, name="flash_diff")
    return _outproj(h, o, w_out)


def _mixer_dilated(h, g, w_in, w_out, tabs):
    nh, hg = 12, 4
    bw = nh * HEAD_DIM
    q, kt, v = _project(h, g, w_in[:, :bw], w_in[:, bw:2 * bw], w_in[:, 2 * bw:], tabs["row_scaled"], tabs["t"],
                        hr=nh, ht=nh, hv=nh, dv=HEAD_DIM, nv=LANES,
                        shift=ROT_DIM // 2, tperm=PARTIAL_KPERM, qk_norm=False)
    os_, ls_ = [], []
    for gi, (win, dil) in enumerate(B_PAIRS):
        halfw = (win // (2 * dil)) * dil
        o, lse = _flash(q, kt, v, units=1, unit0=gi, q_per_unit=hg, k_per_unit=hg, v_per_unit=hg,
                        streams=tuple((j, j, j) for j in range(hg)), tq=256, tk=max(halfw, 256),
                        nv=LANES, dv=HEAD_DIM, out_w=hg * HEAD_DIM, band=(halfw, dil),
                        halo=-(-halfw // LANES) * LANES, mode="lse",
                        name=f"flash_dilated{gi}")
        os_.append(o)
        ls_.append(lse)
    return _outproj_groups(h, os_, ls_, w_out)


GQA_HEADS, GQA_KV_HEADS = 16, 4


def _mixer_window(h, g, w_in, w_out, sink, tabs):
    qd, kvd = GQA_HEADS * HEAD_DIM, GQA_KV_HEADS * HEAD_DIM
    grp = GQA_HEADS // GQA_KV_HEADS
    q, kt, v = _project(h, g, w_in[:, :qd], w_in[:, qd:qd + kvd], w_in[:, qd + kvd:], tabs["row_scaled"], tabs["t"],
                        hr=GQA_HEADS, ht=GQA_KV_HEADS, hv=GQA_KV_HEADS, dv=HEAD_DIM, nv=LANES,
                        shift=ROT_DIM // 2, tperm=PARTIAL_KPERM, qk_norm=False)
    o = _flash(q, kt, v, units=GQA_KV_HEADS, q_per_unit=grp, k_per_unit=1, v_per_unit=1,
               streams=tuple((j, 0, 0) for j in range(grp)), tq=256, tk=256, nv=LANES, dv=HEAD_DIM,
               out_w=grp * HEAD_DIM, band=(128, 1), halo=128, mode="gqa", sink=sink.reshape(GQA_KV_HEADS, grp),
               name="flash_window")
    return _outproj(h, o, w_out)


def _mixer_axial(h, g, w_in, w_out, q_norm, k_norm, tabs):
    qd, kvd = GQA_HEADS * HEAD_DIM, GQA_KV_HEADS * HEAD_DIM
    grp = GQA_HEADS // GQA_KV_HEADS
    qg = jnp.tile(q_norm, 2).reshape(1, LANES)
    kg = k_norm.reshape(HEAD_DIM, 1)
    q, kt, v = _project(h, g, w_in[:, :qd], w_in[:, qd:qd + kvd], w_in[:, qd + kvd:], tabs["row_scaled"], tabs["t"],
                        hr=GQA_HEADS, ht=GQA_KV_HEADS, hv=GQA_KV_HEADS, dv=HEAD_DIM, nv=LANES,
                        shift=HEAD_DIM // 4, tperm=AXIAL_KPERM, qk_norm=True, rg=qg, tg=kg)
    o = _flash(q, kt, v, units=GQA_KV_HEADS, q_per_unit=grp, k_per_unit=1, v_per_unit=1,
               streams=tuple((j, 0, 0) for j in range(grp)), split=4, tq=1024, tk=2048, nv=LANES, dv=HEAD_DIM,
               out_w=grp * HEAD_DIM, mode="gqa", name="flash_axial")
    return _outproj(h, o, w_out)


def kernel(x, p, norm_mix, norm_ffn, norm_ple, norm_final, a_w_in, a_w_out, a_lam_q1, a_lam_k1, a_lam_q2, a_lam_k2, a_subln, b_w_in, b_w_out, c_w_in, c_w_out, c_sink, d_w_in, d_w_out, d_q_norm, d_k_norm, moe_w_group, moe_b_group, moe_w_expert, moe_b_expert, moe_w_gate, moe_w_up, moe_w_down, ple_w_gate, ple_w_proj):
    bn, s, d = x.shape
    assert bn == 1
    depth = p.shape[0]
    h = x[0]
    ptabs = _partial_tables(s)
    atabs = _axial_tables(s)
    row = lambda a: a.reshape(1, -1)
    for i in range(depth):
        r, kind = divmod(i, 4)
        g = row(norm_mix[i])
        if kind == 0:
            lam_init = 0.8 - 0.6 * math.exp(-0.3 * i)
            lam_rows = jnp.stack([a_lam_q1[r], a_lam_k1[r], a_lam_q2[r], a_lam_k2[r]])
            h = _mixer_diff(h, g, a_w_in[r], a_w_out[r], lam_rows, row(a_subln[r]), lam_init, ptabs)
        elif kind == 1:
            h = _mixer_dilated(h, g, b_w_in[r], b_w_out[r], ptabs)
        elif kind == 2:
            h = _mixer_window(h, g, c_w_in[r], c_w_out[r], c_sink[r], ptabs)
        else:
            h = _mixer_axial(h, g, d_w_in[r], d_w_out[r], d_q_norm[r], d_k_norm[r], atabs)
        y_sorted, pos = _moe_sparse(h, row(norm_ffn[i]), moe_w_group[i], moe_b_group[i], moe_w_expert[i],
                                    moe_b_expert[i], moe_w_gate[i], moe_w_up[i], moe_w_down[i])
        h = _ple(h, y_sorted, pos, row(norm_ple[i]), ple_w_gate[i], p[i, 0], ple_w_proj[i], row(norm_final), final=(i == depth - 1))
    return h[None]
```

```python
import functools
import math

import jax
import jax.numpy as jnp
from jax import lax
from jax.experimental import pallas as pl
from jax.experimental.pallas import tpu as pltpu

F32 = jnp.float32
BF16 = jnp.bfloat16

HEAD_DIM = 64
LANES = 128
EPS = 1e-6
LOG2E = 1.4426950408889634
NEG_INF = -1e30
M_FLOOR = -1e29
ROPE_THETA = 500000.0
ROT_DIM = HEAD_DIM // 4
AXIAL_THETA = 10000.0
GRID_W = 64
MOE_GROUPS = 4
MOE_PER_GROUP = 4
MOE_EXPERTS = 16
VMEM_LIMIT = 56 * 1024 * 1024


def _cparams(sem):
    return pltpu.CompilerParams(dimension_semantics=sem, vmem_limit_bytes=VMEM_LIMIT)


def _rms(x, g):
    return x * lax.rsqrt(jnp.mean(x * x, axis=-1, keepdims=True) + EPS) * g


def _proj_kernel(h_ref, g_ref, wr_ref, wtt_ref, wv_ref, rc_ref, rs1_ref, rs2_ref, tc_ref, ts_ref,
                 rg_ref, tg_ref, r_ref, t_ref, v_ref, *, hr, ht, nv, dv, shift, tperm, qk_norm):
    xn = _rms(h_ref[...], g_ref[...]).astype(BF16)
    rf = jnp.dot(xn, wr_ref[...], preferred_element_type=F32)
    rc, rs1, rs2 = rc_ref[...], rs1_ref[...], rs2_ref[...]
    low = lax.broadcasted_iota(jnp.int32, (rf.shape[0], LANES), 1) < HEAD_DIM
    for j in range(hr // 2):
        s = rf[:, j * LANES:(j + 1) * LANES]
        if qk_norm:
            sq = s * s
            ms = jnp.where(low, jnp.sum(jnp.where(low, sq, 0.0), axis=-1, keepdims=True),
                           jnp.sum(jnp.where(low, 0.0, sq), axis=-1, keepdims=True)) * (1.0 / HEAD_DIM)
            s = s * lax.rsqrt(ms + EPS) * rg_ref[...]
        r = s * rc + pltpu.roll(s, LANES - shift, 1) * rs1 + pltpu.roll(s, shift, 1) * rs2
        r_ref[2 * j] = r[:, :HEAD_DIM].astype(BF16)
        r_ref[2 * j + 1] = r[:, HEAD_DIM:].astype(BF16)
    nt = (((1,), (1,)), ((), ()))
    tf = lax.dot_general(wtt_ref[...], xn, nt, preferred_element_type=F32)
    tc, ts = tc_ref[...], ts_ref[...]
    for h in range(ht):
        s = tf[h * HEAD_DIM:(h + 1) * HEAD_DIM, :]
        if qk_norm:
            ms = jnp.sum(s * s, axis=0, keepdims=True) * (1.0 / HEAD_DIM)
            s = s * lax.rsqrt(ms + EPS) * tg_ref[...]
        partner = jnp.concatenate([s[a:b] for a, b in tperm], axis=0)
        t_ref[h] = (s * tc + partner * ts).astype(BF16)
    vf = jnp.dot(xn, wv_ref[...], preferred_element_type=F32)
    lane = lax.broadcasted_iota(jnp.int32, vf.shape, 1)
    v_ref[...] = jnp.where((lane & (nv - 1)) == dv, 1.0, vf).astype(BF16)


def _project(h, g, wq, wk, wv, row_tabs, t_tabs, *, hr, ht, hv, dv, nv, shift, tperm, qk_norm, rg=None, tg=None,
             tm=512):
    s, d = h.shape
    tm = min(tm, s)
    assert hr % 2 == 0
    wr = wq.astype(BF16)
    wtt = wk.T.astype(BF16)
    wv_p = jnp.pad(wv.reshape(d, hv, dv), ((0, 0), (0, 0), (0, nv - dv))).reshape(d, hv * nv).astype(BF16)
    rc, rs1, rs2 = row_tabs
    tc, ts = t_tabs
    if rg is None:
        rg = jnp.ones((1, LANES), F32)
        tg = jnp.ones((HEAD_DIM, 1), F32)
    full = lambda a: pl.BlockSpec(a.shape, lambda i: (0,) * a.ndim)
    kern = functools.partial(_proj_kernel, hr=hr, ht=ht, nv=nv, dv=dv, shift=shift, tperm=tperm, qk_norm=qk_norm)
    return pl.pallas_call(
        kern,
        out_shape=(jax.ShapeDtypeStruct((hr, s, HEAD_DIM), BF16),
                   jax.ShapeDtypeStruct((ht, HEAD_DIM, s), BF16),
                   jax.ShapeDtypeStruct((s, hv * nv), BF16)),
        grid=(s // tm,),
        in_specs=[pl.BlockSpec((tm, d), lambda i: (i, 0)), full(g), full(wr), full(wtt), full(wv_p),
                  pl.BlockSpec((tm, LANES), lambda i: (i, 0)), pl.BlockSpec((tm, LANES), lambda i: (i, 0)),
                  pl.BlockSpec((tm, LANES), lambda i: (i, 0)),
                  pl.BlockSpec((HEAD_DIM, tm), lambda i: (0, i)), pl.BlockSpec((HEAD_DIM, tm), lambda i: (0, i)),
                  full(rg), full(tg)],
        out_specs=(pl.BlockSpec((hr, tm, HEAD_DIM), lambda i: (0, i, 0)),
                   pl.BlockSpec((ht, HEAD_DIM, tm), lambda i: (0, 0, i)),
                   pl.BlockSpec((tm, hv * nv), lambda i: (i, 0))),
        compiler_params=_cparams(("parallel",)),
        name="proj",
    )(h, g, wr, wtt, wv_p, rc, rs1, rs2, tc, ts, rg, tg)


def _rope_tables(ang_list, s, scale):
    cs, sn1, sn2, ksn = [], [], [], []
    used = 0
    for ang in ang_list:
        c, sn = jnp.cos(ang), jnp.sin(ang)
        z = jnp.zeros_like(sn)
        cs += [c, c]
        sn1 += [-sn, z]
        sn2 += [z, sn]
        ksn += [-sn, sn]
        used += 2 * ang.shape[1]
    rest = HEAD_DIM - used
    ones, zeros = jnp.ones((s, rest), F32), jnp.zeros((s, rest), F32)
    c64 = jnp.concatenate(cs + [ones], axis=1)
    twice = lambda parts: jnp.concatenate(parts + parts, axis=1)
    row = (twice([c64]), twice(sn1 + [zeros]), twice(sn2 + [zeros]))
    tr = (c64.T, jnp.concatenate(ksn + [zeros], axis=1).T)
    return {"row_scaled": tuple(t * scale for t in row), "t": tr}


def _lane_tile(x, width):
    reps = width // LANES
    return x if reps == 1 else jnp.concatenate([x] * reps, axis=1)


def _flash_kernel(*refs, streams, split, tq, tk, nv, dv, band, halo, seq, mode, lam_init, has_sink):
    it = iter(refs)
    q_ref, kt_ref, v_ref = next(it), next(it), next(it)
    sink_ref = next(it) if has_sink else None
    lam_ref = subln_ref = None
    if mode == "diff":
        lam_ref, subln_ref = next(it), next(it)
    o_ref = next(it)
    lse_ref = next(it) if mode == "lse" else None
    m_scr, acc_scr = next(it), next(it)

    qb, kb = pl.program_id(1), pl.program_id(2)
    nsteps = pl.num_programs(2)
    cr = tq // split

    @pl.when(kb == 0)
    def _init():
        for si, (qi, _, _) in enumerate(streams):
            if has_sink:
                sink2 = jnp.full((tq, LANES), LOG2E, F32) * sink_ref[pl.program_id(0), qi]
                m0 = jnp.maximum(sink2, M_FLOOR)
                m_scr[si] = m0
                lane = lax.broadcasted_iota(jnp.int32, (tq, nv), 1)
                acc_scr[si] = jnp.where(lane == dv, _lane_tile(jnp.exp2(sink2 - m0), nv), 0.0)
            else:
                m_scr[si] = jnp.full((tq, LANES), M_FLOOR, F32)
                acc_scr[si] = jnp.zeros((tq, nv), F32)

    def _step():
        chunks = [(si, c * cr, qi, ki, vi) for si, (qi, ki, vi) in enumerate(streams) for c in range(split)]
        scores = [jnp.dot(q_ref[qi, r0:r0 + cr, :], kt_ref[ki], preferred_element_type=F32)
                  for _, r0, qi, ki, _ in chunks]
        for sc, (si, r0, _, _, vi) in zip(scores, chunks):
            if band is not None:
                halfw, dil, off = band
                row = lax.broadcasted_iota(jnp.int32, (cr, tk), 0) + r0
                col = lax.broadcasted_iota(jnp.int32, (cr, tk), 1)
                diff = col - row + kshift
                valid = jnp.abs(diff) <= halfw
                if dil > 1:
                    valid = valid & ((diff & (dil - 1)) == 0)
                sc = jnp.where(valid, sc, NEG_INF)
            m_old = m_scr[si, r0:r0 + cr, :]
            m_new = jnp.maximum(m_old, jnp.broadcast_to(jnp.max(sc, axis=1, keepdims=True), (cr, LANES)))
            p = jnp.exp2(sc - _lane_tile(m_new, tk))
            alpha = jnp.exp2(m_old - m_new)
            pv = jnp.dot(p.astype(BF16), v_ref[:, vi * nv:(vi + 1) * nv], preferred_element_type=F32)
            acc_scr[si, r0:r0 + cr, :] = acc_scr[si, r0:r0 + cr, :] * _lane_tile(alpha, nv) + pv
            m_scr[si, r0:r0 + cr, :] = m_new

    if band is None:
        _step()
    elif halo is not None:
        kshift = jnp.clip(qb * tq - halo, 0, seq - tk) - qb * tq
        _step()
    else:
        _, _, off = band
        kshift = (kb - off) * tk
        kabs = qb * (tq // tk) + kb - off
        pl.when((kabs >= 0) & (kabs < seq // tk))(_step)

    @pl.when(kb == nsteps - 1)
    def _fin():
        if mode == "diff":
            a0, a1 = acc_scr[0], acc_scr[1]
            lam_rows = lam_ref[...]
            lam = (jnp.exp(jnp.sum(lam_rows[0:1] * lam_rows[1:2], axis=1, keepdims=True))
                   - jnp.exp(jnp.sum(lam_rows[2:3] * lam_rows[3:4], axis=1, keepdims=True)) + lam_init)
            o = a0[:, :dv] / a0[:, dv:dv + 1] - lam * (a1[:, :dv] / a1[:, dv:dv + 1])
            o = _rms(o, subln_ref[...]) * (1.0 - lam_init)
            o_ref[...] = o.astype(o_ref.dtype)
        else:
            for si in range(len(streams)):
                a = acc_scr[si]
                l = a[:, dv:dv + 1]
                o_ref[:, si * dv:(si + 1) * dv] = (a[:, :dv] / l).astype(o_ref.dtype)
                if mode == "lse":
                    lse_ref[:, si * dv:(si + 1) * dv] = m_scr[si][:, :dv] + jnp.log2(l)


def _flash(q, kt, v, *, units, q_per_unit, k_per_unit, v_per_unit, streams, tq, tk, nv, dv, out_w,
           split=1, unit0=0, band=None, halo=None, mode="gqa", sink=None, lam=None, subln=None, lam_init=0.0,
           name="flash"):
    s = q.shape[1]
    tq, tk = min(tq, s), min(tk, s)
    if halo is not None and tq + 2 * halo > s:
        halo, tq = None, max(tq, tk)
    if halo is not None:
        tk = tq + 2 * halo
    assert s % tq == 0 and (halo is not None or (s % tk == 0 and (band is None or tq % tk == 0)))
    n_kblocks = s // tk
    kspec = vspec = None
    if band is None:
        nsteps = n_kblocks
        kmap = lambda u, i, j: (u + unit0, 0, j)
        vmap = lambda u, i, j: (j, u + unit0)
    elif halo is not None:
        assert tq % LANES == 0 and halo % LANES == 0
        nsteps = 1
        band = (band[0], band[1], 0)
        koff = lambda i: jnp.clip(i * (tq // LANES) - halo // LANES, 0, (s - tk) // LANES) * LANES
        kspec = pl.BlockSpec((pl.Element(k_per_unit), pl.Element(HEAD_DIM), pl.Element(tk)),
                             lambda u, i, j: ((u + unit0) * k_per_unit, 0, koff(i)))
        vspec = pl.BlockSpec((pl.Element(tk), pl.Element(v_per_unit * nv)),
                             lambda u, i, j: (koff(i), (u + unit0) * v_per_unit * nv))
    else:
        halfw = band[0]
        off = -(-halfw // tk)
        nsteps = tq // tk + 2 * off
        band = (band[0], band[1], off)
        kidx = lambda i, j: jnp.clip(i * (tq // tk) + j - off, 0, n_kblocks - 1)
        kmap = lambda u, i, j: (u + unit0, 0, kidx(i, j))
        vmap = lambda u, i, j: (kidx(i, j), u + unit0)
    in_specs = [pl.BlockSpec((q_per_unit, tq, HEAD_DIM), lambda u, i, j: (u + unit0, i, 0)),
                kspec or pl.BlockSpec((k_per_unit, HEAD_DIM, tk), kmap),
                vspec or pl.BlockSpec((tk, v_per_unit * nv), vmap)]
    args = [q, kt, v]
    if sink is not None:
        in_specs.append(pl.BlockSpec(memory_space=pltpu.SMEM))
        args.append(sink)
    if mode == "diff":
        in_specs += [pl.BlockSpec(lam.shape, lambda u, i, j: (0, 0)), pl.BlockSpec(subln.shape, lambda u, i, j: (0, 0))]
        args += [lam, subln]
    out_shape = [jax.ShapeDtypeStruct((s, units * out_w), BF16)]
    out_specs = [pl.BlockSpec((tq, out_w), lambda u, i, j: (i, u))]
    if mode == "lse":
        out_shape.append(jax.ShapeDtypeStruct((s, units * out_w), F32))
        out_specs.append(pl.BlockSpec((tq, out_w), lambda u, i, j: (i, u)))
    assert tq % split == 0
    kern = functools.partial(_flash_kernel, streams=streams, split=split, tq=tq, tk=tk, nv=nv, dv=dv, band=band,
                             halo=halo, seq=s, mode=mode, lam_init=lam_init, has_sink=sink is not None)
    res = pl.pallas_call(
        kern,
        out_shape=tuple(out_shape),
        grid=(units, s // tq, nsteps),
        in_specs=in_specs,
        out_specs=tuple(out_specs),
        scratch_shapes=[pltpu.VMEM((len(streams), tq, LANES), F32), pltpu.VMEM((len(streams), tq, nv), F32)],
        compiler_params=_cparams(("parallel", "parallel", "arbitrary")),
        name=name,
    )(*args)
    return res if mode == "lse" else res[0]


def _outproj_kernel(h_ref, o_ref, w_ref, out_ref):
    out_ref[...] = h_ref[...] + jnp.dot(o_ref[...], w_ref[...], preferred_element_type=F32)


def _outproj(h, o, w, tm=512):
    s, d = h.shape
    tm = min(tm, s)
    w = w.astype(BF16)
    return pl.pallas_call(
        _outproj_kernel,
        out_shape=jax.ShapeDtypeStruct((s, d), F32),
        grid=(s // tm,),
        in_specs=[pl.BlockSpec((tm, d), lambda i: (i, 0)), pl.BlockSpec((tm, o.shape[1]), lambda i: (i, 0)),
                  pl.BlockSpec(w.shape, lambda i: (0, 0))],
        out_specs=pl.BlockSpec((tm, d), lambda i: (i, 0)),
        compiler_params=_cparams(("parallel",)),
        name="outproj",
    )(h, o, w)


def _outproj_groups_kernel(h_ref, o0_ref, o1_ref, o2_ref, l0_ref, l1_ref, l2_ref, w_ref, out_ref):
    l0, l1, l2 = l0_ref[...], l1_ref[...], l2_ref[...]
    mx = jnp.maximum(jnp.maximum(l0, l1), l2)
    e0, e1, e2 = jnp.exp2(l0 - mx), jnp.exp2(l1 - mx), jnp.exp2(l2 - mx)
    tot = e0 + e1 + e2
    acc = h_ref[...]
    gw = o0_ref.shape[1]
    for g, (o_ref, e) in enumerate(((o0_ref, e0), (o1_ref, e1), (o2_ref, e2))):
        og = (o_ref[...].astype(F32) * (e / tot)).astype(BF16)
        acc = acc + jnp.dot(og, w_ref[g * gw:(g + 1) * gw, :], preferred_element_type=F32)
    out_ref[...] = acc


def _outproj_groups(h, os_, ls_, w, tm=512):
    s, d = h.shape
    tm = min(tm, s)
    w = w.astype(BF16)
    gw = os_[0].shape[1]
    row = lambda width: pl.BlockSpec((tm, width), lambda i: (i, 0))
    return pl.pallas_call(
        _outproj_groups_kernel,
        out_shape=jax.ShapeDtypeStruct((s, d), F32),
        grid=(s // tm,),
        in_specs=[row(d)] + [row(gw)] * 6 + [pl.BlockSpec(w.shape, lambda i: (0, 0))],
        out_specs=row(d),
        compiler_params=_cparams(("parallel",)),
        name="outproj_groups",
    )(h, *os_, *ls_, w)


PAIR_EXPERTS = ((0, 1), (0, 2), (0, 3), (1, 3), (1, 2), (2, 3))
MOE_PAIRS = len(PAIR_EXPERTS)


def _route(logits):
    lane = lax.broadcasted_iota(jnp.int32, logits.shape, 1).astype(F32)
    big = 1e6
    gl = jnp.where(lane < MOE_GROUPS, logits, NEG_INF)
    gmax = jnp.max(gl, axis=1, keepdims=True)
    gidx = jnp.min(jnp.where(gl == gmax, lane, big), axis=1, keepdims=True)
    gw = 1.0 / jnp.sum(jnp.exp(gl - gmax), axis=1, keepdims=True)
    lo = MOE_GROUPS + gidx * MOE_PER_GROUP
    el = jnp.where((lane >= lo) & (lane < lo + MOE_PER_GROUP), logits, NEG_INF)
    v1 = jnp.max(el, axis=1, keepdims=True)
    i1 = jnp.min(jnp.where(el == v1, lane, big), axis=1, keepdims=True)
    el2 = jnp.where(lane == i1, NEG_INF, el)
    v2 = jnp.max(el2, axis=1, keepdims=True)
    i2 = jnp.min(jnp.where(el2 == v2, lane, big), axis=1, keepdims=True)
    e2 = jnp.exp(v2 - v1)
    w1 = gw / (1.0 + e2)
    w2 = w1 * e2
    a, b = jnp.minimum(i1, i2) - lo, jnp.maximum(i1, i2) - lo
    code = jnp.where(a == 0, b - 1, jnp.where(a == 1, jnp.where(b == 3, 3.0, 4.0), 5.0))
    return jnp.where(lane == i1, w1, 0.0) + jnp.where(lane == i2, w2, 0.0), gidx * MOE_PAIRS + code


MOE_ROW_TILE = 512
INFO_CAT_LANE = 0
INFO_RANK_LANE = 1
SUBLANES = 8


def _to_slabs(ref, x):
    for c in range(SUBLANES):
        ref[:, c, :] = x[:, c * LANES:(c + 1) * LANES]


def _from_slabs(ref):
    return jnp.concatenate([ref[:, c, :] for c in range(SUBLANES)], axis=1)


def _moe_route_kernel(h_ref, g_ref, wr_ref, br_ref, tri_ref, x3_ref, info_ref, cnt_ref, run_scr):
    @pl.when(pl.program_id(0) == 0)
    def _init():
        run_scr[...] = jnp.zeros_like(run_scr)

    xn = _rms(h_ref[...], g_ref[...])
    logits = jnp.dot(xn, wr_ref[...], preferred_element_type=F32, precision=lax.Precision.HIGHEST) + br_ref[...]
    comb, cat = _route(logits)
    lane = lax.broadcasted_iota(jnp.int32, comb.shape, 1).astype(F32)
    onehot = lane == cat
    before = jnp.dot(tri_ref[...], onehot.astype(BF16), preferred_element_type=F32)
    run = run_scr[...]
    rank = jnp.sum(jnp.where(onehot, before + run, 0.0), axis=1, keepdims=True)
    run = run + jnp.sum(onehot.astype(F32), axis=0, keepdims=True)
    run_scr[...] = run
    cnt_ref[...] = run
    _to_slabs(x3_ref, xn)
    info_ref[...] = jnp.where(lane == INFO_CAT_LANE, cat, jnp.where(lane == INFO_RANK_LANE, rank, comb))


def _moe_dispatch_kernel(pos_ref, x3_ref, info_ref, xs_in_ref, infos_in_ref, xs_ref, infos_ref, sem):
    del xs_in_ref, infos_in_ref
    tm = x3_ref.shape[0]

    def issue(t, c):
        p = pos_ref[0, t]
        pltpu.make_async_copy(x3_ref.at[pl.ds(t, 1)], xs_ref.at[pl.ds(p, 1)], sem).start()
        pltpu.make_async_copy(info_ref.at[pl.ds(t, 1)], infos_ref.at[pl.ds(p, 1)], sem).start()
        return c

    lax.fori_loop(0, tm, issue, 0, unroll=8)
    pltpu.make_async_copy(x3_ref, xs_ref.at[pl.ds(0, tm)], sem).wait()
    pltpu.make_async_copy(info_ref, infos_ref.at[pl.ds(0, tm)], sem).wait()


def _moe_expert_kernel(tg_ref, nv_ref, used_ref, wblk_ref, xs_ref, infos_ref, wg_ref, wu_ref, wd_ref, y_ref,
                       x_scr, acc_scr):
    del wblk_ref
    j, e = pl.program_id(0), pl.program_id(1)
    last = pl.num_programs(1) - 1
    valid = j < nv_ref[0]

    @pl.when(valid & (e == 0))
    def _first():
        acc_scr[...] = jnp.zeros_like(acc_scr)
        x_scr[...] = _from_slabs(xs_ref).astype(BF16)

    @pl.when(used_ref[j * MOE_PER_GROUP + e] == 1)
    def _live():
        x = x_scr[...]
        info = infos_ref[...]
        lane = lax.broadcasted_iota(jnp.int32, info.shape, 1)
        eid = tg_ref[j] * MOE_PER_GROUP + e + MOE_GROUPS
        we = jnp.sum(jnp.where(lane == eid, info, 0.0), axis=1, keepdims=True)
        gate = jnp.dot(x, wg_ref[0], preferred_element_type=F32)
        up = jnp.dot(x, wu_ref[0], preferred_element_type=F32)
        act = gate * jax.nn.sigmoid(gate) * up * we
        acc_scr[...] += jnp.dot(act.astype(BF16), wd_ref[0], preferred_element_type=F32)

    @pl.when(valid & (e == last))
    def _out():
        _to_slabs(y_ref, acc_scr[...])

    @pl.when(jnp.logical_not(valid) & (e == last))
    def _dead():
        y_ref[...] = jnp.zeros_like(y_ref)


def _moe_sparse(h, g, w_group, b_group, w_expert, b_expert, w_gate, w_up, w_down, tm=1024, tmd=512):
    s, d = h.shape
    assert d == SUBLANES * LANES
    tm, tmd, tb = min(tm, s), min(tmd, s), min(MOE_ROW_TILE, s)
    ff = w_gate.shape[2]
    nr = MOE_GROUPS + MOE_EXPERTS
    wr = jnp.pad(jnp.concatenate([w_group, w_expert], axis=1), ((0, 0), (0, LANES - nr)))
    br = jnp.pad(jnp.concatenate([b_group, b_expert]), (0, LANES - nr)).reshape(1, LANES)
    tri = (lax.broadcasted_iota(jnp.int32, (tm, tm), 0) > lax.broadcasted_iota(jnp.int32, (tm, tm), 1)).astype(BF16)
    const = lambda a: pl.BlockSpec(a.shape, lambda i: (0, 0))
    slab = lambda rows: pl.BlockSpec((rows, SUBLANES, LANES), lambda i: (i, 0, 0))
    x3, info, cnt = pl.pallas_call(
        _moe_route_kernel,
        out_shape=(jax.ShapeDtypeStruct((s, SUBLANES, LANES), F32), jax.ShapeDtypeStruct((s, LANES), F32),
                   jax.ShapeDtypeStruct((1, LANES), F32)),
        grid=(s // tm,),
        in_specs=[pl.BlockSpec((tm, d), lambda i: (i, 0)), const(g), const(wr), const(br), const(tri)],
        out_specs=(slab(tm), pl.BlockSpec((tm, LANES), lambda i: (i, 0)), pl.BlockSpec((1, LANES), lambda i: (0, 0))),
        scratch_shapes=[pltpu.VMEM((1, LANES), F32)],
        compiler_params=_cparams(("arbitrary",)),
        name="moe_route",
    )(h, g, wr, br, tri)

    ncat = MOE_GROUPS * MOE_PAIRS
    counts = cnt[0, :ncat].astype(jnp.int32).reshape(MOE_GROUPS, MOE_PAIRS)
    gcount = jnp.sum(counts, axis=1)
    padded = ((gcount + tb - 1) // tb) * tb
    gend = jnp.cumsum(padded)
    gstart = gend - padded
    cstart = (gstart[:, None] + jnp.cumsum(counts, axis=1) - counts).reshape(-1)
    cend = cstart + counts.reshape(-1)
    tok_cat = info[:, INFO_CAT_LANE].astype(jnp.int32)
    tok_rank = info[:, INFO_RANK_LANE].astype(jnp.int32)
    pos = cstart[tok_cat] + tok_rank
    n_tiles = s // tb + MOE_GROUPS
    p_rows = n_tiles * tb
    tile_lo = jnp.arange(n_tiles) * tb
    tile_group = jnp.minimum(jnp.sum(tile_lo[:, None] >= gend[None, :], axis=1), MOE_GROUPS - 1)
    n_valid = (gend[-1] // tb).reshape(1)
    overlap = (cstart[None, :] < tile_lo[:, None] + tb) & (cend[None, :] > tile_lo[:, None])
    member = jnp.array([[e in PAIR_EXPERTS[c % MOE_PAIRS] for e in range(MOE_PER_GROUP)] for c in range(ncat)])
    used = jnp.any(overlap[:, :, None] & member[None], axis=1).reshape(-1)
    want = (tile_group[:, None] * MOE_PER_GROUP + jnp.arange(MOE_PER_GROUP)[None, :]).reshape(-1)
    step = jnp.arange(used.shape[0])
    latest = lax.cummax(jnp.where(used, step, -1))
    wblock = want[jnp.where(latest >= 0, latest, jnp.argmax(used))]

    anyspec = pl.BlockSpec(memory_space=pl.ANY)
    xs, infos = pl.pallas_call(
        _moe_dispatch_kernel,
        out_shape=(jax.ShapeDtypeStruct((p_rows, SUBLANES, LANES), F32), jax.ShapeDtypeStruct((p_rows, LANES), F32)),
        grid=(s // tmd,),
        in_specs=[pl.BlockSpec((None, 1, tmd), lambda i: (i, 0, 0), memory_space=pltpu.SMEM),
                  slab(tmd), pl.BlockSpec((tmd, LANES), lambda i: (i, 0)), anyspec, anyspec],
        out_specs=(anyspec, anyspec),
        scratch_shapes=[pltpu.SemaphoreType.DMA],
        input_output_aliases={3: 0, 4: 1},
        compiler_params=_cparams(("arbitrary",)),
        name="moe_dispatch",
    )(pos.reshape(s // tmd, 1, tmd), x3, info, jnp.zeros((p_rows, SUBLANES, LANES), F32),
      jnp.zeros((p_rows, LANES), F32))

    wg, wu, wd = w_gate.astype(BF16), w_up.astype(BF16), w_down.astype(BF16)
    wsel = lambda j, e, tg, nv, us, wb: (wb[j * MOE_PER_GROUP + e], 0, 0)
    y = pl.pallas_call(
        _moe_expert_kernel,
        out_shape=jax.ShapeDtypeStruct((p_rows, SUBLANES, LANES), F32),
        grid_spec=pltpu.PrefetchScalarGridSpec(
            num_scalar_prefetch=4,
            grid=(n_tiles, MOE_PER_GROUP),
            in_specs=[pl.BlockSpec((tb, SUBLANES, LANES), lambda j, e, tg, nv, us, wb: (j, 0, 0)),
                      pl.BlockSpec((tb, LANES), lambda j, e, tg, nv, us, wb: (j, 0)),
                      pl.BlockSpec((1, d, ff), wsel), pl.BlockSpec((1, d, ff), wsel), pl.BlockSpec((1, ff, d), wsel)],
            out_specs=pl.BlockSpec((tb, SUBLANES, LANES), lambda j, e, tg, nv, us, wb: (j, 0, 0)),
            scratch_shapes=[pltpu.VMEM((tb, d), BF16), pltpu.VMEM((tb, d), F32)]),
        compiler_params=_cparams(("arbitrary", "arbitrary")),
        name="moe_expert",
    )(tile_group.astype(jnp.int32), n_valid.astype(jnp.int32), used.astype(jnp.int32), wblock.astype(jnp.int32),
      xs, infos, wg, wu, wd)
    return y, pos


def _ple_kernel(pos_ref, posn_ref, h_ref, y_hbm, g_ref, wg_ref, p_ref, wp_ref, gf_ref, out_ref, ybuf, sem, *, final):
    tm = h_ref.shape[0]
    i, n = pl.program_id(0), pl.num_programs(0)
    slot = lax.rem(i, 2)

    def gather(idx_ref, dst_slot):
        def issue(t, c):
            pltpu.make_async_copy(y_hbm.at[pl.ds(idx_ref[0, t], 1)], ybuf.at[dst_slot, pl.ds(t, 1)],
                                  sem.at[dst_slot]).start()
            return c
        lax.fori_loop(0, tm, issue, 0, unroll=8)

    @pl.when(i == 0)
    def _first():
        gather(pos_ref, slot)

    @pl.when(i + 1 < n)
    def _next():
        gather(posn_ref, 1 - slot)

    pltpu.make_async_copy(y_hbm.at[pl.ds(0, tm)], ybuf.at[slot], sem.at[slot]).wait()
    x = h_ref[...] + _from_slabs(ybuf.at[slot])
    xn = _rms(x, g_ref[...]).astype(BF16)
    gate = jax.nn.sigmoid(jnp.dot(xn, wg_ref[...], preferred_element_type=F32))
    proj = jnp.dot(p_ref[...].astype(BF16), wp_ref[...], preferred_element_type=F32)
    y = x + gate * proj
    if final:
        y = _rms(y, gf_ref[...])
    out_ref[...] = y


def _ple(h, y_sorted, pos, g, wg, p, wp, gf, final, tm=512):
    s, d = h.shape
    tm = min(tm, s)
    wg, wp = wg.astype(BF16), wp.astype(BF16)
    const = lambda a: pl.BlockSpec(a.shape, lambda i: (0, 0))
    n = s // tm
    pos3 = pos.reshape(n, 1, tm)
    return pl.pallas_call(
        functools.partial(_ple_kernel, final=final),
        out_shape=jax.ShapeDtypeStruct((s, d), F32),
        grid=(n,),
        in_specs=[pl.BlockSpec((None, 1, tm), lambda i: (i, 0, 0), memory_space=pltpu.SMEM),
                  pl.BlockSpec((None, 1, tm), lambda i: (jnp.minimum(i + 1, n - 1), 0, 0), memory_space=pltpu.SMEM),
                  pl.BlockSpec((tm, d), lambda i: (i, 0)), pl.BlockSpec(memory_space=pl.ANY), const(g), const(wg),
                  pl.BlockSpec((tm, p.shape[1]), lambda i: (i, 0)), const(wp), const(gf)],
        out_specs=pl.BlockSpec((tm, d), lambda i: (i, 0)),
        scratch_shapes=[pltpu.VMEM((2, tm, SUBLANES, LANES), F32), pltpu.SemaphoreType.DMA((2,))],
        compiler_params=_cparams(("arbitrary",)),
        name="ple",
    )(pos3, pos3, h, y_sorted, g, wg, p, wp, gf)


PARTIAL_KPERM = ((8, 16), (0, 8), (16, 64))
AXIAL_KPERM = ((16, 32), (0, 16), (48, 64), (32, 48))
B_PAIRS = ((128, 1), (512, 4), (2048, 16))


def _partial_tables(s):
    inv = ROPE_THETA ** (-jnp.arange(0, ROT_DIM, 2, dtype=F32) / ROT_DIM)
    ang = jnp.arange(s).astype(F32)[:, None] * inv[None, :]
    return _rope_tables([ang], s, LOG2E * HEAD_DIM ** -0.5)


def _axial_tables(s):
    half = HEAD_DIM // 2
    inv = AXIAL_THETA ** (-jnp.arange(0, half, 2, dtype=F32) / half)
    t = jnp.arange(s)
    ang_r = (t // GRID_W).astype(F32)[:, None] * inv[None, :]
    ang_c = (t % GRID_W).astype(F32)[:, None] * inv[None, :]
    return _rope_tables([ang_r, ang_c], s, LOG2E * HEAD_DIM ** -0.5)


def _mixer_diff(h, g, w_in, w_out, lam_rows, subln, lam_init, tabs):
    d = h.shape[1]
    heads = d // (2 * HEAD_DIM)
    aw = 2 * heads * HEAD_DIM
    dv = 2 * HEAD_DIM
    q, kt, v = _project(h, g, w_in[:, :aw], w_in[:, aw:2 * aw], w_in[:, 2 * aw:], tabs["row_scaled"], tabs["t"],
                        hr=2 * heads, ht=2 * heads, hv=heads, dv=dv, nv=2 * dv,
                        shift=ROT_DIM // 2, tperm=PARTIAL_KPERM, qk_norm=False)
    o = _flash(q, kt, v, units=heads, q_per_unit=2, k_per_unit=2, v_per_unit=1,
               streams=((0, 0, 0), (1, 1, 0)), split=4, tq=1024, tk=2048, nv=2 * dv, dv=dv, out_w=dv,
               mode="diff", lam=lam_rows, subln=subln, lam_init=lam_init, name="flash_diff")
    return _outproj(h, o, w_out)


def _mixer_dilated(h, g, w_in, w_out, tabs):
    nh, hg = 12, 4
    bw = nh * HEAD_DIM
    q, kt, v = _project(h, g, w_in[:, :bw], w_in[:, bw:2 * bw], w_in[:, 2 * bw:], tabs["row_scaled"], tabs["t"],
                        hr=nh, ht=nh, hv=nh, dv=HEAD_DIM, nv=LANES,
                        shift=ROT_DIM // 2, tperm=PARTIAL_KPERM, qk_norm=False)
    os_, ls_ = [], []
    for gi, (win, dil) in enumerate(B_PAIRS):
        halfw = (win // (2 * dil)) * dil
        o, lse = _flash(q, kt, v, units=1, unit0=gi, q_per_unit=hg, k_per_unit=hg, v_per_unit=hg,
                        streams=tuple((j, j, j) for j in range(hg)), tq=256, tk=max(halfw, 256),
                        nv=LANES, dv=HEAD_DIM, out_w=hg * HEAD_DIM, band=(halfw, dil),
                        halo=-(-halfw // LANES) * LANES, mode="lse",
                        name=f"flash_dilated{gi}")
        os_.append(o)
        ls_.append(lse)
    return _outproj_groups(h, os_, ls_, w_out)


GQA_HEADS, GQA_KV_HEADS = 16, 4


def _mixer_window(h, g, w_in, w_out, sink, tabs):
    qd, kvd = GQA_HEADS * HEAD_DIM, GQA_KV_HEADS * HEAD_DIM
    grp = GQA_HEADS // GQA_KV_HEADS
    q, kt, v = _project(h, g, w_in[:, :qd], w_in[:, qd:qd + kvd], w_in[:, qd + kvd:], tabs["row_scaled"], tabs["t"],
                        hr=GQA_HEADS, ht=GQA_KV_HEADS, hv=GQA_KV_HEADS, dv=HEAD_DIM, nv=LANES,
                        shift=ROT_DIM // 2, tperm=PARTIAL_KPERM, qk_norm=False)
    o = _flash(q, kt, v, units=GQA_KV_HEADS, q_per_unit=grp, k_per_unit=1, v_per_unit=1,
               streams=tuple((j, 0, 0) for j in range(grp)), tq=256, tk=256, nv=LANES, dv=HEAD_DIM,
               out_w=grp * HEAD_DIM, band=(128, 1), halo=128, mode="gqa", sink=sink.reshape(GQA_KV_HEADS, grp),
               name="flash_window")
    return _outproj(h, o, w_out)


def _mixer_axial(h, g, w_in, w_out, q_norm, k_norm, tabs):
    qd, kvd = GQA_HEADS * HEAD_DIM, GQA_KV_HEADS * HEAD_DIM
    grp = GQA_HEADS // GQA_KV_HEADS
    qg = jnp.tile(q_norm, 2).reshape(1, LANES)
    kg = k_norm.reshape(HEAD_DIM, 1)
    q, kt, v = _project(h, g, w_in[:, :qd], w_in[:, qd:qd + kvd], w_in[:, qd + kvd:], tabs["row_scaled"], tabs["t"],
                        hr=GQA_HEADS, ht=GQA_KV_HEADS, hv=GQA_KV_HEADS, dv=HEAD_DIM, nv=LANES,
                        shift=HEAD_DIM // 4, tperm=AXIAL_KPERM, qk_norm=True, rg=qg, tg=kg)
    o = _flash(q, kt, v, units=GQA_KV_HEADS, q_per_unit=grp, k_per_unit=1, v_per_unit=1,
               streams=tuple((j, 0, 0) for j in range(grp)), split=4, tq=1024, tk=2048, nv=LANES, dv=HEAD_DIM,
               out_w=grp * HEAD_DIM, mode="gqa", name="flash_axial")
    return _outproj(h, o, w_out)


def kernel(x, p, norm_mix, norm_ffn, norm_ple, norm_final, a_w_in, a_w_out, a_lam_q1, a_lam_k1, a_lam_q2, a_lam_k2, a_subln, b_w_in, b_w_out, c_w_in, c_w_out, c_sink, d_w_in, d_w_out, d_q_norm, d_k_norm, moe_w_group, moe_b_group, moe_w_expert, moe_b_expert, moe_w_gate, moe_w_up, moe_w_down, ple_w_gate, ple_w_proj):
    bn, s, d = x.shape
    assert bn == 1
    depth = p.shape[0]
    h = x[0]
    ptabs = _partial_tables(s)
    atabs = _axial_tables(s)
    row = lambda a: a.reshape(1, -1)
    for i in range(depth):
        r, kind = divmod(i, 4)
        g = row(norm_mix[i])
        if kind == 0:
            lam_init = 0.8 - 0.6 * math.exp(-0.3 * i)
            lam_rows = jnp.stack([a_lam_q1[r], a_lam_k1[r], a_lam_q2[r], a_lam_k2[r]])
            h = _mixer_diff(h, g, a_w_in[r], a_w_out[r], lam_rows, row(a_subln[r]), lam_init, ptabs)
        elif kind == 1:
            h = _mixer_dilated(h, g, b_w_in[r], b_w_out[r], ptabs)
        elif kind == 2:
            h = _mixer_window(h, g, c_w_in[r], c_w_out[r], c_sink[r], ptabs)
        else:
            h = _mixer_axial(h, g, d_w_in[r], d_w_out[r], d_q_norm[r], d_k_norm[r], atabs)
        y_sorted, pos = _moe_sparse(h, row(norm_ffn[i]), moe_w_group[i], moe_b_group[i], moe_w_expert[i],
                                    moe_b_expert[i], moe_w_gate[i], moe_w_up[i], moe_w_down[i])
        h = _ple(h, y_sorted, pos, row(norm_ple[i]), ple_w_gate[i], p[i, 0], ple_w_proj[i], row(norm_final), final=(i == depth - 1))
    return h[None]
```

```python
import functools
import math

import jax
import jax.numpy as jnp
from jax import lax
from jax.experimental import pallas as pl
from jax.experimental.pallas import tpu as pltpu

F32 = jnp.float32
BF16 = jnp.bfloat16

HEAD_DIM = 64
LANES = 128
EPS = 1e-6
LOG2E = 1.4426950408889634
NEG_INF = -1e30
M_FLOOR = -1e29
ROPE_THETA = 500000.0
ROT_DIM = HEAD_DIM // 4
AXIAL_THETA = 10000.0
GRID_W = 64
MOE_GROUPS = 4
MOE_PER_GROUP = 4
MOE_EXPERTS = 16
VMEM_LIMIT = 56 * 1024 * 1024


def _cparams(sem):
    return pltpu.CompilerParams(dimension_semantics=sem, vmem_limit_bytes=VMEM_LIMIT)


def _rms(x, g):
    return x * lax.rsqrt(jnp.mean(x * x, axis=-1, keepdims=True) + EPS) * g


def _proj_kernel(h_ref, g_ref, wr_ref, wtt_ref, wv_ref, rc_ref, rs1_ref, rs2_ref, tc_ref, ts_ref,
                 rg_ref, tg_ref, r_ref, t_ref, v_ref, *, hr, ht, nv, dv, shift, tperm, qk_norm):
    xn = _rms(h_ref[...], g_ref[...]).astype(BF16)
    rf = jnp.dot(xn, wr_ref[...], preferred_element_type=F32)
    rc, rs1, rs2 = rc_ref[...], rs1_ref[...], rs2_ref[...]
    low = lax.broadcasted_iota(jnp.int32, (rf.shape[0], LANES), 1) < HEAD_DIM
    for j in range(hr // 2):
        s = rf[:, j * LANES:(j + 1) * LANES]
        if qk_norm:
            sq = s * s
            ms = jnp.where(low, jnp.sum(jnp.where(low, sq, 0.0), axis=-1, keepdims=True),
                           jnp.sum(jnp.where(low, 0.0, sq), axis=-1, keepdims=True)) * (1.0 / HEAD_DIM)
            s = s * lax.rsqrt(ms + EPS) * rg_ref[...]
        r = s * rc + pltpu.roll(s, LANES - shift, 1) * rs1 + pltpu.roll(s, shift, 1) * rs2
        r_ref[2 * j] = r[:, :HEAD_DIM].astype(BF16)
        r_ref[2 * j + 1] = r[:, HEAD_DIM:].astype(BF16)
    nt = (((1,), (1,)), ((), ()))
    tf = lax.dot_general(wtt_ref[...], xn, nt, preferred_element_type=F32)
    tc, ts = tc_ref[...], ts_ref[...]
    for h in range(ht):
        s = tf[h * HEAD_DIM:(h + 1) * HEAD_DIM, :]
        if qk_norm:
            ms = jnp.sum(s * s, axis=0, keepdims=True) * (1.0 / HEAD_DIM)
            s = s * lax.rsqrt(ms + EPS) * tg_ref[...]
        partner = jnp.concatenate([s[a:b] for a, b in tperm], axis=0)
        t_ref[h] = (s * tc + partner * ts).astype(BF16)
    vf = jnp.dot(xn, wv_ref[...], preferred_element_type=F32)
    lane = lax.broadcasted_iota(jnp.int32, vf.shape, 1)
    v_ref[...] = jnp.where((lane & (nv - 1)) == dv, 1.0, vf).astype(BF16)


def _project(h, g, wq, wk, wv, row_tabs, t_tabs, *, hr, ht, hv, dv, nv, shift, tperm, qk_norm, rg=None, tg=None,
             tm=512):
    s, d = h.shape
    tm = min(tm, s)
    assert hr % 2 == 0
    wr = wq.astype(BF16)
    wtt = wk.T.astype(BF16)
    wv_p = jnp.pad(wv.reshape(d, hv, dv), ((0, 0), (0, 0), (0, nv - dv))).reshape(d, hv * nv).astype(BF16)
    rc, rs1, rs2 = row_tabs
    tc, ts = t_tabs
    if rg is None:
        rg = jnp.ones((1, LANES), F32)
        tg = jnp.ones((HEAD_DIM, 1), F32)
    full = lambda a: pl.BlockSpec(a.shape, lambda i: (0,) * a.ndim)
    kern = functools.partial(_proj_kernel, hr=hr, ht=ht, nv=nv, dv=dv, shift=shift, tperm=tperm, qk_norm=qk_norm)
    return pl.pallas_call(
        kern,
        out_shape=(jax.ShapeDtypeStruct((hr, s, HEAD_DIM), BF16),
                   jax.ShapeDtypeStruct((ht, HEAD_DIM, s), BF16),
                   jax.ShapeDtypeStruct((s, hv * nv), BF16)),
        grid=(s // tm,),
        in_specs=[pl.BlockSpec((tm, d), lambda i: (i, 0)), full(g), full(wr), full(wtt), full(wv_p),
                  pl.BlockSpec((tm, LANES), lambda i: (i, 0)), pl.BlockSpec((tm, LANES), lambda i: (i, 0)),
                  pl.BlockSpec((tm, LANES), lambda i: (i, 0)),
                  pl.BlockSpec((HEAD_DIM, tm), lambda i: (0, i)), pl.BlockSpec((HEAD_DIM, tm), lambda i: (0, i)),
                  full(rg), full(tg)],
        out_specs=(pl.BlockSpec((hr, tm, HEAD_DIM), lambda i: (0, i, 0)),
                   pl.BlockSpec((ht, HEAD_DIM, tm), lambda i: (0, 0, i)),
                   pl.BlockSpec((tm, hv * nv), lambda i: (i, 0))),
        compiler_params=_cparams(("parallel",)),
        name="proj",
    )(h, g, wr, wtt, wv_p, rc, rs1, rs2, tc, ts, rg, tg)


def _rope_tables(ang_list, s, scale):
    cs, sn1, sn2, ksn = [], [], [], []
    used = 0
    for ang in ang_list:
        c, sn = jnp.cos(ang), jnp.sin(ang)
        z = jnp.zeros_like(sn)
        cs += [c, c]
        sn1 += [-sn, z]
        sn2 += [z, sn]
        ksn += [-sn, sn]
        used += 2 * ang.shape[1]
    rest = HEAD_DIM - used
    ones, zeros = jnp.ones((s, rest), F32), jnp.zeros((s, rest), F32)
    c64 = jnp.concatenate(cs + [ones], axis=1)
    twice = lambda parts: jnp.concatenate(parts + parts, axis=1)
    row = (twice([c64]), twice(sn1 + [zeros]), twice(sn2 + [zeros]))
    tr = (c64.T, jnp.concatenate(ksn + [zeros], axis=1).T)
    return {"row_scaled": tuple(t * scale for t in row), "t": tr}


def _lane_tile(x, width):
    reps = width // LANES
    return x if reps == 1 else jnp.concatenate([x] * reps, axis=1)


def _flash_kernel(*refs, streams, split, tq, tk, nv, dv, band, halo, seq, mode, lam_init, has_sink):
    it = iter(refs)
    q_ref, kt_ref, v_ref = next(it), next(it), next(it)
    sink_ref = next(it) if has_sink else None
    lam_ref = subln_ref = None
    if mode == "diff":
        lam_ref, subln_ref = next(it), next(it)
    o_ref = next(it)
    lse_ref = next(it) if mode == "lse" else None
    m_scr, acc_scr = next(it), next(it)

    qb, kb = pl.program_id(1), pl.program_id(2)
    nsteps = pl.num_programs(2)
    cr = tq // split

    @pl.when(kb == 0)
    def _init():
        for si, (qi, _, _) in enumerate(streams):
            if has_sink:
                sink2 = jnp.full((tq, LANES), LOG2E, F32) * sink_ref[pl.program_id(0), qi]
                m0 = jnp.maximum(sink2, M_FLOOR)
                m_scr[si] = m0
                lane = lax.broadcasted_iota(jnp.int32, (tq, nv), 1)
                acc_scr[si] = jnp.where(lane == dv, _lane_tile(jnp.exp2(sink2 - m0), nv), 0.0)
            else:
                m_scr[si] = jnp.full((tq, LANES), M_FLOOR, F32)
                acc_scr[si] = jnp.zeros((tq, nv), F32)

    def _step():
        chunks = [(si, c * cr, qi, ki, vi) for si, (qi, ki, vi) in enumerate(streams) for c in range(split)]
        scores = [jnp.dot(q_ref[qi, r0:r0 + cr, :], kt_ref[ki], preferred_element_type=F32)
                  for _, r0, qi, ki, _ in chunks]
        for sc, (si, r0, _, _, vi) in zip(scores, chunks):
            if band is not None:
                halfw, dil, off = band
                row = lax.broadcasted_iota(jnp.int32, (cr, tk), 0) + r0
                col = lax.broadcasted_iota(jnp.int32, (cr, tk), 1)
                diff = col - row + kshift
                valid = jnp.abs(diff) <= halfw
                if dil > 1:
                    valid = valid & ((diff & (dil - 1)) == 0)
                sc = jnp.where(valid, sc, NEG_INF)
            m_old = m_scr[si, r0:r0 + cr, :]
            m_new = jnp.maximum(m_old, jnp.broadcast_to(jnp.max(sc, axis=1, keepdims=True), (cr, LANES)))
            p = jnp.exp2(sc - _lane_tile(m_new, tk))
            alpha = jnp.exp2(m_old - m_new)
            pv = jnp.dot(p.astype(BF16), v_ref[:, vi * nv:(vi + 1) * nv], preferred_element_type=F32)
            acc_scr[si, r0:r0 + cr, :] = acc_scr[si, r0:r0 + cr, :] * _lane_tile(alpha, nv) + pv
            m_scr[si, r0:r0 + cr, :] = m_new

    if band is None:
        _step()
    elif halo is not None:
        kshift = jnp.clip(qb * tq - halo, 0, seq - tk) - qb * tq
        _step()
    else:
        _, _, off = band
        kshift = (kb - off) * tk
        kabs = qb * (tq // tk) + kb - off
        pl.when((kabs >= 0) & (kabs < seq // tk))(_step)

    @pl.when(kb == nsteps - 1)
    def _fin():
        if mode == "diff":
            a0, a1 = acc_scr[0], acc_scr[1]
            lam_rows = lam_ref[...]
            lam = (jnp.exp(jnp.sum(lam_rows[0:1] * lam_rows[1:2], axis=1, keepdims=True))
                   - jnp.exp(jnp.sum(lam_rows[2:3] * lam_rows[3:4], axis=1, keepdims=True)) + lam_init)
            o = a0[:, :dv] / a0[:, dv:dv + 1] - lam * (a1[:, :dv] / a1[:, dv:dv + 1])
            o = _rms(o, subln_ref[...]) * (1.0 - lam_init)
            o_ref[...] = o.astype(o_ref.dtype)
        else:
            for si in range(len(streams)):
                a = acc_scr[si]
                l = a[:, dv:dv + 1]
                o_ref[:, si * dv:(si + 1) * dv] = (a[:, :dv] / l).astype(o_ref.dtype)
                if mode == "lse":
                    lse_ref[:, si * dv:(si + 1) * dv] = m_scr[si][:, :dv] + jnp.log2(l)


def _flash(q, kt, v, *, units, q_per_unit, k_per_unit, v_per_unit, streams, tq, tk, nv, dv, out_w,
           split=1, unit0=0, band=None, halo=None, mode="gqa", sink=None, lam=None, subln=None, lam_init=0.0,
           name="flash"):
    s = q.shape[1]
    tq, tk = min(tq, s), min(tk, s)
    if halo is not None and tq + 2 * halo > s:
        halo, tq = None, max(tq, tk)
    if halo is not None:
        tk = tq + 2 * halo
    assert s % tq == 0 and (halo is not None or (s % tk == 0 and (band is None or tq % tk == 0)))
    n_kblocks = s // tk
    kspec = vspec = None
    if band is None:
        nsteps = n_kblocks
        kmap = lambda u, i, j: (u + unit0, 0, j)
        vmap = lambda u, i, j: (j, u + unit0)
    elif halo is not None:
        assert tq % LANES == 0 and halo % LANES == 0
        nsteps = 1
        band = (band[0], band[1], 0)
        koff = lambda i: jnp.clip(i * (tq // LANES) - halo // LANES, 0, (s - tk) // LANES) * LANES
        kspec = pl.BlockSpec((pl.Element(k_per_unit), pl.Element(HEAD_DIM), pl.Element(tk)),
                             lambda u, i, j: ((u + unit0) * k_per_unit, 0, koff(i)))
        vspec = pl.BlockSpec((pl.Element(tk), pl.Element(v_per_unit * nv)),
                             lambda u, i, j: (koff(i), (u + unit0) * v_per_unit * nv))
    else:
        halfw = band[0]
        off = -(-halfw // tk)
        nsteps = tq // tk + 2 * off
        band = (band[0], band[1], off)
        kidx = lambda i, j: jnp.clip(i * (tq // tk) + j - off, 0, n_kblocks - 1)
        kmap = lambda u, i, j: (u + unit0, 0, kidx(i, j))
        vmap = lambda u, i, j: (kidx(i, j), u + unit0)
    in_specs = [pl.BlockSpec((q_per_unit, tq, HEAD_DIM), lambda u, i, j: (u + unit0, i, 0)),
                kspec or pl.BlockSpec((k_per_unit, HEAD_DIM, tk), kmap),
                vspec or pl.BlockSpec((tk, v_per_unit * nv), vmap)]
    args = [q, kt, v]
    if sink is not None:
        in_specs.append(pl.BlockSpec(memory_space=pltpu.SMEM))
        args.append(sink)
    if mode == "diff":
        in_specs += [pl.BlockSpec(lam.shape, lambda u, i, j: (0, 0)), pl.BlockSpec(subln.shape, lambda u, i, j: (0, 0))]
        args += [lam, subln]
    out_shape = [jax.ShapeDtypeStruct((s, units * out_w), BF16)]
    out_specs = [pl.BlockSpec((tq, out_w), lambda u, i, j: (i, u))]
    if mode == "lse":
        out_shape.append(jax.ShapeDtypeStruct((s, units * out_w), F32))
        out_specs.append(pl.BlockSpec((tq, out_w), lambda u, i, j: (i, u)))
    assert tq % split == 0
    kern = functools.partial(_flash_kernel, streams=streams, split=split, tq=tq, tk=tk, nv=nv, dv=dv, band=band,
                             halo=halo, seq=s, mode=mode, lam_init=lam_init, has_sink=sink is not None)
    res = pl.pallas_call(
        kern,
        out_shape=tuple(out_shape),
        grid=(units, s // tq, nsteps),
        in_specs=in_specs,
        out_specs=tuple(out_specs),
        scratch_shapes=[pltpu.VMEM((len(streams), tq, LANES), F32), pltpu.VMEM((len(streams), tq, nv), F32)],
        compiler_params=_cparams(("parallel", "parallel", "arbitrary")),
        name=name,
    )(*args)
    return res if mode == "lse" else res[0]


def _outproj_kernel(h_ref, o_ref, w_ref, out_ref):
    out_ref[...] = h_ref[...] + jnp.dot(o_ref[...], w_ref[...], preferred_element_type=F32)


def _outproj(h, o, w, tm=512):
    s, d = h.shape
    tm = min(tm, s)
    w = w.astype(BF16)
    return pl.pallas_call(
        _outproj_kernel,
        out_shape=jax.ShapeDtypeStruct((s, d), F32),
        grid=(s // tm,),
        in_specs=[pl.BlockSpec((tm, d), lambda i: (i, 0)), pl.BlockSpec((tm, o.shape[1]), lambda i: (i, 0)),
                  pl.BlockSpec(w.shape, lambda i: (0, 0))],
        out_specs=pl.BlockSpec((tm, d), lambda i: (i, 0)),
        compiler_params=_cparams(("parallel",)),
        name="outproj",
    )(h, o, w)


def _outproj_groups_kernel(h_ref, o0_ref, o1_ref, o2_ref, l0_ref, l1_ref, l2_ref, w_ref, out_ref):
    l0, l1, l2 = l0_ref[...], l1_ref[...], l2_ref[...]
    mx = jnp.maximum(jnp.maximum(l0, l1), l2)
    e0, e1, e2 = jnp.exp2(l0 - mx), jnp.exp2(l1 - mx), jnp.exp2(l2 - mx)
    tot = e0 + e1 + e2
    acc = h_ref[...]
    gw = o0_ref.shape[1]
    for g, (o_ref, e) in enumerate(((o0_ref, e0), (o1_ref, e1), (o2_ref, e2))):
        og = (o_ref[...].astype(F32) * (e / tot)).astype(BF16)
        acc = acc + jnp.dot(og, w_ref[g * gw:(g + 1) * gw, :], preferred_element_type=F32)
    out_ref[...] = acc


def _outproj_groups(h, os_, ls_, w, tm=512):
    s, d = h.shape
    tm = min(tm, s)
    w = w.astype(BF16)
    gw = os_[0].shape[1]
    row = lambda width: pl.BlockSpec((tm, width), lambda i: (i, 0))
    return pl.pallas_call(
        _outproj_groups_kernel,
        out_shape=jax.ShapeDtypeStruct((s, d), F32),
        grid=(s // tm,),
        in_specs=[row(d)] + [row(gw)] * 6 + [pl.BlockSpec(w.shape, lambda i: (0, 0))],
        out_specs=row(d),
        compiler_params=_cparams(("parallel",)),
        name="outproj_groups",
    )(h, *os_, *ls_, w)


def _route(logits):
    lane = lax.broadcasted_iota(jnp.int32, logits.shape, 1).astype(F32)
    big = 1e6
    gl = jnp.where(lane < MOE_GROUPS, logits, NEG_INF)
    gmax = jnp.max(gl, axis=1, keepdims=True)
    gidx = jnp.min(jnp.where(gl == gmax, lane, big), axis=1, keepdims=True)
    gw = 1.0 / jnp.sum(jnp.exp(gl - gmax), axis=1, keepdims=True)
    lo = MOE_GROUPS + gidx * MOE_PER_GROUP
    el = jnp.where((lane >= lo) & (lane < lo + MOE_PER_GROUP), logits, NEG_INF)
    v1 = jnp.max(el, axis=1, keepdims=True)
    i1 = jnp.min(jnp.where(el == v1, lane, big), axis=1, keepdims=True)
    el2 = jnp.where(lane == i1, NEG_INF, el)
    v2 = jnp.max(el2, axis=1, keepdims=True)
    i2 = jnp.min(jnp.where(el2 == v2, lane, big), axis=1, keepdims=True)
    e2 = jnp.exp(v2 - v1)
    w1 = gw / (1.0 + e2)
    w2 = w1 * e2
    return jnp.where(lane == i1, w1, 0.0) + jnp.where(lane == i2, w2, 0.0), gidx


MOE_ROW_TILE = 512
INFO_GROUP_LANE = 0
INFO_RANK_LANE = 1
SUBLANES = 8


def _to_slabs(ref, x):
    for c in range(SUBLANES):
        ref[:, c, :] = x[:, c * LANES:(c + 1) * LANES]


def _from_slabs(ref):
    return jnp.concatenate([ref[:, c, :] for c in range(SUBLANES)], axis=1)


def _moe_route_kernel(h_ref, g_ref, wr_ref, br_ref, tri_ref, x3_ref, info_ref, cnt_ref, run_scr):
    @pl.when(pl.program_id(0) == 0)
    def _init():
        run_scr[...] = jnp.zeros_like(run_scr)

    xn = _rms(h_ref[...], g_ref[...])
    logits = jnp.dot(xn, wr_ref[...], preferred_element_type=F32, precision=lax.Precision.HIGHEST) + br_ref[...]
    comb, gidx = _route(logits)
    lane = lax.broadcasted_iota(jnp.int32, comb.shape, 1).astype(F32)
    onehot = lane == gidx
    before = jnp.dot(tri_ref[...], onehot.astype(BF16), preferred_element_type=F32)
    run = run_scr[...]
    rank = jnp.sum(jnp.where(onehot, before + run, 0.0), axis=1, keepdims=True)
    run = run + jnp.sum(onehot.astype(F32), axis=0, keepdims=True)
    run_scr[...] = run
    cnt_ref[...] = run
    _to_slabs(x3_ref, xn)
    info_ref[...] = jnp.where(lane == INFO_GROUP_LANE, gidx, jnp.where(lane == INFO_RANK_LANE, rank, comb))


def _moe_dispatch_kernel(pos_ref, x3_ref, info_ref, xs_in_ref, infos_in_ref, xs_ref, infos_ref, sem):
    del xs_in_ref, infos_in_ref
    tm = x3_ref.shape[0]

    def issue(t, c):
        p = pos_ref[0, t]
        pltpu.make_async_copy(x3_ref.at[pl.ds(t, 1)], xs_ref.at[pl.ds(p, 1)], sem).start()
        pltpu.make_async_copy(info_ref.at[pl.ds(t, 1)], infos_ref.at[pl.ds(p, 1)], sem).start()
        return c

    lax.fori_loop(0, tm, issue, 0, unroll=8)
    pltpu.make_async_copy(x3_ref, xs_ref.at[pl.ds(0, tm)], sem).wait()
    pltpu.make_async_copy(info_ref, infos_ref.at[pl.ds(0, tm)], sem).wait()


def _moe_expert_kernel(tg_ref, nv_ref, xs_ref, infos_ref, wg_ref, wu_ref, wd_ref, y_ref):
    j = pl.program_id(0)

    @pl.when(j < nv_ref[0])
    def _live():
        x = _from_slabs(xs_ref).astype(BF16)
        info = infos_ref[...]
        lane = lax.broadcasted_iota(jnp.int32, info.shape, 1)
        first = tg_ref[j] * MOE_PER_GROUP + MOE_GROUPS
        acc = None
        for e in range(MOE_PER_GROUP):
            we = jnp.sum(jnp.where(lane == first + e, info, 0.0), axis=1, keepdims=True)
            gate = jnp.dot(x, wg_ref[e], preferred_element_type=F32)
            up = jnp.dot(x, wu_ref[e], preferred_element_type=F32)
            act = gate * jax.nn.sigmoid(gate) * up * we
            part = jnp.dot(act.astype(BF16), wd_ref[e], preferred_element_type=F32)
            acc = part if acc is None else acc + part
        _to_slabs(y_ref, acc)

    @pl.when(j >= nv_ref[0])
    def _dead():
        y_ref[...] = jnp.zeros_like(y_ref)


def _moe_sparse(h, g, w_group, b_group, w_expert, b_expert, w_gate, w_up, w_down, tm=1024, tmd=512):
    s, d = h.shape
    assert d == SUBLANES * LANES
    tm, tmd, tb = min(tm, s), min(tmd, s), min(MOE_ROW_TILE, s)
    ff = w_gate.shape[2]
    nr = MOE_GROUPS + MOE_EXPERTS
    wr = jnp.pad(jnp.concatenate([w_group, w_expert], axis=1), ((0, 0), (0, LANES - nr)))
    br = jnp.pad(jnp.concatenate([b_group, b_expert]), (0, LANES - nr)).reshape(1, LANES)
    tri = (lax.broadcasted_iota(jnp.int32, (tm, tm), 0) > lax.broadcasted_iota(jnp.int32, (tm, tm), 1)).astype(BF16)
    const = lambda a: pl.BlockSpec(a.shape, lambda i: (0, 0))
    slab = lambda rows: pl.BlockSpec((rows, SUBLANES, LANES), lambda i: (i, 0, 0))
    x3, info, cnt = pl.pallas_call(
        _moe_route_kernel,
        out_shape=(jax.ShapeDtypeStruct((s, SUBLANES, LANES), F32), jax.ShapeDtypeStruct((s, LANES), F32),
                   jax.ShapeDtypeStruct((1, LANES), F32)),
        grid=(s // tm,),
        in_specs=[pl.BlockSpec((tm, d), lambda i: (i, 0)), const(g), const(wr), const(br), const(tri)],
        out_specs=(slab(tm), pl.BlockSpec((tm, LANES), lambda i: (i, 0)), pl.BlockSpec((1, LANES), lambda i: (0, 0))),
        scratch_shapes=[pltpu.VMEM((1, LANES), F32)],
        compiler_params=_cparams(("arbitrary",)),
        name="moe_route",
    )(h, g, wr, br, tri)

    counts = cnt[0, :MOE_GROUPS].astype(jnp.int32)
    padded = ((counts + tb - 1) // tb) * tb
    ends = jnp.cumsum(padded)
    starts = ends - padded
    tok_group = info[:, INFO_GROUP_LANE].astype(jnp.int32)
    tok_rank = info[:, INFO_RANK_LANE].astype(jnp.int32)
    pos = starts[tok_group] + tok_rank
    n_tiles = s // tb + MOE_GROUPS
    p_rows = n_tiles * tb
    tile_group = jnp.minimum(jnp.sum((jnp.arange(n_tiles) * tb)[:, None] >= ends[None, :], axis=1), MOE_GROUPS - 1)
    n_valid = (ends[-1] // tb).reshape(1)

    anyspec = pl.BlockSpec(memory_space=pl.ANY)
    xs, infos = pl.pallas_call(
        _moe_dispatch_kernel,
        out_shape=(jax.ShapeDtypeStruct((p_rows, SUBLANES, LANES), F32), jax.ShapeDtypeStruct((p_rows, LANES), F32)),
        grid=(s // tmd,),
        in_specs=[pl.BlockSpec((None, 1, tmd), lambda i: (i, 0, 0), memory_space=pltpu.SMEM),
                  slab(tmd), pl.BlockSpec((tmd, LANES), lambda i: (i, 0)), anyspec, anyspec],
        out_specs=(anyspec, anyspec),
        scratch_shapes=[pltpu.SemaphoreType.DMA],
        input_output_aliases={3: 0, 4: 1},
        compiler_params=_cparams(("arbitrary",)),
        name="moe_dispatch",
    )(pos.reshape(s // tmd, 1, tmd), x3, info, jnp.zeros((p_rows, SUBLANES, LANES), F32),
      jnp.zeros((p_rows, LANES), F32))

    wg, wu, wd = w_gate.astype(BF16), w_up.astype(BF16), w_down.astype(BF16)
    wsel = lambda j, tg, nv: (jnp.where(j < nv[0], tg[j], MOE_GROUPS - 1), 0, 0)
    y = pl.pallas_call(
        _moe_expert_kernel,
        out_shape=jax.ShapeDtypeStruct((p_rows, SUBLANES, LANES), F32),
        grid_spec=pltpu.PrefetchScalarGridSpec(
            num_scalar_prefetch=2,
            grid=(n_tiles,),
            in_specs=[pl.BlockSpec((tb, SUBLANES, LANES), lambda j, tg, nv: (j, 0, 0)),
                      pl.BlockSpec((tb, LANES), lambda j, tg, nv: (j, 0)),
                      pl.BlockSpec((MOE_PER_GROUP, d, ff), wsel), pl.BlockSpec((MOE_PER_GROUP, d, ff), wsel),
                      pl.BlockSpec((MOE_PER_GROUP, ff, d), wsel)],
            out_specs=pl.BlockSpec((tb, SUBLANES, LANES), lambda j, tg, nv: (j, 0, 0))),
        compiler_params=_cparams(("arbitrary",)),
        name="moe_expert",
    )(tile_group.astype(jnp.int32), n_valid.astype(jnp.int32), xs, infos, wg, wu, wd)
    return y, pos


def _ple_kernel(pos_ref, posn_ref, h_ref, y_hbm, g_ref, wg_ref, p_ref, wp_ref, gf_ref, out_ref, ybuf, sem, *, final):
    tm = h_ref.shape[0]
    i, n = pl.program_id(0), pl.num_programs(0)
    slot = lax.rem(i, 2)

    def gather(idx_ref, dst_slot):
        def issue(t, c):
            pltpu.make_async_copy(y_hbm.at[pl.ds(idx_ref[0, t], 1)], ybuf.at[dst_slot, pl.ds(t, 1)],
                                  sem.at[dst_slot]).start()
            return c
        lax.fori_loop(0, tm, issue, 0, unroll=8)

    @pl.when(i == 0)
    def _first():
        gather(pos_ref, slot)

    @pl.when(i + 1 < n)
    def _next():
        gather(posn_ref, 1 - slot)

    pltpu.make_async_copy(y_hbm.at[pl.ds(0, tm)], ybuf.at[slot], sem.at[slot]).wait()
    x = h_ref[...] + _from_slabs(ybuf.at[slot])
    xn = _rms(x, g_ref[...]).astype(BF16)
    gate = jax.nn.sigmoid(jnp.dot(xn, wg_ref[...], preferred_element_type=F32))
    proj = jnp.dot(p_ref[...].astype(BF16), wp_ref[...], preferred_element_type=F32)
    y = x + gate * proj
    if final:
        y = _rms(y, gf_ref[...])
    out_ref[...] = y


def _ple(h, y_sorted, pos, g, wg, p, wp, gf, final, tm=512):
    s, d = h.shape
    tm = min(tm, s)
    wg, wp = wg.astype(BF16), wp.astype(BF16)
    const = lambda a: pl.BlockSpec(a.shape, lambda i: (0, 0))
    n = s // tm
    pos3 = pos.reshape(n, 1, tm)
    return pl.pallas_call(
        functools.partial(_ple_kernel, final=final),
        out_shape=jax.ShapeDtypeStruct((s, d), F32),
        grid=(n,),
        in_specs=[pl.BlockSpec((None, 1, tm), lambda i: (i, 0, 0), memory_space=pltpu.SMEM),
                  pl.BlockSpec((None, 1, tm), lambda i: (jnp.minimum(i + 1, n - 1), 0, 0), memory_space=pltpu.SMEM),
                  pl.BlockSpec((tm, d), lambda i: (i, 0)), pl.BlockSpec(memory_space=pl.ANY), const(g), const(wg),
                  pl.BlockSpec((tm, p.shape[1]), lambda i: (i, 0)), const(wp), const(gf)],
        out_specs=pl.BlockSpec((tm, d), lambda i: (i, 0)),
        scratch_shapes=[pltpu.VMEM((2, tm, SUBLANES, LANES), F32), pltpu.SemaphoreType.DMA((2,))],
        compiler_params=_cparams(("arbitrary",)),
        name="ple",
    )(pos3, pos3, h, y_sorted, g, wg, p, wp, gf)


PARTIAL_KPERM = ((8, 16), (0, 8), (16, 64))
AXIAL_KPERM = ((16, 32), (0, 16), (48, 64), (32, 48))
B_PAIRS = ((128, 1), (512, 4), (2048, 16))


def _partial_tables(s):
    inv = ROPE_THETA ** (-jnp.arange(0, ROT_DIM, 2, dtype=F32) / ROT_DIM)
    ang = jnp.arange(s).astype(F32)[:, None] * inv[None, :]
    return _rope_tables([ang], s, LOG2E * HEAD_DIM ** -0.5)


def _axial_tables(s):
    half = HEAD_DIM // 2
    inv = AXIAL_THETA ** (-jnp.arange(0, half, 2, dtype=F32) / half)
    t = jnp.arange(s)
    ang_r = (t // GRID_W).astype(F32)[:, None] * inv[None, :]
    ang_c = (t % GRID_W).astype(F32)[:, None] * inv[None, :]
    return _rope_tables([ang_r, ang_c], s, LOG2E * HEAD_DIM ** -0.5)


def _mixer_diff(h, g, w_in, w_out, lam_rows, subln, lam_init, tabs):
    d = h.shape[1]
    heads = d // (2 * HEAD_DIM)
    aw = 2 * heads * HEAD_DIM
    dv = 2 * HEAD_DIM
    q, kt, v = _project(h, g, w_in[:, :aw], w_in[:, aw:2 * aw], w_in[:, 2 * aw:], tabs["row_scaled"], tabs["t"],
                        hr=2 * heads, ht=2 * heads, hv=heads, dv=dv, nv=2 * dv,
                        shift=ROT_DIM // 2, tperm=PARTIAL_KPERM, qk_norm=False)
    o = _flash(q, kt, v, units=heads, q_per_unit=2, k_per_unit=2, v_per_unit=1,
               streams=((0, 0, 0), (1, 1, 0)), split=4, tq=1024, tk=2048, nv=2 * dv, dv=dv, out_w=dv,
               mode="diff", lam=lam_rows, subln=subln, lam_init=lam_init, name="flash_diff")
    return _outproj(h, o, w_out)


def _mixer_dilated(h, g, w_in, w_out, tabs):
    nh, hg = 12, 4
    bw = nh * HEAD_DIM
    q, kt, v = _project(h, g, w_in[:, :bw], w_in[:, bw:2 * bw], w_in[:, 2 * bw:], tabs["row_scaled"], tabs["t"],
                        hr=nh, ht=nh, hv=nh, dv=HEAD_DIM, nv=LANES,
                        shift=ROT_DIM // 2, tperm=PARTIAL_KPERM, qk_norm=False)
    os_, ls_ = [], []
    for gi, (win, dil) in enumerate(B_PAIRS):
        halfw = (win // (2 * dil)) * dil
        o, lse = _flash(q, kt, v, units=1, unit0=gi, q_per_unit=hg, k_per_unit=hg, v_per_unit=hg,
                        streams=tuple((j, j, j) for j in range(hg)), tq=256, tk=max(halfw, 256),
                        nv=LANES, dv=HEAD_DIM, out_w=hg * HEAD_DIM, band=(halfw, dil),
                        halo=-(-halfw // LANES) * LANES, mode="lse",
                        name=f"flash_dilated{gi}")
        os_.append(o)
        ls_.append(lse)
    return _outproj_groups(h, os_, ls_, w_out)


GQA_HEADS, GQA_KV_HEADS = 16, 4


def _mixer_window(h, g, w_in, w_out, sink, tabs):
    qd, kvd = GQA_HEADS * HEAD_DIM, GQA_KV_HEADS * HEAD_DIM
    grp = GQA_HEADS // GQA_KV_HEADS
    q, kt, v = _project(h, g, w_in[:, :qd], w_in[:, qd:qd + kvd], w_in[:, qd + kvd:], tabs["row_scaled"], tabs["t"],
                        hr=GQA_HEADS, ht=GQA_KV_HEADS, hv=GQA_KV_HEADS, dv=HEAD_DIM, nv=LANES,
                        shift=ROT_DIM // 2, tperm=PARTIAL_KPERM, qk_norm=False)
    o = _flash(q, kt, v, units=GQA_KV_HEADS, q_per_unit=grp, k_per_unit=1, v_per_unit=1,
               streams=tuple((j, 0, 0) for j in range(grp)), tq=256, tk=256, nv=LANES, dv=HEAD_DIM,
               out_w=grp * HEAD_DIM, band=(128, 1), halo=128, mode="gqa", sink=sink.reshape(GQA_KV_HEADS, grp),
               name="flash_window")
    return _outproj(h, o, w_out)


def _mixer_axial(h, g, w_in, w_out, q_norm, k_norm, tabs):
    qd, kvd = GQA_HEADS * HEAD_DIM, GQA_KV_HEADS * HEAD_DIM
    grp = GQA_HEADS // GQA_KV_HEADS
    qg = jnp.tile(q_norm, 2).reshape(1, LANES)
    kg = k_norm.reshape(HEAD_DIM, 1)
    q, kt, v = _project(h, g, w_in[:, :qd], w_in[:, qd:qd + kvd], w_in[:, qd + kvd:], tabs["row_scaled"], tabs["t"],
                        hr=GQA_HEADS, ht=GQA_KV_HEADS, hv=GQA_KV_HEADS, dv=HEAD_DIM, nv=LANES,
                        shift=HEAD_DIM // 4, tperm=AXIAL_KPERM, qk_norm=True, rg=qg, tg=kg)
    o = _flash(q, kt, v, units=GQA_KV_HEADS, q_per_unit=grp, k_per_unit=1, v_per_unit=1,
               streams=tuple((j, 0, 0) for j in range(grp)), split=4, tq=1024, tk=2048, nv=LANES, dv=HEAD_DIM,
               out_w=grp * HEAD_DIM, mode="gqa", name="flash_axial")
    return _outproj(h, o, w_out)


def kernel(x, p, norm_mix, norm_ffn, norm_ple, norm_final, a_w_in, a_w_out, a_lam_q1, a_lam_k1, a_lam_q2, a_lam_k2, a_subln, b_w_in, b_w_out, c_w_in, c_w_out, c_sink, d_w_in, d_w_out, d_q_norm, d_k_norm, moe_w_group, moe_b_group, moe_w_expert, moe_b_expert, moe_w_gate, moe_w_up, moe_w_down, ple_w_gate, ple_w_proj):
    bn, s, d = x.shape
    assert bn == 1
    depth = p.shape[0]
    h = x[0]
    ptabs = _partial_tables(s)
    atabs = _axial_tables(s)
    row = lambda a: a.reshape(1, -1)
    for i in range(depth):
        r, kind = divmod(i, 4)
        g = row(norm_mix[i])
        if kind == 0:
            lam_init = 0.8 - 0.6 * math.exp(-0.3 * i)
            lam_rows = jnp.stack([a_lam_q1[r], a_lam_k1[r], a_lam_q2[r], a_lam_k2[r]])
            h = _mixer_diff(h, g, a_w_in[r], a_w_out[r], lam_rows, row(a_subln[r]), lam_init, ptabs)
        elif kind == 1:
            h = _mixer_dilated(h, g, b_w_in[r], b_w_out[r], ptabs)
        elif kind == 2:
            h = _mixer_window(h, g, c_w_in[r], c_w_out[r], c_sink[r], ptabs)
        else:
            h = _mixer_axial(h, g, d_w_in[r], d_w_out[r], d_q_norm[r], d_k_norm[r], atabs)
        y_sorted, pos = _moe_sparse(h, row(norm_ffn[i]), moe_w_group[i], moe_b_group[i], moe_w_expert[i],
                                    moe_b_expert[i], moe_w_gate[i], moe_w_up[i], moe_w_down[i])
        h = _ple(h, y_sorted, pos, row(norm_ple[i]), ple_w_gate[i], p[i, 0], ple_w_proj[i], row(norm_final), final=(i == depth - 1))
    return h[None]
```

```python
import functools
import math

import jax
import jax.numpy as jnp
from jax import lax
from jax.experimental import pallas as pl
from jax.experimental.pallas import tpu as pltpu

F32 = jnp.float32
BF16 = jnp.bfloat16

HEAD_DIM = 64
LANES = 128
EPS = 1e-6
LOG2E = 1.4426950408889634
NEG_INF = -1e30
M_FLOOR = -1e29
ROPE_THETA = 500000.0
ROT_DIM = HEAD_DIM // 4
AXIAL_THETA = 10000.0
GRID_W = 64
MOE_GROUPS = 4
MOE_PER_GROUP = 4
MOE_EXPERTS = 16
VMEM_LIMIT = 56 * 1024 * 1024


def _cparams(sem):
    return pltpu.CompilerParams(dimension_semantics=sem, vmem_limit_bytes=VMEM_LIMIT)


def _rms(x, g):
    return x * lax.rsqrt(jnp.mean(x * x, axis=-1, keepdims=True) + EPS) * g


def _proj_kernel(h_ref, g_ref, wr_ref, wtt_ref, wv_ref, rc_ref, rs1_ref, rs2_ref, tc_ref, ts_ref,
                 rg_ref, tg_ref, r_ref, t_ref, v_ref, *, hr, ht, hv, nv, dv, shift, tperm, qk_norm):
    xn = _rms(h_ref[...], g_ref[...]).astype(BF16)
    rf = jnp.dot(xn, wr_ref[...], preferred_element_type=F32)
    rc, rs1, rs2 = rc_ref[...], rs1_ref[...], rs2_ref[...]
    low = lax.broadcasted_iota(jnp.int32, (rf.shape[0], LANES), 1) < HEAD_DIM
    for j in range(hr // 2):
        s = rf[:, j * LANES:(j + 1) * LANES]
        if qk_norm:
            sq = s * s
            ms = jnp.where(low, jnp.sum(jnp.where(low, sq, 0.0), axis=-1, keepdims=True),
                           jnp.sum(jnp.where(low, 0.0, sq), axis=-1, keepdims=True)) * (1.0 / HEAD_DIM)
            s = s * lax.rsqrt(ms + EPS) * rg_ref[...]
        r = s * rc + pltpu.roll(s, LANES - shift, 1) * rs1 + pltpu.roll(s, shift, 1) * rs2
        r_ref[2 * j] = r[:, :HEAD_DIM].astype(BF16)
        r_ref[2 * j + 1] = r[:, HEAD_DIM:].astype(BF16)
    nt = (((1,), (1,)), ((), ()))
    tf = lax.dot_general(wtt_ref[...], xn, nt, preferred_element_type=F32)
    tc, ts = tc_ref[...], ts_ref[...]
    for h in range(ht):
        s = tf[h * HEAD_DIM:(h + 1) * HEAD_DIM, :]
        if qk_norm:
            ms = jnp.sum(s * s, axis=0, keepdims=True) * (1.0 / HEAD_DIM)
            s = s * lax.rsqrt(ms + EPS) * tg_ref[...]
        partner = jnp.concatenate([s[a:b] for a, b in tperm], axis=0)
        t_ref[h] = (s * tc + partner * ts).astype(BF16)
    vf = jnp.dot(xn, wv_ref[...], preferred_element_type=F32).astype(BF16)
    ones_col = (lax.broadcasted_iota(jnp.int32, (vf.shape[0], nv - dv), 1) == 0).astype(BF16)
    for h in range(hv):
        v_ref[:, h * nv:h * nv + dv] = vf[:, h * dv:(h + 1) * dv]
        v_ref[:, h * nv + dv:(h + 1) * nv] = ones_col


def _project(h, g, wq, wk, wv, row_tabs, t_tabs, *, hr, ht, hv, dv, nv, shift, tperm, qk_norm, rg=None, tg=None,
             tm=512):
    s, d = h.shape
    tm = min(tm, s)
    assert hr % 2 == 0
    wr = wq.astype(BF16)
    wtt = wk.T.astype(BF16)
    wv_p = wv.astype(BF16)
    rc, rs1, rs2 = row_tabs
    tc, ts = t_tabs
    if rg is None:
        rg = jnp.ones((1, LANES), F32)
        tg = jnp.ones((HEAD_DIM, 1), F32)
    full = lambda a: pl.BlockSpec(a.shape, lambda i: (0,) * a.ndim)
    kern = functools.partial(_proj_kernel, hr=hr, ht=ht, hv=hv, nv=nv, dv=dv, shift=shift, tperm=tperm,
                             qk_norm=qk_norm)
    return pl.pallas_call(
        kern,
        out_shape=(jax.ShapeDtypeStruct((hr, s, HEAD_DIM), BF16),
                   jax.ShapeDtypeStruct((ht, HEAD_DIM, s), BF16),
                   jax.ShapeDtypeStruct((s, hv * nv), BF16)),
        grid=(s // tm,),
        in_specs=[pl.BlockSpec((tm, d), lambda i: (i, 0)), full(g), full(wr), full(wtt), full(wv_p),
                  pl.BlockSpec((tm, LANES), lambda i: (i, 0)), pl.BlockSpec((tm, LANES), lambda i: (i, 0)),
                  pl.BlockSpec((tm, LANES), lambda i: (i, 0)),
                  pl.BlockSpec((HEAD_DIM, tm), lambda i: (0, i)), pl.BlockSpec((HEAD_DIM, tm), lambda i: (0, i)),
                  full(rg), full(tg)],
        out_specs=(pl.BlockSpec((hr, tm, HEAD_DIM), lambda i: (0, i, 0)),
                   pl.BlockSpec((ht, HEAD_DIM, tm), lambda i: (0, 0, i)),
                   pl.BlockSpec((tm, hv * nv), lambda i: (i, 0))),
        compiler_params=_cparams(("parallel",)),
        name="proj",
    )(h, g, wr, wtt, wv_p, rc, rs1, rs2, tc, ts, rg, tg)


def _rope_tables(ang_list, s, scale):
    cs, sn1, sn2, ksn = [], [], [], []
    used = 0
    for ang in ang_list:
        c, sn = jnp.cos(ang), jnp.sin(ang)
        z = jnp.zeros_like(sn)
        cs += [c, c]
        sn1 += [-sn, z]
        sn2 += [z, sn]
        ksn += [-sn, sn]
        used += 2 * ang.shape[1]
    rest = HEAD_DIM - used
    ones, zeros = jnp.ones((s, rest), F32), jnp.zeros((s, rest), F32)
    c64 = jnp.concatenate(cs + [ones], axis=1)
    twice = lambda parts: jnp.concatenate(parts + parts, axis=1)
    row = (twice([c64]), twice(sn1 + [zeros]), twice(sn2 + [zeros]))
    tr = (c64.T, jnp.concatenate(ksn + [zeros], axis=1).T)
    return {"row_scaled": tuple(t * scale for t in row), "t": tr}


def _lane_tile(x, width):
    reps = width // LANES
    return x if reps == 1 else jnp.concatenate([x] * reps, axis=1)


def _flash_kernel(*refs, streams, split, tq, tk, nv, dv, band, halo, seq, mode, lam_init, has_sink):
    it = iter(refs)
    q_ref, kt_ref, v_ref = next(it), next(it), next(it)
    sink_ref = next(it) if has_sink else None
    lam_ref = subln_ref = None
    if mode == "diff":
        lam_ref, subln_ref = next(it), next(it)
    o_ref = next(it)
    lse_ref = next(it) if mode == "lse" else None
    m_scr, acc_scr = next(it), next(it)

    qb, kb = pl.program_id(1), pl.program_id(2)
    nsteps = pl.num_programs(2)
    cr = tq // split

    @pl.when(kb == 0)
    def _init():
        for si, (qi, _, _) in enumerate(streams):
            if has_sink:
                sink2 = jnp.full((tq, LANES), LOG2E, F32) * sink_ref[pl.program_id(0), qi]
                m0 = jnp.maximum(sink2, M_FLOOR)
                m_scr[si] = m0
                lane = lax.broadcasted_iota(jnp.int32, (tq, nv), 1)
                acc_scr[si] = jnp.where(lane == dv, _lane_tile(jnp.exp2(sink2 - m0), nv), 0.0)
            else:
                m_scr[si] = jnp.full((tq, LANES), M_FLOOR, F32)
                acc_scr[si] = jnp.zeros((tq, nv), F32)

    def _step():
        chunks = [(si, c * cr, qi, ki, vi) for si, (qi, ki, vi) in enumerate(streams) for c in range(split)]
        scores = [jnp.dot(q_ref[qi, r0:r0 + cr, :], kt_ref[ki], preferred_element_type=F32)
                  for _, r0, qi, ki, _ in chunks]
        for sc, (si, r0, _, _, vi) in zip(scores, chunks):
            if band is not None:
                halfw, dil, off = band
                row = lax.broadcasted_iota(jnp.int32, (cr, tk), 0) + r0
                col = lax.broadcasted_iota(jnp.int32, (cr, tk), 1)
                diff = col - row + kshift
                valid = jnp.abs(diff) <= halfw
                if dil > 1:
                    valid = valid & ((diff & (dil - 1)) == 0)
                sc = jnp.where(valid, sc, NEG_INF)
            m_old = m_scr[si, r0:r0 + cr, :]
            m_new = jnp.maximum(m_old, jnp.broadcast_to(jnp.max(sc, axis=1, keepdims=True), (cr, LANES)))
            p = jnp.exp2(sc - _lane_tile(m_new, tk))
            alpha = jnp.exp2(m_old - m_new)
            pv = jnp.dot(p.astype(BF16), v_ref[:, vi * nv:(vi + 1) * nv], preferred_element_type=F32)
            acc_scr[si, r0:r0 + cr, :] = acc_scr[si, r0:r0 + cr, :] * _lane_tile(alpha, nv) + pv
            m_scr[si, r0:r0 + cr, :] = m_new

    if band is None:
        _step()
    elif halo is not None:
        kshift = jnp.clip(qb * tq - halo, 0, seq - tk) - qb * tq
        _step()
    else:
        _, _, off = band
        kshift = (kb - off) * tk
        kabs = qb * (tq // tk) + kb - off
        pl.when((kabs >= 0) & (kabs < seq // tk))(_step)

    @pl.when(kb == nsteps - 1)
    def _fin():
        if mode == "diff":
            a0, a1 = acc_scr[0], acc_scr[1]
            lam_rows = lam_ref[...]
            lam = (jnp.exp(jnp.sum(lam_rows[0:1] * lam_rows[1:2], axis=1, keepdims=True))
                   - jnp.exp(jnp.sum(lam_rows[2:3] * lam_rows[3:4], axis=1, keepdims=True)) + lam_init)
            o = a0[:, :dv] / a0[:, dv:dv + 1] - lam * (a1[:, :dv] / a1[:, dv:dv + 1])
            o = _rms(o, subln_ref[...]) * (1.0 - lam_init)
            o_ref[...] = o.astype(o_ref.dtype)
        else:
            for si in range(len(streams)):
                a = acc_scr[si]
                l = a[:, dv:dv + 1]
                o_ref[:, si * dv:(si + 1) * dv] = (a[:, :dv] / l).astype(o_ref.dtype)
                if mode == "lse":
                    lse_ref[:, si * dv:(si + 1) * dv] = m_scr[si][:, :dv] + jnp.log2(l)


def _flash(q, kt, v, *, units, q_per_unit, k_per_unit, v_per_unit, streams, tq, tk, nv, dv, out_w,
           split=1, unit0=0, band=None, halo=None, mode="gqa", sink=None, lam=None, subln=None, lam_init=0.0,
           name="flash"):
    s = q.shape[1]
    tq, tk = min(tq, s), min(tk, s)
    if halo is not None and tq + 2 * halo > s:
        halo, tq = None, max(tq, tk)
    if halo is not None:
        tk = tq + 2 * halo
    assert s % tq == 0 and (halo is not None or (s % tk == 0 and (band is None or tq % tk == 0)))
    n_kblocks = s // tk
    kspec = vspec = None
    if band is None:
        nsteps = n_kblocks
        kmap = lambda u, i, j: (u + unit0, 0, j)
        vmap = lambda u, i, j: (j, u + unit0)
    elif halo is not None:
        assert tq % LANES == 0 and halo % LANES == 0
        nsteps = 1
        band = (band[0], band[1], 0)
        koff = lambda i: jnp.clip(i * (tq // LANES) - halo // LANES, 0, (s - tk) // LANES) * LANES
        kspec = pl.BlockSpec((pl.Element(k_per_unit), pl.Element(HEAD_DIM), pl.Element(tk)),
                             lambda u, i, j: ((u + unit0) * k_per_unit, 0, koff(i)))
        vspec = pl.BlockSpec((pl.Element(tk), pl.Element(v_per_unit * nv)),
                             lambda u, i, j: (koff(i), (u + unit0) * v_per_unit * nv))
    else:
        halfw = band[0]
        off = -(-halfw // tk)
        nsteps = tq // tk + 2 * off
        band = (band[0], band[1], off)
        kidx = lambda i, j: jnp.clip(i * (tq // tk) + j - off, 0, n_kblocks - 1)
        kmap = lambda u, i, j: (u + unit0, 0, kidx(i, j))
        vmap = lambda u, i, j: (kidx(i, j), u + unit0)
    in_specs = [pl.BlockSpec((q_per_unit, tq, HEAD_DIM), lambda u, i, j: (u + unit0, i, 0)),
                kspec or pl.BlockSpec((k_per_unit, HEAD_DIM, tk), kmap),
                vspec or pl.BlockSpec((tk, v_per_unit * nv), vmap)]
    args = [q, kt, v]
    if sink is not None:
        in_specs.append(pl.BlockSpec(memory_space=pltpu.SMEM))
        args.append(sink)
    if mode == "diff":
        in_specs += [pl.BlockSpec(lam.shape, lambda u, i, j: (0, 0)), pl.BlockSpec(subln.shape, lambda u, i, j: (0, 0))]
        args += [lam, subln]
    out_shape = [jax.ShapeDtypeStruct((s, units * out_w), BF16)]
    out_specs = [pl.BlockSpec((tq, out_w), lambda u, i, j: (i, u))]
    if mode == "lse":
        out_shape.append(jax.ShapeDtypeStruct((s, units * out_w), F32))
        out_specs.append(pl.BlockSpec((tq, out_w), lambda u, i, j: (i, u)))
    assert tq % split == 0
    kern = functools.partial(_flash_kernel, streams=streams, split=split, tq=tq, tk=tk, nv=nv, dv=dv, band=band,
                             halo=halo, seq=s, mode=mode, lam_init=lam_init, has_sink=sink is not None)
    res = pl.pallas_call(
        kern,
        out_shape=tuple(out_shape),
        grid=(units, s // tq, nsteps),
        in_specs=in_specs,
        out_specs=tuple(out_specs),
        scratch_shapes=[pltpu.VMEM((len(streams), tq, LANES), F32), pltpu.VMEM((len(streams), tq, nv), F32)],
        compiler_params=_cparams(("parallel", "parallel", "arbitrary")),
        name=name,
    )(*args)
    return res if mode == "lse" else res[0]


def _outproj_kernel(h_ref, o_ref, w_ref, out_ref):
    out_ref[...] = h_ref[...] + jnp.dot(o_ref[...], w_ref[...], preferred_element_type=F32)


def _outproj(h, o, w, tm=512):
    s, d = h.shape
    tm = min(tm, s)
    w = w.astype(BF16)
    return pl.pallas_call(
        _outproj_kernel,
        out_shape=jax.ShapeDtypeStruct((s, d), F32),
        grid=(s // tm,),
        in_specs=[pl.BlockSpec((tm, d), lambda i: (i, 0)), pl.BlockSpec((tm, o.shape[1]), lambda i: (i, 0)),
                  pl.BlockSpec(w.shape, lambda i: (0, 0))],
        out_specs=pl.BlockSpec((tm, d), lambda i: (i, 0)),
        compiler_params=_cparams(("parallel",)),
        name="outproj",
    )(h, o, w)


def _outproj_groups_kernel(h_ref, o0_ref, o1_ref, o2_ref, l0_ref, l1_ref, l2_ref, w_ref, out_ref):
    l0, l1, l2 = l0_ref[...], l1_ref[...], l2_ref[...]
    mx = jnp.maximum(jnp.maximum(l0, l1), l2)
    e0, e1, e2 = jnp.exp2(l0 - mx), jnp.exp2(l1 - mx), jnp.exp2(l2 - mx)
    tot = e0 + e1 + e2
    acc = h_ref[...]
    gw = o0_ref.shape[1]
    for g, (o_ref, e) in enumerate(((o0_ref, e0), (o1_ref, e1), (o2_ref, e2))):
        og = (o_ref[...].astype(F32) * (e / tot)).astype(BF16)
        acc = acc + jnp.dot(og, w_ref[g * gw:(g + 1) * gw, :], preferred_element_type=F32)
    out_ref[...] = acc


def _outproj_groups(h, os_, ls_, w, tm=512):
    s, d = h.shape
    tm = min(tm, s)
    w = w.astype(BF16)
    gw = os_[0].shape[1]
    row = lambda width: pl.BlockSpec((tm, width), lambda i: (i, 0))
    return pl.pallas_call(
        _outproj_groups_kernel,
        out_shape=jax.ShapeDtypeStruct((s, d), F32),
        grid=(s // tm,),
        in_specs=[row(d)] + [row(gw)] * 6 + [pl.BlockSpec(w.shape, lambda i: (0, 0))],
        out_specs=row(d),
        compiler_params=_cparams(("parallel",)),
        name="outproj_groups",
    )(h, *os_, *ls_, w)


def _route(logits):
    lane = lax.broadcasted_iota(jnp.int32, logits.shape, 1).astype(F32)
    big = 1e6
    gl = jnp.where(lane < MOE_GROUPS, logits, NEG_INF)
    gmax = jnp.max(gl, axis=1, keepdims=True)
    gidx = jnp.min(jnp.where(gl == gmax, lane, big), axis=1, keepdims=True)
    gw = 1.0 / jnp.sum(jnp.exp(gl - gmax), axis=1, keepdims=True)
    lo = MOE_GROUPS + gidx * MOE_PER_GROUP
    el = jnp.where((lane >= lo) & (lane < lo + MOE_PER_GROUP), logits, NEG_INF)
    v1 = jnp.max(el, axis=1, keepdims=True)
    i1 = jnp.min(jnp.where(el == v1, lane, big), axis=1, keepdims=True)
    el2 = jnp.where(lane == i1, NEG_INF, el)
    v2 = jnp.max(el2, axis=1, keepdims=True)
    i2 = jnp.min(jnp.where(el2 == v2, lane, big), axis=1, keepdims=True)
    e2 = jnp.exp(v2 - v1)
    w1 = gw / (1.0 + e2)
    w2 = w1 * e2
    return jnp.where(lane == i1, w1, 0.0) + jnp.where(lane == i2, w2, 0.0), gidx


MOE_ROW_TILE = 512
INFO_GROUP_LANE = 0
INFO_RANK_LANE = 1
SUBLANES = 8


def _to_slabs(ref, x):
    for c in range(SUBLANES):
        ref[:, c, :] = x[:, c * LANES:(c + 1) * LANES]


def _from_slabs(ref):
    return jnp.concatenate([ref[:, c, :] for c in range(SUBLANES)], axis=1)


def _moe_route_kernel(h_ref, g_ref, wr_ref, br_ref, tri_ref, x3_ref, info_ref, cnt_ref, run_scr):
    @pl.when(pl.program_id(0) == 0)
    def _init():
        run_scr[...] = jnp.zeros_like(run_scr)

    xn = _rms(h_ref[...], g_ref[...])
    logits = jnp.dot(xn, wr_ref[...], preferred_element_type=F32, precision=lax.Precision.HIGHEST) + br_ref[...]
    comb, gidx = _route(logits)
    lane = lax.broadcasted_iota(jnp.int32, comb.shape, 1).astype(F32)
    onehot = lane == gidx
    before = jnp.dot(tri_ref[...], onehot.astype(BF16), preferred_element_type=F32)
    run = run_scr[...]
    rank = jnp.sum(jnp.where(onehot, before + run, 0.0), axis=1, keepdims=True)
    run = run + jnp.sum(onehot.astype(F32), axis=0, keepdims=True)
    run_scr[...] = run
    cnt_ref[...] = run
    _to_slabs(x3_ref, xn)
    info_ref[...] = jnp.where(lane == INFO_GROUP_LANE, gidx, jnp.where(lane == INFO_RANK_LANE, rank, comb))


def _moe_dispatch_kernel(pos_ref, x3_ref, info_ref, xs_in_ref, infos_in_ref, xs_ref, infos_ref, sem):
    del xs_in_ref, infos_in_ref
    tm = x3_ref.shape[0]

    def issue(t, c):
        p = pos_ref[0, t]
        pltpu.make_async_copy(x3_ref.at[pl.ds(t, 1)], xs_ref.at[pl.ds(p, 1)], sem).start()
        pltpu.make_async_copy(info_ref.at[pl.ds(t, 1)], infos_ref.at[pl.ds(p, 1)], sem).start()
        return c

    lax.fori_loop(0, tm, issue, 0, unroll=8)
    pltpu.make_async_copy(x3_ref, xs_ref.at[pl.ds(0, tm)], sem).wait()
    pltpu.make_async_copy(info_ref, infos_ref.at[pl.ds(0, tm)], sem).wait()


def _moe_expert_kernel(tg_ref, nv_ref, xs_ref, infos_ref, wg_ref, wu_ref, wd_ref, y_ref):
    j = pl.program_id(0)

    @pl.when(j < nv_ref[0])
    def _live():
        x = _from_slabs(xs_ref).astype(BF16)
        info = infos_ref[...]
        lane = lax.broadcasted_iota(jnp.int32, info.shape, 1)
        first = tg_ref[j] * MOE_PER_GROUP + MOE_GROUPS
        acc = None
        for e in range(MOE_PER_GROUP):
            we = jnp.sum(jnp.where(lane == first + e, info, 0.0), axis=1, keepdims=True)
            gate = jnp.dot(x, wg_ref[e], preferred_element_type=F32)
            up = jnp.dot(x, wu_ref[e], preferred_element_type=F32)
            act = gate * jax.nn.sigmoid(gate) * up * we
            part = jnp.dot(act.astype(BF16), wd_ref[e], preferred_element_type=F32)
            acc = part if acc is None else acc + part
        _to_slabs(y_ref, acc)

    @pl.when(j >= nv_ref[0])
    def _dead():
        y_ref[...] = jnp.zeros_like(y_ref)


def _moe_sparse(h, g, w_group, b_group, w_expert, b_expert, w_gate, w_up, w_down, tm=1024, tmd=512):
    s, d = h.shape
    assert d == SUBLANES * LANES
    tm, tmd, tb = min(tm, s), min(tmd, s), min(MOE_ROW_TILE, s)
    ff = w_gate.shape[2]
    nr = MOE_GROUPS + MOE_EXPERTS
    wr = jnp.pad(jnp.concatenate([w_group, w_expert], axis=1), ((0, 0), (0, LANES - nr)))
    br = jnp.pad(jnp.concatenate([b_group, b_expert]), (0, LANES - nr)).reshape(1, LANES)
    tri = (lax.broadcasted_iota(jnp.int32, (tm, tm), 0) > lax.broadcasted_iota(jnp.int32, (tm, tm), 1)).astype(BF16)
    const = lambda a: pl.BlockSpec(a.shape, lambda i: (0, 0))
    slab = lambda rows: pl.BlockSpec((rows, SUBLANES, LANES), lambda i: (i, 0, 0))
    x3, info, cnt = pl.pallas_call(
        _moe_route_kernel,
        out_shape=(jax.ShapeDtypeStruct((s, SUBLANES, LANES), F32), jax.ShapeDtypeStruct((s, LANES), F32),
                   jax.ShapeDtypeStruct((1, LANES), F32)),
        grid=(s // tm,),
        in_specs=[pl.BlockSpec((tm, d), lambda i: (i, 0)), const(g), const(wr), const(br), const(tri)],
        out_specs=(slab(tm), pl.BlockSpec((tm, LANES), lambda i: (i, 0)), pl.BlockSpec((1, LANES), lambda i: (0, 0))),
        scratch_shapes=[pltpu.VMEM((1, LANES), F32)],
        compiler_params=_cparams(("arbitrary",)),
        name="moe_route",
    )(h, g, wr, br, tri)

    counts = cnt[0, :MOE_GROUPS].astype(jnp.int32)
    padded = ((counts + tb - 1) // tb) * tb
    ends = jnp.cumsum(padded)
    starts = ends - padded
    tok_group = info[:, INFO_GROUP_LANE].astype(jnp.int32)
    tok_rank = info[:, INFO_RANK_LANE].astype(jnp.int32)
    pos = starts[tok_group] + tok_rank
    n_tiles = s // tb + MOE_GROUPS
    p_rows = n_tiles * tb
    tile_group = jnp.minimum(jnp.sum((jnp.arange(n_tiles) * tb)[:, None] >= ends[None, :], axis=1), MOE_GROUPS - 1)
    n_valid = (ends[-1] // tb).reshape(1)

    anyspec = pl.BlockSpec(memory_space=pl.ANY)
    xs, infos = pl.pallas_call(
        _moe_dispatch_kernel,
        out_shape=(jax.ShapeDtypeStruct((p_rows, SUBLANES, LANES), F32), jax.ShapeDtypeStruct((p_rows, LANES), F32)),
        grid=(s // tmd,),
        in_specs=[pl.BlockSpec((None, 1, tmd), lambda i: (i, 0, 0), memory_space=pltpu.SMEM),
                  slab(tmd), pl.BlockSpec((tmd, LANES), lambda i: (i, 0)), anyspec, anyspec],
        out_specs=(anyspec, anyspec),
        scratch_shapes=[pltpu.SemaphoreType.DMA],
        input_output_aliases={3: 0, 4: 1},
        compiler_params=_cparams(("arbitrary",)),
        name="moe_dispatch",
    )(pos.reshape(s // tmd, 1, tmd), x3, info, jnp.zeros((p_rows, SUBLANES, LANES), F32),
      jnp.zeros((p_rows, LANES), F32))

    wg, wu, wd = w_gate.astype(BF16), w_up.astype(BF16), w_down.astype(BF16)
    wsel = lambda j, tg, nv: (jnp.where(j < nv[0], tg[j], MOE_GROUPS - 1), 0, 0)
    y = pl.pallas_call(
        _moe_expert_kernel,
        out_shape=jax.ShapeDtypeStruct((p_rows, SUBLANES, LANES), F32),
        grid_spec=pltpu.PrefetchScalarGridSpec(
            num_scalar_prefetch=2,
            grid=(n_tiles,),
            in_specs=[pl.BlockSpec((tb, SUBLANES, LANES), lambda j, tg, nv: (j, 0, 0)),
                      pl.BlockSpec((tb, LANES), lambda j, tg, nv: (j, 0)),
                      pl.BlockSpec((MOE_PER_GROUP, d, ff), wsel), pl.BlockSpec((MOE_PER_GROUP, d, ff), wsel),
                      pl.BlockSpec((MOE_PER_GROUP, ff, d), wsel)],
            out_specs=pl.BlockSpec((tb, SUBLANES, LANES), lambda j, tg, nv: (j, 0, 0))),
        compiler_params=_cparams(("arbitrary",)),
        name="moe_expert",
    )(tile_group.astype(jnp.int32), n_valid.astype(jnp.int32), xs, infos, wg, wu, wd)
    return y, pos


def _ple_kernel(pos_ref, posn_ref, h_ref, y_hbm, g_ref, wg_ref, p_ref, wp_ref, gf_ref, out_ref, ybuf, sem, *, final):
    tm = h_ref.shape[0]
    i, n = pl.program_id(0), pl.num_programs(0)
    slot = lax.rem(i, 2)

    def gather(idx_ref, dst_slot):
        def issue(t, c):
            pltpu.make_async_copy(y_hbm.at[pl.ds(idx_ref[0, t], 1)], ybuf.at[dst_slot, pl.ds(t, 1)],
                                  sem.at[dst_slot]).start()
            return c
        lax.fori_loop(0, tm, issue, 0, unroll=8)

    @pl.when(i == 0)
    def _first():
        gather(pos_ref, slot)

    @pl.when(i + 1 < n)
    def _next():
        gather(posn_ref, 1 - slot)

    pltpu.make_async_copy(y_hbm.at[pl.ds(0, tm)], ybuf.at[slot], sem.at[slot]).wait()
    x = h_ref[...] + _from_slabs(ybuf.at[slot])
    xn = _rms(x, g_ref[...]).astype(BF16)
    gate = jax.nn.sigmoid(jnp.dot(xn, wg_ref[...], preferred_element_type=F32))
    proj = jnp.dot(p_ref[...].astype(BF16), wp_ref[...], preferred_element_type=F32)
    y = x + gate * proj
    if final:
        y = _rms(y, gf_ref[...])
    out_ref[...] = y


def _ple(h, y_sorted, pos, g, wg, p, wp, gf, final, tm=512):
    s, d = h.shape
    tm = min(tm, s)
    wg, wp = wg.astype(BF16), wp.astype(BF16)
    const = lambda a: pl.BlockSpec(a.shape, lambda i: (0, 0))
    n = s // tm
    pos3 = pos.reshape(n, 1, tm)
    return pl.pallas_call(
        functools.partial(_ple_kernel, final=final),
        out_shape=jax.ShapeDtypeStruct((s, d), F32),
        grid=(n,),
        in_specs=[pl.BlockSpec((None, 1, tm), lambda i: (i, 0, 0), memory_space=pltpu.SMEM),
                  pl.BlockSpec((None, 1, tm), lambda i: (jnp.minimum(i + 1, n - 1), 0, 0), memory_space=pltpu.SMEM),
                  pl.BlockSpec((tm, d), lambda i: (i, 0)), pl.BlockSpec(memory_space=pl.ANY), const(g), const(wg),
                  pl.BlockSpec((tm, p.shape[1]), lambda i: (i, 0)), const(wp), const(gf)],
        out_specs=pl.BlockSpec((tm, d), lambda i: (i, 0)),
        scratch_shapes=[pltpu.VMEM((2, tm, SUBLANES, LANES), F32), pltpu.SemaphoreType.DMA((2,))],
        compiler_params=_cparams(("arbitrary",)),
        name="ple",
    )(pos3, pos3, h, y_sorted, g, wg, p, wp, gf)


PARTIAL_KPERM = ((8, 16), (0, 8), (16, 64))
AXIAL_KPERM = ((16, 32), (0, 16), (48, 64), (32, 48))
B_PAIRS = ((128, 1), (512, 4), (2048, 16))


def _partial_tables(s):
    inv = ROPE_THETA ** (-jnp.arange(0, ROT_DIM, 2, dtype=F32) / ROT_DIM)
    ang = jnp.arange(s).astype(F32)[:, None] * inv[None, :]
    return _rope_tables([ang], s, LOG2E * HEAD_DIM ** -0.5)


def _axial_tables(s):
    half = HEAD_DIM // 2
    inv = AXIAL_THETA ** (-jnp.arange(0, half, 2, dtype=F32) / half)
    t = jnp.arange(s)
    ang_r = (t // GRID_W).astype(F32)[:, None] * inv[None, :]
    ang_c = (t % GRID_W).astype(F32)[:, None] * inv[None, :]
    return _rope_tables([ang_r, ang_c], s, LOG2E * HEAD_DIM ** -0.5)


def _mixer_diff(h, g, w_in, w_out, lam_rows, subln, lam_init, tabs):
    d = h.shape[1]
    heads = d // (2 * HEAD_DIM)
    aw = 2 * heads * HEAD_DIM
    dv = 2 * HEAD_DIM
    q, kt, v = _project(h, g, w_in[:, :aw], w_in[:, aw:2 * aw], w_in[:, 2 * aw:], tabs["row_scaled"], tabs["t"],
                        hr=2 * heads, ht=2 * heads, hv=heads, dv=dv, nv=2 * dv,
                        shift=ROT_DIM // 2, tperm=PARTIAL_KPERM, qk_norm=False)
    o = _flash(q, kt, v, units=heads, q_per_unit=2, k_per_unit=2, v_per_unit=1,
               streams=((0, 0, 0), (1, 1, 0)), split=1, tq=1024, tk=2048, nv=2 * dv, dv=dv, out_w=dv,
               mode="diff", lam=lam_rows, subln=subln, lam_init=lam_init, name="flash_diff")
    return _outproj(h, o, w_out)


def _mixer_dilated(h, g, w_in, w_out, tabs):
    nh, hg = 12, 4
    bw = nh * HEAD_DIM
    q, kt, v = _project(h, g, w_in[:, :bw], w_in[:, bw:2 * bw], w_in[:, 2 * bw:], tabs["row_scaled"], tabs["t"],
                        hr=nh, ht=nh, hv=nh, dv=HEAD_DIM, nv=LANES,
                        shift=ROT_DIM // 2, tperm=PARTIAL_KPERM, qk_norm=False)
    os_, ls_ = [], []
    for gi, (win, dil) in enumerate(B_PAIRS):
        halfw = (win // (2 * dil)) * dil
        o, lse = _flash(q, kt, v, units=1, unit0=gi, q_per_unit=hg, k_per_unit=hg, v_per_unit=hg,
                        streams=tuple((j, j, j) for j in range(hg)), tq=256, tk=max(halfw, 256),
                        nv=LANES, dv=HEAD_DIM, out_w=hg * HEAD_DIM, band=(halfw, dil),
                        halo=-(-halfw // LANES) * LANES, mode="lse",
                        name=f"flash_dilated{gi}")
        os_.append(o)
        ls_.append(lse)
    return _outproj_groups(h, os_, ls_, w_out)


GQA_HEADS, GQA_KV_HEADS = 16, 4


def _mixer_window(h, g, w_in, w_out, sink, tabs):
    qd, kvd = GQA_HEADS * HEAD_DIM, GQA_KV_HEADS * HEAD_DIM
    grp = GQA_HEADS // GQA_KV_HEADS
    q, kt, v = _project(h, g, w_in[:, :qd], w_in[:, qd:qd + kvd], w_in[:, qd + kvd:], tabs["row_scaled"], tabs["t"],
                        hr=GQA_HEADS, ht=GQA_KV_HEADS, hv=GQA_KV_HEADS, dv=HEAD_DIM, nv=LANES,
                        shift=ROT_DIM // 2, tperm=PARTIAL_KPERM, qk_norm=False)
    o = _flash(q, kt, v, units=GQA_KV_HEADS, q_per_unit=grp, k_per_unit=1, v_per_unit=1,
               streams=tuple((j, 0, 0) for j in range(grp)), tq=256, tk=256, nv=LANES, dv=HEAD_DIM,
               out_w=grp * HEAD_DIM, band=(128, 1), halo=128, mode="gqa", sink=sink.reshape(GQA_KV_HEADS, grp),
               name="flash_window")
    return _outproj(h, o, w_out)


def _mixer_axial(h, g, w_in, w_out, q_norm, k_norm, tabs):
    qd, kvd = GQA_HEADS * HEAD_DIM, GQA_KV_HEADS * HEAD_DIM
    grp = GQA_HEADS // GQA_KV_HEADS
    qg = jnp.tile(q_norm, 2).reshape(1, LANES)
    kg = k_norm.reshape(HEAD_DIM, 1)
    q, kt, v = _project(h, g, w_in[:, :qd], w_in[:, qd:qd + kvd], w_in[:, qd + kvd:], tabs["row_scaled"], tabs["t"],
                        hr=GQA_HEADS, ht=GQA_KV_HEADS, hv=GQA_KV_HEADS, dv=HEAD_DIM, nv=LANES,
                        shift=HEAD_DIM // 4, tperm=AXIAL_KPERM, qk_norm=True, rg=qg, tg=kg)
    o = _flash(q, kt, v, units=GQA_KV_HEADS, q_per_unit=grp, k_per_unit=1, v_per_unit=1,
               streams=tuple((j, 0, 0) for j in range(grp)), split=8, tq=1024, tk=2048, nv=LANES, dv=HEAD_DIM,
               out_w=grp * HEAD_DIM, mode="gqa", name="flash_axial")
    return _outproj(h, o, w_out)


def kernel(x, p, norm_mix, norm_ffn, norm_ple, norm_final, a_w_in, a_w_out, a_lam_q1, a_lam_k1, a_lam_q2, a_lam_k2, a_subln, b_w_in, b_w_out, c_w_in, c_w_out, c_sink, d_w_in, d_w_out, d_q_norm, d_k_norm, moe_w_group, moe_b_group, moe_w_expert, moe_b_expert, moe_w_gate, moe_w_up, moe_w_down, ple_w_gate, ple_w_proj):
    bn, s, d = x.shape
    assert bn == 1
    depth = p.shape[0]
    h = x[0]
    ptabs = _partial_tables(s)
    atabs = _axial_tables(s)
    row = lambda a: a.reshape(1, -1)
    for i in range(depth):
        r, kind = divmod(i, 4)
        g = row(norm_mix[i])
        if kind == 0:
            lam_init = 0.8 - 0.6 * math.exp(-0.3 * i)
            lam_rows = jnp.stack([a_lam_q1[r], a_lam_k1[r], a_lam_q2[r], a_lam_k2[r]])
            h = _mixer_diff(h, g, a_w_in[r], a_w_out[r], lam_rows, row(a_subln[r]), lam_init, ptabs)
        elif kind == 1:
            h = _mixer_dilated(h, g, b_w_in[r], b_w_out[r], ptabs)
        elif kind == 2:
            h = _mixer_window(h, g, c_w_in[r], c_w_out[r], c_sink[r], ptabs)
        else:
            h = _mixer_axial(h, g, d_w_in[r], d_w_out[r], d_q_norm[r], d_k_norm[r], atabs)
        y_sorted, pos = _moe_sparse(h, row(norm_ffn[i]), moe_w_group[i], moe_b_group[i], moe_w_expert[i],
                                    moe_b_expert[i], moe_w_gate[i], moe_w_up[i], moe_w_down[i])
        h = _ple(h, y_sorted, pos, row(norm_ple[i]), ple_w_gate[i], p[i, 0], ple_w_proj[i], row(norm_final), final=(i == depth - 1))
    return h[None]
```

```python
import functools
import math

import jax
import jax.numpy as jnp
from jax import lax
from jax.experimental import pallas as pl
from jax.experimental.pallas import tpu as pltpu

F32 = jnp.float32
BF16 = jnp.bfloat16

HEAD_DIM = 64
LANES = 128
EPS = 1e-6
LOG2E = 1.4426950408889634
NEG_INF = -1e30
M_FLOOR = -1e29
ROPE_THETA = 500000.0
ROT_DIM = HEAD_DIM // 4
AXIAL_THETA = 10000.0
GRID_W = 64
MOE_GROUPS = 4
MOE_PER_GROUP = 4
MOE_EXPERTS = 16
VMEM_LIMIT = 56 * 1024 * 1024


def _cparams(sem):
    return pltpu.CompilerParams(dimension_semantics=sem, vmem_limit_bytes=VMEM_LIMIT)


def _rms(x, g):
    return x * lax.rsqrt(jnp.mean(x * x, axis=-1, keepdims=True) + EPS) * g


def _proj_kernel(h_ref, g_ref, wr_ref, wtt_ref, wv_ref, rc_ref, rs1_ref, rs2_ref, tc_ref, ts_ref,
                 rg_ref, tg_ref, r_ref, t_ref, v_ref, *, hr, ht, hv, nv, dv, shift, tperm, qk_norm):
    xn = _rms(h_ref[...], g_ref[...]).astype(BF16)
    rf = jnp.dot(xn, wr_ref[...], preferred_element_type=F32)
    rc, rs1, rs2 = rc_ref[...], rs1_ref[...], rs2_ref[...]
    low = lax.broadcasted_iota(jnp.int32, (rf.shape[0], LANES), 1) < HEAD_DIM
    for j in range(hr // 2):
        s = rf[:, j * LANES:(j + 1) * LANES]
        if qk_norm:
            sq = s * s
            ms = jnp.where(low, jnp.sum(jnp.where(low, sq, 0.0), axis=-1, keepdims=True),
                           jnp.sum(jnp.where(low, 0.0, sq), axis=-1, keepdims=True)) * (1.0 / HEAD_DIM)
            s = s * lax.rsqrt(ms + EPS) * rg_ref[...]
        r = s * rc + pltpu.roll(s, LANES - shift, 1) * rs1 + pltpu.roll(s, shift, 1) * rs2
        r_ref[2 * j] = r[:, :HEAD_DIM].astype(BF16)
        r_ref[2 * j + 1] = r[:, HEAD_DIM:].astype(BF16)
    nt = (((1,), (1,)), ((), ()))
    tf = lax.dot_general(wtt_ref[...], xn, nt, preferred_element_type=F32)
    tc, ts = tc_ref[...], ts_ref[...]
    for h in range(ht):
        s = tf[h * HEAD_DIM:(h + 1) * HEAD_DIM, :]
        if qk_norm:
            ms = jnp.sum(s * s, axis=0, keepdims=True) * (1.0 / HEAD_DIM)
            s = s * lax.rsqrt(ms + EPS) * tg_ref[...]
        partner = jnp.concatenate([s[a:b] for a, b in tperm], axis=0)
        t_ref[h] = (s * tc + partner * ts).astype(BF16)
    vf = jnp.dot(xn, wv_ref[...], preferred_element_type=F32).astype(BF16)
    ones_col = (lax.broadcasted_iota(jnp.int32, (vf.shape[0], nv - dv), 1) == 0).astype(BF16)
    for h in range(hv):
        v_ref[:, h * nv:h * nv + dv] = vf[:, h * dv:(h + 1) * dv]
        v_ref[:, h * nv + dv:(h + 1) * nv] = ones_col


def _project(h, g, wq, wk, wv, row_tabs, t_tabs, *, hr, ht, hv, dv, nv, shift, tperm, qk_norm, rg=None, tg=None,
             tm=512):
    s, d = h.shape
    tm = min(tm, s)
    assert hr % 2 == 0
    wr = wq.astype(BF16)
    wtt = wk.T.astype(BF16)
    wv_p = wv.astype(BF16)
    rc, rs1, rs2 = row_tabs
    tc, ts = t_tabs
    if rg is None:
        rg = jnp.ones((1, LANES), F32)
        tg = jnp.ones((HEAD_DIM, 1), F32)
    full = lambda a: pl.BlockSpec(a.shape, lambda i: (0,) * a.ndim)
    kern = functools.partial(_proj_kernel, hr=hr, ht=ht, hv=hv, nv=nv, dv=dv, shift=shift, tperm=tperm,
                             qk_norm=qk_norm)
    return pl.pallas_call(
        kern,
        out_shape=(jax.ShapeDtypeStruct((hr, s, HEAD_DIM), BF16),
                   jax.ShapeDtypeStruct((ht, HEAD_DIM, s), BF16),
                   jax.ShapeDtypeStruct((s, hv * nv), BF16)),
        grid=(s // tm,),
        in_specs=[pl.BlockSpec((tm, d), lambda i: (i, 0)), full(g), full(wr), full(wtt), full(wv_p),
                  pl.BlockSpec((tm, LANES), lambda i: (i, 0)), pl.BlockSpec((tm, LANES), lambda i: (i, 0)),
                  pl.BlockSpec((tm, LANES), lambda i: (i, 0)),
                  pl.BlockSpec((HEAD_DIM, tm), lambda i: (0, i)), pl.BlockSpec((HEAD_DIM, tm), lambda i: (0, i)),
                  full(rg), full(tg)],
        out_specs=(pl.BlockSpec((hr, tm, HEAD_DIM), lambda i: (0, i, 0)),
                   pl.BlockSpec((ht, HEAD_DIM, tm), lambda i: (0, 0, i)),
                   pl.BlockSpec((tm, hv * nv), lambda i: (i, 0))),
        compiler_params=_cparams(("parallel",)),
        name="proj",
    )(h, g, wr, wtt, wv_p, rc, rs1, rs2, tc, ts, rg, tg)


def _rope_tables(ang_list, s, scale):
    cs, sn1, sn2, ksn = [], [], [], []
    used = 0
    for ang in ang_list:
        c, sn = jnp.cos(ang), jnp.sin(ang)
        z = jnp.zeros_like(sn)
        cs += [c, c]
        sn1 += [-sn, z]
        sn2 += [z, sn]
        ksn += [-sn, sn]
        used += 2 * ang.shape[1]
    rest = HEAD_DIM - used
    ones, zeros = jnp.ones((s, rest), F32), jnp.zeros((s, rest), F32)
    c64 = jnp.concatenate(cs + [ones], axis=1)
    twice = lambda parts: jnp.concatenate(parts + parts, axis=1)
    row = (twice([c64]), twice(sn1 + [zeros]), twice(sn2 + [zeros]))
    tr = (c64.T, jnp.concatenate(ksn + [zeros], axis=1).T)
    return {"row_scaled": tuple(t * scale for t in row), "t": tr}


def _lane_tile(x, width):
    reps = width // LANES
    return x if reps == 1 else jnp.concatenate([x] * reps, axis=1)


def _flash_kernel(*refs, streams, split, tq, tk, nv, dv, band, halo, seq, mode, lam_init, has_sink):
    it = iter(refs)
    q_ref, kt_ref, v_ref = next(it), next(it), next(it)
    sink_ref = next(it) if has_sink else None
    lam_ref = subln_ref = None
    if mode == "diff":
        lam_ref, subln_ref = next(it), next(it)
    o_ref = next(it)
    lse_ref = next(it) if mode == "lse" else None
    m_scr, acc_scr = next(it), next(it)

    qb, kb = pl.program_id(1), pl.program_id(2)
    nsteps = pl.num_programs(2)
    cr = tq // split

    @pl.when(kb == 0)
    def _init():
        for si, (qi, _, _) in enumerate(streams):
            if has_sink:
                sink2 = jnp.full((tq, LANES), LOG2E, F32) * sink_ref[pl.program_id(0), qi]
                m0 = jnp.maximum(sink2, M_FLOOR)
                m_scr[si] = m0
                lane = lax.broadcasted_iota(jnp.int32, (tq, nv), 1)
                acc_scr[si] = jnp.where(lane == dv, _lane_tile(jnp.exp2(sink2 - m0), nv), 0.0)
            else:
                m_scr[si] = jnp.full((tq, LANES), M_FLOOR, F32)
                acc_scr[si] = jnp.zeros((tq, nv), F32)

    def _step():
        chunks = [(si, c * cr, qi, ki, vi) for si, (qi, ki, vi) in enumerate(streams) for c in range(split)]
        scores = [jnp.dot(q_ref[qi, r0:r0 + cr, :], kt_ref[ki], preferred_element_type=F32)
                  for _, r0, qi, ki, _ in chunks]
        for sc, (si, r0, _, _, vi) in zip(scores, chunks):
            if band is not None:
                halfw, dil, off = band
                row = lax.broadcasted_iota(jnp.int32, (cr, tk), 0) + r0
                col = lax.broadcasted_iota(jnp.int32, (cr, tk), 1)
                diff = col - row + kshift
                valid = jnp.abs(diff) <= halfw
                if dil > 1:
                    valid = valid & ((diff & (dil - 1)) == 0)
                sc = jnp.where(valid, sc, NEG_INF)
            m_old = m_scr[si, r0:r0 + cr, :]
            m_new = jnp.maximum(m_old, jnp.broadcast_to(jnp.max(sc, axis=1, keepdims=True), (cr, LANES)))
            p = jnp.exp2(sc - _lane_tile(m_new, tk))
            alpha = jnp.exp2(m_old - m_new)
            pv = jnp.dot(p.astype(BF16), v_ref[:, vi * nv:(vi + 1) * nv], preferred_element_type=F32)
            acc_scr[si, r0:r0 + cr, :] = acc_scr[si, r0:r0 + cr, :] * _lane_tile(alpha, nv) + pv
            m_scr[si, r0:r0 + cr, :] = m_new

    if band is None:
        _step()
    elif halo is not None:
        kshift = jnp.clip(qb * tq - halo, 0, seq - tk) - qb * tq
        _step()
    else:
        _, _, off = band
        kshift = (kb - off) * tk
        kabs = qb * (tq // tk) + kb - off
        pl.when((kabs >= 0) & (kabs < seq // tk))(_step)

    @pl.when(kb == nsteps - 1)
    def _fin():
        if mode == "diff":
            a0, a1 = acc_scr[0], acc_scr[1]
            lam_rows = lam_ref[...]
            lam = (jnp.exp(jnp.sum(lam_rows[0:1] * lam_rows[1:2], axis=1, keepdims=True))
                   - jnp.exp(jnp.sum(lam_rows[2:3] * lam_rows[3:4], axis=1, keepdims=True)) + lam_init)
            o = a0[:, :dv] / a0[:, dv:dv + 1] - lam * (a1[:, :dv] / a1[:, dv:dv + 1])
            o = _rms(o, subln_ref[...]) * (1.0 - lam_init)
            o_ref[...] = o.astype(o_ref.dtype)
        else:
            for si in range(len(streams)):
                a = acc_scr[si]
                l = a[:, dv:dv + 1]
                o_ref[:, si * dv:(si + 1) * dv] = (a[:, :dv] / l).astype(o_ref.dtype)
                if mode == "lse":
                    lse_ref[:, si * dv:(si + 1) * dv] = m_scr[si][:, :dv] + jnp.log2(l)


def _flash(q, kt, v, *, units, q_per_unit, k_per_unit, v_per_unit, streams, tq, tk, nv, dv, out_w,
           split=1, unit0=0, band=None, halo=None, mode="gqa", sink=None, lam=None, subln=None, lam_init=0.0,
           name="flash"):
    s = q.shape[1]
    tq, tk = min(tq, s), min(tk, s)
    if halo is not None and tq + 2 * halo > s:
        halo, tq = None, max(tq, tk)
    if halo is not None:
        tk = tq + 2 * halo
    assert s % tq == 0 and (halo is not None or (s % tk == 0 and (band is None or tq % tk == 0)))
    n_kblocks = s // tk
    kspec = vspec = None
    if band is None:
        nsteps = n_kblocks
        kmap = lambda u, i, j: (u + unit0, 0, j)
        vmap = lambda u, i, j: (j, u + unit0)
    elif halo is not None:
        assert tq % LANES == 0 and halo % LANES == 0
        nsteps = 1
        band = (band[0], band[1], 0)
        koff = lambda i: jnp.clip(i * (tq // LANES) - halo // LANES, 0, (s - tk) // LANES) * LANES
        kspec = pl.BlockSpec((pl.Element(k_per_unit), pl.Element(HEAD_DIM), pl.Element(tk)),
                             lambda u, i, j: ((u + unit0) * k_per_unit, 0, koff(i)))
        vspec = pl.BlockSpec((pl.Element(tk), pl.Element(v_per_unit * nv)),
                             lambda u, i, j: (koff(i), (u + unit0) * v_per_unit * nv))
    else:
        halfw = band[0]
        off = -(-halfw // tk)
        nsteps = tq // tk + 2 * off
        band = (band[0], band[1], off)
        kidx = lambda i, j: jnp.clip(i * (tq // tk) + j - off, 0, n_kblocks - 1)
        kmap = lambda u, i, j: (u + unit0, 0, kidx(i, j))
        vmap = lambda u, i, j: (kidx(i, j), u + unit0)
    in_specs = [pl.BlockSpec((q_per_unit, tq, HEAD_DIM), lambda u, i, j: (u + unit0, i, 0)),
                kspec or pl.BlockSpec((k_per_unit, HEAD_DIM, tk), kmap),
                vspec or pl.BlockSpec((tk, v_per_unit * nv), vmap)]
    args = [q, kt, v]
    if sink is not None:
        in_specs.append(pl.BlockSpec(memory_space=pltpu.SMEM))
        args.append(sink)
    if mode == "diff":
        in_specs += [pl.BlockSpec(lam.shape, lambda u, i, j: (0, 0)), pl.BlockSpec(subln.shape, lambda u, i, j: (0, 0))]
        args += [lam, subln]
    out_shape = [jax.ShapeDtypeStruct((s, units * out_w), BF16)]
    out_specs = [pl.BlockSpec((tq, out_w), lambda u, i, j: (i, u))]
    if mode == "lse":
        out_shape.append(jax.ShapeDtypeStruct((s, units * out_w), F32))
        out_specs.append(pl.BlockSpec((tq, out_w), lambda u, i, j: (i, u)))
    assert tq % split == 0
    kern = functools.partial(_flash_kernel, streams=streams, split=split, tq=tq, tk=tk, nv=nv, dv=dv, band=band,
                             halo=halo, seq=s, mode=mode, lam_init=lam_init, has_sink=sink is not None)
    res = pl.pallas_call(
        kern,
        out_shape=tuple(out_shape),
        grid=(units, s // tq, nsteps),
        in_specs=in_specs,
        out_specs=tuple(out_specs),
        scratch_shapes=[pltpu.VMEM((len(streams), tq, LANES), F32), pltpu.VMEM((len(streams), tq, nv), F32)],
        compiler_params=_cparams(("parallel", "parallel", "arbitrary")),
        name=name,
    )(*args)
    return res if mode == "lse" else res[0]


def _outproj_kernel(h_ref, o_ref, w_ref, out_ref):
    out_ref[...] = h_ref[...] + jnp.dot(o_ref[...], w_ref[...], preferred_element_type=F32)


def _outproj(h, o, w, tm=512):
    s, d = h.shape
    tm = min(tm, s)
    w = w.astype(BF16)
    return pl.pallas_call(
        _outproj_kernel,
        out_shape=jax.ShapeDtypeStruct((s, d), F32),
        grid=(s // tm,),
        in_specs=[pl.BlockSpec((tm, d), lambda i: (i, 0)), pl.BlockSpec((tm, o.shape[1]), lambda i: (i, 0)),
                  pl.BlockSpec(w.shape, lambda i: (0, 0))],
        out_specs=pl.BlockSpec((tm, d), lambda i: (i, 0)),
        compiler_params=_cparams(("parallel",)),
        name="outproj",
    )(h, o, w)


def _outproj_groups_kernel(h_ref, o0_ref, o1_ref, o2_ref, l0_ref, l1_ref, l2_ref, w_ref, out_ref):
    l0, l1, l2 = l0_ref[...], l1_ref[...], l2_ref[...]
    mx = jnp.maximum(jnp.maximum(l0, l1), l2)
    e0, e1, e2 = jnp.exp2(l0 - mx), jnp.exp2(l1 - mx), jnp.exp2(l2 - mx)
    tot = e0 + e1 + e2
    acc = h_ref[...]
    gw = o0_ref.shape[1]
    for g, (o_ref, e) in enumerate(((o0_ref, e0), (o1_ref, e1), (o2_ref, e2))):
        og = (o_ref[...].astype(F32) * (e / tot)).astype(BF16)
        acc = acc + jnp.dot(og, w_ref[g * gw:(g + 1) * gw, :], preferred_element_type=F32)
    out_ref[...] = acc


def _outproj_groups(h, os_, ls_, w, tm=512):
    s, d = h.shape
    tm = min(tm, s)
    w = w.astype(BF16)
    gw = os_[0].shape[1]
    row = lambda width: pl.BlockSpec((tm, width), lambda i: (i, 0))
    return pl.pallas_call(
        _outproj_groups_kernel,
        out_shape=jax.ShapeDtypeStruct((s, d), F32),
        grid=(s // tm,),
        in_specs=[row(d)] + [row(gw)] * 6 + [pl.BlockSpec(w.shape, lambda i: (0, 0))],
        out_specs=row(d),
        compiler_params=_cparams(("parallel",)),
        name="outproj_groups",
    )(h, *os_, *ls_, w)


def _route(logits):
    lane = lax.broadcasted_iota(jnp.int32, logits.shape, 1).astype(F32)
    big = 1e6
    gl = jnp.where(lane < MOE_GROUPS, logits, NEG_INF)
    gmax = jnp.max(gl, axis=1, keepdims=True)
    gidx = jnp.min(jnp.where(gl == gmax, lane, big), axis=1, keepdims=True)
    gw = 1.0 / jnp.sum(jnp.exp(gl - gmax), axis=1, keepdims=True)
    lo = MOE_GROUPS + gidx * MOE_PER_GROUP
    el = jnp.where((lane >= lo) & (lane < lo + MOE_PER_GROUP), logits, NEG_INF)
    v1 = jnp.max(el, axis=1, keepdims=True)
    i1 = jnp.min(jnp.where(el == v1, lane, big), axis=1, keepdims=True)
    el2 = jnp.where(lane == i1, NEG_INF, el)
    v2 = jnp.max(el2, axis=1, keepdims=True)
    i2 = jnp.min(jnp.where(el2 == v2, lane, big), axis=1, keepdims=True)
    e2 = jnp.exp(v2 - v1)
    w1 = gw / (1.0 + e2)
    w2 = w1 * e2
    return jnp.where(lane == i1, w1, 0.0) + jnp.where(lane == i2, w2, 0.0), gidx


MOE_ROW_TILE = 512
INFO_GROUP_LANE = 0
INFO_RANK_LANE = 1
SUBLANES = 8


def _to_slabs(ref, x):
    for c in range(SUBLANES):
        ref[:, c, :] = x[:, c * LANES:(c + 1) * LANES]


def _from_slabs(ref):
    return jnp.concatenate([ref[:, c, :] for c in range(SUBLANES)], axis=1)


def _moe_route_kernel(h_ref, g_ref, wr_ref, br_ref, tri_ref, x3_ref, info_ref, cnt_ref, run_scr):
    @pl.when(pl.program_id(0) == 0)
    def _init():
        run_scr[...] = jnp.zeros_like(run_scr)

    xn = _rms(h_ref[...], g_ref[...])
    logits = jnp.dot(xn, wr_ref[...], preferred_element_type=F32, precision=lax.Precision.HIGHEST) + br_ref[...]
    comb, gidx = _route(logits)
    lane = lax.broadcasted_iota(jnp.int32, comb.shape, 1).astype(F32)
    onehot = lane == gidx
    before = jnp.dot(tri_ref[...], onehot.astype(BF16), preferred_element_type=F32)
    run = run_scr[...]
    rank = jnp.sum(jnp.where(onehot, before + run, 0.0), axis=1, keepdims=True)
    run = run + jnp.sum(onehot.astype(F32), axis=0, keepdims=True)
    run_scr[...] = run
    cnt_ref[...] = run
    _to_slabs(x3_ref, xn)
    info = jnp.where(lane == INFO_GROUP_LANE, gidx, jnp.where(lane == INFO_RANK_LANE, rank, comb))
    info_ref[...] = info
    x3_ref[:, SUBLANES, :] = info
    for c in range(SUBLANES + 1, 2 * SUBLANES):
        x3_ref[:, c, :] = jnp.zeros_like(info)


def _moe_dispatch_kernel(pos_ref, x3_ref, xs_in_ref, xs_ref, sem):
    del xs_in_ref
    tm = x3_ref.shape[0]

    def issue(t, c):
        pltpu.make_async_copy(x3_ref.at[pl.ds(t, 1)], xs_ref.at[pl.ds(pos_ref[0, t], 1)], sem).start()
        return c

    lax.fori_loop(0, tm, issue, 0, unroll=8)
    pltpu.make_async_copy(x3_ref, xs_ref.at[pl.ds(0, tm)], sem).wait()


def _moe_expert_kernel(tg_ref, nv_ref, xs_ref, wg_ref, wu_ref, wd_ref, y_ref):
    j = pl.program_id(0)

    @pl.when(j < nv_ref[0])
    def _live():
        x = _from_slabs(xs_ref).astype(BF16)
        info = xs_ref[:, SUBLANES, :]
        lane = lax.broadcasted_iota(jnp.int32, info.shape, 1)
        first = tg_ref[j] * MOE_PER_GROUP + MOE_GROUPS
        acc = None
        for e in range(MOE_PER_GROUP):
            we = jnp.sum(jnp.where(lane == first + e, info, 0.0), axis=1, keepdims=True)
            gate = jnp.dot(x, wg_ref[e], preferred_element_type=F32)
            up = jnp.dot(x, wu_ref[e], preferred_element_type=F32)
            act = gate * jax.nn.sigmoid(gate) * up * we
            part = jnp.dot(act.astype(BF16), wd_ref[e], preferred_element_type=F32)
            acc = part if acc is None else acc + part
        _to_slabs(y_ref, acc)

    @pl.when(j >= nv_ref[0])
    def _dead():
        y_ref[...] = jnp.zeros_like(y_ref)


def _moe_sparse(h, g, w_group, b_group, w_expert, b_expert, w_gate, w_up, w_down, tm=1024, tmd=512):
    s, d = h.shape
    assert d == SUBLANES * LANES
    tm, tmd, tb = min(tm, s), min(tmd, s), min(MOE_ROW_TILE, s)
    ff = w_gate.shape[2]
    nr = MOE_GROUPS + MOE_EXPERTS
    wr = jnp.pad(jnp.concatenate([w_group, w_expert], axis=1), ((0, 0), (0, LANES - nr)))
    br = jnp.pad(jnp.concatenate([b_group, b_expert]), (0, LANES - nr)).reshape(1, LANES)
    tri = (lax.broadcasted_iota(jnp.int32, (tm, tm), 0) > lax.broadcasted_iota(jnp.int32, (tm, tm), 1)).astype(BF16)
    const = lambda a: pl.BlockSpec(a.shape, lambda i: (0, 0))
    slab = lambda rows: pl.BlockSpec((rows, 2 * SUBLANES, LANES), lambda i: (i, 0, 0))
    x3, info, cnt = pl.pallas_call(
        _moe_route_kernel,
        out_shape=(jax.ShapeDtypeStruct((s, 2 * SUBLANES, LANES), F32), jax.ShapeDtypeStruct((s, LANES), F32),
                   jax.ShapeDtypeStruct((1, LANES), F32)),
        grid=(s // tm,),
        in_specs=[pl.BlockSpec((tm, d), lambda i: (i, 0)), const(g), const(wr), const(br), const(tri)],
        out_specs=(slab(tm), pl.BlockSpec((tm, LANES), lambda i: (i, 0)), pl.BlockSpec((1, LANES), lambda i: (0, 0))),
        scratch_shapes=[pltpu.VMEM((1, LANES), F32)],
        compiler_params=_cparams(("arbitrary",)),
        name="moe_route",
    )(h, g, wr, br, tri)

    counts = cnt[0, :MOE_GROUPS].astype(jnp.int32)
    padded = ((counts + tb - 1) // tb) * tb
    ends = jnp.cumsum(padded)
    starts = ends - padded
    tok_group = info[:, INFO_GROUP_LANE].astype(jnp.int32)
    tok_rank = info[:, INFO_RANK_LANE].astype(jnp.int32)
    pos = starts[tok_group] + tok_rank
    n_tiles = s // tb + MOE_GROUPS
    p_rows = n_tiles * tb
    tile_group = jnp.minimum(jnp.sum((jnp.arange(n_tiles) * tb)[:, None] >= ends[None, :], axis=1), MOE_GROUPS - 1)
    n_valid = (ends[-1] // tb).reshape(1)

    anyspec = pl.BlockSpec(memory_space=pl.ANY)
    xs = pl.pallas_call(
        _moe_dispatch_kernel,
        out_shape=jax.ShapeDtypeStruct((p_rows, 2 * SUBLANES, LANES), F32),
        grid=(s // tmd,),
        in_specs=[pl.BlockSpec((None, 1, tmd), lambda i: (i, 0, 0), memory_space=pltpu.SMEM), slab(tmd), anyspec],
        out_specs=anyspec,
        scratch_shapes=[pltpu.SemaphoreType.DMA],
        input_output_aliases={2: 0},
        compiler_params=_cparams(("arbitrary",)),
        name="moe_dispatch",
    )(pos.reshape(s // tmd, 1, tmd), x3, jnp.zeros((p_rows, 2 * SUBLANES, LANES), F32))

    wg, wu, wd = w_gate.astype(BF16), w_up.astype(BF16), w_down.astype(BF16)
    wsel = lambda j, tg, nv: (jnp.where(j < nv[0], tg[j], MOE_GROUPS - 1), 0, 0)
    y = pl.pallas_call(
        _moe_expert_kernel,
        out_shape=jax.ShapeDtypeStruct((p_rows, SUBLANES, LANES), F32),
        grid_spec=pltpu.PrefetchScalarGridSpec(
            num_scalar_prefetch=2,
            grid=(n_tiles,),
            in_specs=[pl.BlockSpec((tb, 2 * SUBLANES, LANES), lambda j, tg, nv: (j, 0, 0)),
                      pl.BlockSpec((MOE_PER_GROUP, d, ff), wsel), pl.BlockSpec((MOE_PER_GROUP, d, ff), wsel),
                      pl.BlockSpec((MOE_PER_GROUP, ff, d), wsel)],
            out_specs=pl.BlockSpec((tb, SUBLANES, LANES), lambda j, tg, nv: (j, 0, 0))),
        compiler_params=_cparams(("arbitrary",)),
        name="moe_expert",
    )(tile_group.astype(jnp.int32), n_valid.astype(jnp.int32), xs, wg, wu, wd)
    return y, pos


def _ple_kernel(pos_ref, posn_ref, h_ref, y_hbm, g_ref, wg_ref, p_ref, wp_ref, gf_ref, out_ref, ybuf, sem, *, final):
    tm = h_ref.shape[0]
    i, n = pl.program_id(0), pl.num_programs(0)
    slot = lax.rem(i, 2)

    def gather(idx_ref, dst_slot):
        def issue(t, c):
            pltpu.make_async_copy(y_hbm.at[pl.ds(idx_ref[0, t], 1)], ybuf.at[dst_slot, pl.ds(t, 1)],
                                  sem.at[dst_slot]).start()
            return c
        lax.fori_loop(0, tm, issue, 0, unroll=8)

    @pl.when(i == 0)
    def _first():
        gather(pos_ref, slot)

    @pl.when(i + 1 < n)
    def _next():
        gather(posn_ref, 1 - slot)

    pltpu.make_async_copy(y_hbm.at[pl.ds(0, tm)], ybuf.at[slot], sem.at[slot]).wait()
    x = h_ref[...] + _from_slabs(ybuf.at[slot])
    xn = _rms(x, g_ref[...]).astype(BF16)
    gate = jax.nn.sigmoid(jnp.dot(xn, wg_ref[...], preferred_element_type=F32))
    proj = jnp.dot(p_ref[...].astype(BF16), wp_ref[...], preferred_element_type=F32)
    y = x + gate * proj
    if final:
        y = _rms(y, gf_ref[...])
    out_ref[...] = y


def _ple(h, y_sorted, pos, g, wg, p, wp, gf, final, tm=512):
    s, d = h.shape
    tm = min(tm, s)
    wg, wp = wg.astype(BF16), wp.astype(BF16)
    const = lambda a: pl.BlockSpec(a.shape, lambda i: (0, 0))
    n = s // tm
    pos3 = pos.reshape(n, 1, tm)
    return pl.pallas_call(
        functools.partial(_ple_kernel, final=final),
        out_shape=jax.ShapeDtypeStruct((s, d), F32),
        grid=(n,),
        in_specs=[pl.BlockSpec((None, 1, tm), lambda i: (i, 0, 0), memory_space=pltpu.SMEM),
                  pl.BlockSpec((None, 1, tm), lambda i: (jnp.minimum(i + 1, n - 1), 0, 0), memory_space=pltpu.SMEM),
                  pl.BlockSpec((tm, d), lambda i: (i, 0)), pl.BlockSpec(memory_space=pl.ANY), const(g), const(wg),
                  pl.BlockSpec((tm, p.shape[1]), lambda i: (i, 0)), const(wp), const(gf)],
        out_specs=pl.BlockSpec((tm, d), lambda i: (i, 0)),
        scratch_shapes=[pltpu.VMEM((2, tm, SUBLANES, LANES), F32), pltpu.SemaphoreType.DMA((2,))],
        compiler_params=_cparams(("arbitrary",)),
        name="ple",
    )(pos3, pos3, h, y_sorted, g, wg, p, wp, gf)


PARTIAL_KPERM = ((8, 16), (0, 8), (16, 64))
AXIAL_KPERM = ((16, 32), (0, 16), (48, 64), (32, 48))
B_PAIRS = ((128, 1), (512, 4), (2048, 16))


def _partial_tables(s):
    inv = ROPE_THETA ** (-jnp.arange(0, ROT_DIM, 2, dtype=F32) / ROT_DIM)
    ang = jnp.arange(s).astype(F32)[:, None] * inv[None, :]
    return _rope_tables([ang], s, LOG2E * HEAD_DIM ** -0.5)


def _axial_tables(s):
    half = HEAD_DIM // 2
    inv = AXIAL_THETA ** (-jnp.arange(0, half, 2, dtype=F32) / half)
    t = jnp.arange(s)
    ang_r = (t // GRID_W).astype(F32)[:, None] * inv[None, :]
    ang_c = (t % GRID_W).astype(F32)[:, None] * inv[None, :]
    return _rope_tables([ang_r, ang_c], s, LOG2E * HEAD_DIM ** -0.5)


def _mixer_diff(h, g, w_in, w_out, lam_rows, subln, lam_init, tabs):
    d = h.shape[1]
    heads = d // (2 * HEAD_DIM)
    aw = 2 * heads * HEAD_DIM
    dv = 2 * HEAD_DIM
    q, kt, v = _project(h, g, w_in[:, :aw], w_in[:, aw:2 * aw], w_in[:, 2 * aw:], tabs["row_scaled"], tabs["t"],
                        hr=2 * heads, ht=2 * heads, hv=heads, dv=dv, nv=2 * dv,
                        shift=ROT_DIM // 2, tperm=PARTIAL_KPERM, qk_norm=False)
    o = _flash(q, kt, v, units=heads, q_per_unit=2, k_per_unit=2, v_per_unit=1,
               streams=((0, 0, 0), (1, 1, 0)), split=1, tq=1024, tk=2048, nv=2 * dv, dv=dv, out_w=dv,
               mode="diff", lam=lam_rows, subln=subln, lam_init=lam_init, name="flash_diff")
    return _outproj(h, o, w_out)


def _mixer_dilated(h, g, w_in, w_out, tabs):
    nh, hg = 12, 4
    bw = nh * HEAD_DIM
    q, kt, v = _project(h, g, w_in[:, :bw], w_in[:, bw:2 * bw], w_in[:, 2 * bw:], tabs["row_scaled"], tabs["t"],
                        hr=nh, ht=nh, hv=nh, dv=HEAD_DIM, nv=LANES,
                        shift=ROT_DIM // 2, tperm=PARTIAL_KPERM, qk_norm=False)
    os_, ls_ = [], []
    for gi, (win, dil) in enumerate(B_PAIRS):
        halfw = (win // (2 * dil)) * dil
        o, lse = _flash(q, kt, v, units=1, unit0=gi, q_per_unit=hg, k_per_unit=hg, v_per_unit=hg,
                        streams=tuple((j, j, j) for j in range(hg)), tq=256, tk=max(halfw, 256),
                        nv=LANES, dv=HEAD_DIM, out_w=hg * HEAD_DIM, band=(halfw, dil),
                        halo=-(-halfw // LANES) * LANES, mode="lse",
                        name=f"flash_dilated{gi}")
        os_.append(o)
        ls_.append(lse)
    return _outproj_groups(h, os_, ls_, w_out)


GQA_HEADS, GQA_KV_HEADS = 16, 4


def _mixer_window(h, g, w_in, w_out, sink, tabs):
    qd, kvd = GQA_HEADS * HEAD_DIM, GQA_KV_HEADS * HEAD_DIM
    grp = GQA_HEADS // GQA_KV_HEADS
    q, kt, v = _project(h, g, w_in[:, :qd], w_in[:, qd:qd + kvd], w_in[:, qd + kvd:], tabs["row_scaled"], tabs["t"],
                        hr=GQA_HEADS, ht=GQA_KV_HEADS, hv=GQA_KV_HEADS, dv=HEAD_DIM, nv=LANES,
                        shift=ROT_DIM // 2, tperm=PARTIAL_KPERM, qk_norm=False)
    o = _flash(q, kt, v, units=GQA_KV_HEADS, q_per_unit=grp, k_per_unit=1, v_per_unit=1,
               streams=tuple((j, 0, 0) for j in range(grp)), tq=256, tk=256, nv=LANES, dv=HEAD_DIM,
               out_w=grp * HEAD_DIM, band=(128, 1), halo=128, mode="gqa", sink=sink.reshape(GQA_KV_HEADS, grp),
               name="flash_window")
    return _outproj(h, o, w_out)


def _mixer_axial(h, g, w_in, w_out, q_norm, k_norm, tabs):
    qd, kvd = GQA_HEADS * HEAD_DIM, GQA_KV_HEADS * HEAD_DIM
    grp = GQA_HEADS // GQA_KV_HEADS
    qg = jnp.tile(q_norm, 2).reshape(1, LANES)
    kg = k_norm.reshape(HEAD_DIM, 1)
    q, kt, v = _project(h, g, w_in[:, :qd], w_in[:, qd:qd + kvd], w_in[:, qd + kvd:], tabs["row_scaled"], tabs["t"],
                        hr=GQA_HEADS, ht=GQA_KV_HEADS, hv=GQA_KV_HEADS, dv=HEAD_DIM, nv=LANES,
                        shift=HEAD_DIM // 4, tperm=AXIAL_KPERM, qk_norm=True, rg=qg, tg=kg)
    o = _flash(q, kt, v, units=GQA_KV_HEADS, q_per_unit=grp, k_per_unit=1, v_per_unit=1,
               streams=tuple((j, 0, 0) for j in range(grp)), split=8, tq=1024, tk=2048, nv=LANES, dv=HEAD_DIM,
               out_w=grp * HEAD_DIM, mode="gqa", name="flash_axial")
    return _outproj(h, o, w_out)


def kernel(x, p, norm_mix, norm_ffn, norm_ple, norm_final, a_w_in, a_w_out, a_lam_q1, a_lam_k1, a_lam_q2, a_lam_k2, a_subln, b_w_in, b_w_out, c_w_in, c_w_out, c_sink, d_w_in, d_w_out, d_q_norm, d_k_norm, moe_w_group, moe_b_group, moe_w_expert, moe_b_expert, moe_w_gate, moe_w_up, moe_w_down, ple_w_gate, ple_w_proj):
    bn, s, d = x.shape
    assert bn == 1
    depth = p.shape[0]
    h = x[0]
    ptabs = _partial_tables(s)
    atabs = _axial_tables(s)
    row = lambda a: a.reshape(1, -1)
    for i in range(depth):
        r, kind = divmod(i, 4)
        g = row(norm_mix[i])
        if kind == 0:
            lam_init = 0.8 - 0.6 * math.exp(-0.3 * i)
            lam_rows = jnp.stack([a_lam_q1[r], a_lam_k1[r], a_lam_q2[r], a_lam_k2[r]])
            h = _mixer_diff(h, g, a_w_in[r], a_w_out[r], lam_rows, row(a_subln[r]), lam_init, ptabs)
        elif kind == 1:
            h = _mixer_dilated(h, g, b_w_in[r], b_w_out[r], ptabs)
        elif kind == 2:
            h = _mixer_window(h, g, c_w_in[r], c_w_out[r], c_sink[r], ptabs)
        else:
            h = _mixer_axial(h, g, d_w_in[r], d_w_out[r], d_q_norm[r], d_k_norm[r], atabs)
        y_sorted, pos = _moe_sparse(h, row(norm_ffn[i]), moe_w_group[i], moe_b_group[i], moe_w_expert[i],
                                    moe_b_expert[i], moe_w_gate[i], moe_w_up[i], moe_w_down[i])
        h = _ple(h, y_sorted, pos, row(norm_ple[i]), ple_w_gate[i], p[i, 0], ple_w_proj[i], row(norm_final), final=(i == depth - 1))
    return h[None]
```

```python
import functools
import math

import jax
import jax.numpy as jnp
from jax import lax
from jax.experimental import pallas as pl
from jax.experimental.pallas import tpu as pltpu

F32 = jnp.float32
BF16 = jnp.bfloat16

HEAD_DIM = 64
LANES = 128
EPS = 1e-6
LOG2E = 1.4426950408889634
NEG_INF = -1e30
M_FLOOR = -1e29
ROPE_THETA = 500000.0
ROT_DIM = HEAD_DIM // 4
AXIAL_THETA = 10000.0
GRID_W = 64
MOE_GROUPS = 4
MOE_PER_GROUP = 4
MOE_EXPERTS = 16
VMEM_LIMIT = 56 * 1024 * 1024


def _cparams(sem):
    return pltpu.CompilerParams(dimension_semantics=sem, vmem_limit_bytes=VMEM_LIMIT)


def _rms(x, g):
    return x * lax.rsqrt(jnp.mean(x * x, axis=-1, keepdims=True) + EPS) * g


def _proj_kernel(h_ref, g_ref, wr_ref, wtt_ref, wv_ref, rc_ref, rs1_ref, rs2_ref, tc_ref, ts_ref,
                 rg_ref, tg_ref, r_ref, t_ref, v_ref, *, hr, ht, hv, nv, dv, shift, tperm, qk_norm):
    xn = _rms(h_ref[...], g_ref[...]).astype(BF16)
    rf = jnp.dot(xn, wr_ref[...], preferred_element_type=F32)
    rc, rs1, rs2 = rc_ref[...], rs1_ref[...], rs2_ref[...]
    low = lax.broadcasted_iota(jnp.int32, (rf.shape[0], LANES), 1) < HEAD_DIM
    for j in range(hr // 2):
        s = rf[:, j * LANES:(j + 1) * LANES]
        if qk_norm:
            sq = s * s
            ms = jnp.where(low, jnp.sum(jnp.where(low, sq, 0.0), axis=-1, keepdims=True),
                           jnp.sum(jnp.where(low, 0.0, sq), axis=-1, keepdims=True)) * (1.0 / HEAD_DIM)
            s = s * lax.rsqrt(ms + EPS) * rg_ref[...]
        r = s * rc + pltpu.roll(s, LANES - shift, 1) * rs1 + pltpu.roll(s, shift, 1) * rs2
        r_ref[2 * j] = r[:, :HEAD_DIM].astype(BF16)
        r_ref[2 * j + 1] = r[:, HEAD_DIM:].astype(BF16)
    nt = (((1,), (1,)), ((), ()))
    tf = lax.dot_general(wtt_ref[...], xn, nt, preferred_element_type=F32)
    tc, ts = tc_ref[...], ts_ref[...]
    for h in range(ht):
        s = tf[h * HEAD_DIM:(h + 1) * HEAD_DIM, :]
        if qk_norm:
            ms = jnp.sum(s * s, axis=0, keepdims=True) * (1.0 / HEAD_DIM)
            s = s * lax.rsqrt(ms + EPS) * tg_ref[...]
        partner = jnp.concatenate([s[a:b] for a, b in tperm], axis=0)
        t_ref[h] = (s * tc + partner * ts).astype(BF16)
    vf = jnp.dot(xn, wv_ref[...], preferred_element_type=F32).astype(BF16)
    ones_col = (lax.broadcasted_iota(jnp.int32, (vf.shape[0], nv - dv), 1) == 0).astype(BF16)
    for h in range(hv):
        v_ref[:, h * nv:h * nv + dv] = vf[:, h * dv:(h + 1) * dv]
        v_ref[:, h * nv + dv:(h + 1) * nv] = ones_col


def _project(h, g, wq, wk, wv, row_tabs, t_tabs, *, hr, ht, hv, dv, nv, shift, tperm, qk_norm, rg=None, tg=None,
             tm=512):
    s, d = h.shape
    tm = min(tm, s)
    assert hr % 2 == 0
    wr = wq.astype(BF16)
    wtt = wk.T.astype(BF16)
    wv_p = wv.astype(BF16)
    rc, rs1, rs2 = row_tabs
    tc, ts = t_tabs
    if rg is None:
        rg = jnp.ones((1, LANES), F32)
        tg = jnp.ones((HEAD_DIM, 1), F32)
    full = lambda a: pl.BlockSpec(a.shape, lambda i: (0,) * a.ndim)
    kern = functools.partial(_proj_kernel, hr=hr, ht=ht, hv=hv, nv=nv, dv=dv, shift=shift, tperm=tperm,
                             qk_norm=qk_norm)
    return pl.pallas_call(
        kern,
        out_shape=(jax.ShapeDtypeStruct((hr, s, HEAD_DIM), BF16),
                   jax.ShapeDtypeStruct((ht, HEAD_DIM, s), BF16),
                   jax.ShapeDtypeStruct((s, hv * nv), BF16)),
        grid=(s // tm,),
        in_specs=[pl.BlockSpec((tm, d), lambda i: (i, 0)), full(g), full(wr), full(wtt), full(wv_p),
                  pl.BlockSpec((tm, LANES), lambda i: (i, 0)), pl.BlockSpec((tm, LANES), lambda i: (i, 0)),
                  pl.BlockSpec((tm, LANES), lambda i: (i, 0)),
                  pl.BlockSpec((HEAD_DIM, tm), lambda i: (0, i)), pl.BlockSpec((HEAD_DIM, tm), lambda i: (0, i)),
                  full(rg), full(tg)],
        out_specs=(pl.BlockSpec((hr, tm, HEAD_DIM), lambda i: (0, i, 0)),
                   pl.BlockSpec((ht, HEAD_DIM, tm), lambda i: (0, 0, i)),
                   pl.BlockSpec((tm, hv * nv), lambda i: (i, 0))),
        compiler_params=_cparams(("parallel",)),
        name="proj",
    )(h, g, wr, wtt, wv_p, rc, rs1, rs2, tc, ts, rg, tg)


def _rope_tables(ang_list, s, scale):
    cs, sn1, sn2, ksn = [], [], [], []
    used = 0
    for ang in ang_list:
        c, sn = jnp.cos(ang), jnp.sin(ang)
        z = jnp.zeros_like(sn)
        cs += [c, c]
        sn1 += [-sn, z]
        sn2 += [z, sn]
        ksn += [-sn, sn]
        used += 2 * ang.shape[1]
    rest = HEAD_DIM - used
    ones, zeros = jnp.ones((s, rest), F32), jnp.zeros((s, rest), F32)
    c64 = jnp.concatenate(cs + [ones], axis=1)
    twice = lambda parts: jnp.concatenate(parts + parts, axis=1)
    row = (twice([c64]), twice(sn1 + [zeros]), twice(sn2 + [zeros]))
    tr = (c64.T, jnp.concatenate(ksn + [zeros], axis=1).T)
    return {"row_scaled": tuple(t * scale for t in row), "t": tr}


def _lane_tile(x, width):
    reps = width // LANES
    return x if reps == 1 else jnp.concatenate([x] * reps, axis=1)


def _flash_kernel(*refs, streams, split, tq, tk, nv, dv, band, halo, seq, mode, lam_init, has_sink):
    it = iter(refs)
    q_ref, kt_ref, v_ref = next(it), next(it), next(it)
    sink_ref = next(it) if has_sink else None
    lam_ref = subln_ref = None
    if mode == "diff":
        lam_ref, subln_ref = next(it), next(it)
    o_ref = next(it)
    lse_ref = next(it) if mode == "lse" else None
    m_scr, acc_scr = next(it), next(it)

    qb, kb = pl.program_id(1), pl.program_id(2)
    nsteps = pl.num_programs(2)
    cr = tq // split

    @pl.when(kb == 0)
    def _init():
        for si, (qi, _, _) in enumerate(streams):
            if has_sink:
                sink2 = jnp.full((tq, LANES), LOG2E, F32) * sink_ref[pl.program_id(0), qi]
                m0 = jnp.maximum(sink2, M_FLOOR)
                m_scr[si] = m0
                lane = lax.broadcasted_iota(jnp.int32, (tq, nv), 1)
                acc_scr[si] = jnp.where(lane == dv, _lane_tile(jnp.exp2(sink2 - m0), nv), 0.0)
            else:
                m_scr[si] = jnp.full((tq, LANES), M_FLOOR, F32)
                acc_scr[si] = jnp.zeros((tq, nv), F32)

    def _step():
        chunks = [(si, c * cr, qi, ki, vi) for si, (qi, ki, vi) in enumerate(streams) for c in range(split)]
        scores = [jnp.dot(q_ref[qi, r0:r0 + cr, :], kt_ref[ki], preferred_element_type=F32)
                  for _, r0, qi, ki, _ in chunks]
        for sc, (si, r0, _, _, vi) in zip(scores, chunks):
            if band is not None:
                halfw, dil, off = band
                row = lax.broadcasted_iota(jnp.int32, (cr, tk), 0) + r0
                col = lax.broadcasted_iota(jnp.int32, (cr, tk), 1)
                diff = col - row + kshift
                valid = jnp.abs(diff) <= halfw
                if dil > 1:
                    valid = valid & ((diff & (dil - 1)) == 0)
                sc = jnp.where(valid, sc, NEG_INF)
            m_old = m_scr[si, r0:r0 + cr, :]
            m_new = jnp.maximum(m_old, jnp.broadcast_to(jnp.max(sc, axis=1, keepdims=True), (cr, LANES)))
            p = jnp.exp2(sc - _lane_tile(m_new, tk))
            alpha = jnp.exp2(m_old - m_new)
            pv = jnp.dot(p.astype(BF16), v_ref[:, vi * nv:(vi + 1) * nv], preferred_element_type=F32)
            acc_scr[si, r0:r0 + cr, :] = acc_scr[si, r0:r0 + cr, :] * _lane_tile(alpha, nv) + pv
            m_scr[si, r0:r0 + cr, :] = m_new

    if band is None:
        _step()
    elif halo is not None:
        kshift = jnp.clip(qb * tq - halo, 0, seq - tk) - qb * tq
        _step()
    else:
        _, _, off = band
        kshift = (kb - off) * tk
        kabs = qb * (tq // tk) + kb - off
        pl.when((kabs >= 0) & (kabs < seq // tk))(_step)

    @pl.when(kb == nsteps - 1)
    def _fin():
        if mode == "diff":
            a0, a1 = acc_scr[0], acc_scr[1]
            lam_rows = lam_ref[...]
            lam = (jnp.exp(jnp.sum(lam_rows[0:1] * lam_rows[1:2], axis=1, keepdims=True))
                   - jnp.exp(jnp.sum(lam_rows[2:3] * lam_rows[3:4], axis=1, keepdims=True)) + lam_init)
            o = a0[:, :dv] / a0[:, dv:dv + 1] - lam * (a1[:, :dv] / a1[:, dv:dv + 1])
            o = _rms(o, subln_ref[...]) * (1.0 - lam_init)
            o_ref[...] = o.astype(o_ref.dtype)
        else:
            for si in range(len(streams)):
                a = acc_scr[si]
                l = a[:, dv:dv + 1]
                o_ref[:, si * dv:(si + 1) * dv] = (a[:, :dv] / l).astype(o_ref.dtype)
                if mode == "lse":
                    lse_ref[:, si * dv:(si + 1) * dv] = m_scr[si][:, :dv] + jnp.log2(l)


def _flash(q, kt, v, *, units, q_per_unit, k_per_unit, v_per_unit, streams, tq, tk, nv, dv, out_w,
           split=1, unit0=0, band=None, halo=None, mode="gqa", sink=None, lam=None, subln=None, lam_init=0.0,
           name="flash"):
    s = q.shape[1]
    tq, tk = min(tq, s), min(tk, s)
    if halo is not None and tq + 2 * halo > s:
        halo, tq = None, max(tq, tk)
    if halo is not None:
        tk = tq + 2 * halo
    assert s % tq == 0 and (halo is not None or (s % tk == 0 and (band is None or tq % tk == 0)))
    n_kblocks = s // tk
    kspec = vspec = None
    if band is None:
        nsteps = n_kblocks
        kmap = lambda u, i, j: (u + unit0, 0, j)
        vmap = lambda u, i, j: (j, u + unit0)
    elif halo is not None:
        assert tq % LANES == 0 and halo % LANES == 0
        nsteps = 1
        band = (band[0], band[1], 0)
        koff = lambda i: jnp.clip(i * (tq // LANES) - halo // LANES, 0, (s - tk) // LANES) * LANES
        kspec = pl.BlockSpec((pl.Element(k_per_unit), pl.Element(HEAD_DIM), pl.Element(tk)),
                             lambda u, i, j: ((u + unit0) * k_per_unit, 0, koff(i)))
        vspec = pl.BlockSpec((pl.Element(tk), pl.Element(v_per_unit * nv)),
                             lambda u, i, j: (koff(i), (u + unit0) * v_per_unit * nv))
    else:
        halfw = band[0]
        off = -(-halfw // tk)
        nsteps = tq // tk + 2 * off
        band = (band[0], band[1], off)
        kidx = lambda i, j: jnp.clip(i * (tq // tk) + j - off, 0, n_kblocks - 1)
        kmap = lambda u, i, j: (u + unit0, 0, kidx(i, j))
        vmap = lambda u, i, j: (kidx(i, j), u + unit0)
    in_specs = [pl.BlockSpec((q_per_unit, tq, HEAD_DIM), lambda u, i, j: (u + unit0, i, 0)),
                kspec or pl.BlockSpec((k_per_unit, HEAD_DIM, tk), kmap),
                vspec or pl.BlockSpec((tk, v_per_unit * nv), vmap)]
    args = [q, kt, v]
    if sink is not None:
        in_specs.append(pl.BlockSpec(memory_space=pltpu.SMEM))
        args.append(sink)
    if mode == "diff":
        in_specs += [pl.BlockSpec(lam.shape, lambda u, i, j: (0, 0)), pl.BlockSpec(subln.shape, lambda u, i, j: (0, 0))]
        args += [lam, subln]
    out_shape = [jax.ShapeDtypeStruct((s, units * out_w), BF16)]
    out_specs = [pl.BlockSpec((tq, out_w), lambda u, i, j: (i, u))]
    if mode == "lse":
        out_shape.append(jax.ShapeDtypeStruct((s, units * out_w), F32))
        out_specs.append(pl.BlockSpec((tq, out_w), lambda u, i, j: (i, u)))
    assert tq % split == 0
    kern = functools.partial(_flash_kernel, streams=streams, split=split, tq=tq, tk=tk, nv=nv, dv=dv, band=band,
                             halo=halo, seq=s, mode=mode, lam_init=lam_init, has_sink=sink is not None)
    res = pl.pallas_call(
        kern,
        out_shape=tuple(out_shape),
        grid=(units, s // tq, nsteps),
        in_specs=in_specs,
        out_specs=tuple(out_specs),
        scratch_shapes=[pltpu.VMEM((len(streams), tq, LANES), F32), pltpu.VMEM((len(streams), tq, nv), F32)],
        compiler_params=_cparams(("parallel", "parallel", "arbitrary")),
        name=name,
    )(*args)
    return res if mode == "lse" else res[0]


def _outproj_kernel(h_ref, o_ref, w_ref, out_ref):
    out_ref[...] = h_ref[...] + jnp.dot(o_ref[...], w_ref[...], preferred_element_type=F32)


def _outproj(h, o, w, tm=512):
    s, d = h.shape
    tm = min(tm, s)
    w = w.astype(BF16)
    return pl.pallas_call(
        _outproj_kernel,
        out_shape=jax.ShapeDtypeStruct((s, d), F32),
        grid=(s // tm,),
        in_specs=[pl.BlockSpec((tm, d), lambda i: (i, 0)), pl.BlockSpec((tm, o.shape[1]), lambda i: (i, 0)),
                  pl.BlockSpec(w.shape, lambda i: (0, 0))],
        out_specs=pl.BlockSpec((tm, d), lambda i: (i, 0)),
        compiler_params=_cparams(("parallel",)),
        name="outproj",
    )(h, o, w)


def _outproj_groups_kernel(h_ref, o0_ref, o1_ref, o2_ref, l0_ref, l1_ref, l2_ref, w_ref, out_ref):
    l0, l1, l2 = l0_ref[...], l1_ref[...], l2_ref[...]
    mx = jnp.maximum(jnp.maximum(l0, l1), l2)
    e0, e1, e2 = jnp.exp2(l0 - mx), jnp.exp2(l1 - mx), jnp.exp2(l2 - mx)
    tot = e0 + e1 + e2
    acc = h_ref[...]
    gw = o0_ref.shape[1]
    for g, (o_ref, e) in enumerate(((o0_ref, e0), (o1_ref, e1), (o2_ref, e2))):
        og = (o_ref[...].astype(F32) * (e / tot)).astype(BF16)
        acc = acc + jnp.dot(og, w_ref[g * gw:(g + 1) * gw, :], preferred_element_type=F32)
    out_ref[...] = acc


def _outproj_groups(h, os_, ls_, w, tm=512):
    s, d = h.shape
    tm = min(tm, s)
    w = w.astype(BF16)
    gw = os_[0].shape[1]
    row = lambda width: pl.BlockSpec((tm, width), lambda i: (i, 0))
    return pl.pallas_call(
        _outproj_groups_kernel,
        out_shape=jax.ShapeDtypeStruct((s, d), F32),
        grid=(s // tm,),
        in_specs=[row(d)] + [row(gw)] * 6 + [pl.BlockSpec(w.shape, lambda i: (0, 0))],
        out_specs=row(d),
        compiler_params=_cparams(("parallel",)),
        name="outproj_groups",
    )(h, *os_, *ls_, w)


def _route(logits):
    lane = lax.broadcasted_iota(jnp.int32, logits.shape, 1).astype(F32)
    big = 1e6
    gl = jnp.where(lane < MOE_GROUPS, logits, NEG_INF)
    gmax = jnp.max(gl, axis=1, keepdims=True)
    gidx = jnp.min(jnp.where(gl == gmax, lane, big), axis=1, keepdims=True)
    gw = 1.0 / jnp.sum(jnp.exp(gl - gmax), axis=1, keepdims=True)
    lo = MOE_GROUPS + gidx * MOE_PER_GROUP
    el = jnp.where((lane >= lo) & (lane < lo + MOE_PER_GROUP), logits, NEG_INF)
    v1 = jnp.max(el, axis=1, keepdims=True)
    i1 = jnp.min(jnp.where(el == v1, lane, big), axis=1, keepdims=True)
    el2 = jnp.where(lane == i1, NEG_INF, el)
    v2 = jnp.max(el2, axis=1, keepdims=True)
    i2 = jnp.min(jnp.where(el2 == v2, lane, big), axis=1, keepdims=True)
    e2 = jnp.exp(v2 - v1)
    w1 = gw / (1.0 + e2)
    w2 = w1 * e2
    return jnp.where(lane == i1, w1, 0.0) + jnp.where(lane == i2, w2, 0.0), gidx


MOE_ROW_TILE = 512
INFO_GROUP_LANE = 0
INFO_RANK_LANE = 1
SUBLANES = 8


def _to_slabs(ref, x):
    for c in range(SUBLANES):
        ref[:, c, :] = x[:, c * LANES:(c + 1) * LANES]


def _from_slabs(ref):
    return jnp.concatenate([ref[:, c, :] for c in range(SUBLANES)], axis=1)


def _moe_route_kernel(h_ref, g_ref, wr_ref, br_ref, tri_ref, x3_ref, info_ref, cnt_ref, run_scr):
    @pl.when(pl.program_id(0) == 0)
    def _init():
        run_scr[...] = jnp.zeros_like(run_scr)

    xn = _rms(h_ref[...], g_ref[...])
    logits = jnp.dot(xn, wr_ref[...], preferred_element_type=F32, precision=lax.Precision.HIGHEST) + br_ref[...]
    comb, gidx = _route(logits)
    lane = lax.broadcasted_iota(jnp.int32, comb.shape, 1).astype(F32)
    onehot = lane == gidx
    before = jnp.dot(tri_ref[...], onehot.astype(BF16), preferred_element_type=F32)
    run = run_scr[...]
    rank = jnp.sum(jnp.where(onehot, before + run, 0.0), axis=1, keepdims=True)
    run = run + jnp.sum(onehot.astype(F32), axis=0, keepdims=True)
    run_scr[...] = run
    cnt_ref[...] = run
    _to_slabs(x3_ref, xn)
    info_ref[...] = jnp.where(lane == INFO_GROUP_LANE, gidx, jnp.where(lane == INFO_RANK_LANE, rank, comb))


def _moe_dispatch_kernel(pos_ref, x3_ref, info_ref, xs_in_ref, infos_in_ref, xs_ref, infos_ref, sem):
    del xs_in_ref, infos_in_ref
    tm = x3_ref.shape[0]

    def issue(i, c):
        for prio in range(2):
            t = 2 * i + prio
            p = pos_ref[0, t]
            pltpu.make_async_copy(x3_ref.at[pl.ds(t, 1)], xs_ref.at[pl.ds(p, 1)], sem).start(priority=prio)
            pltpu.make_async_copy(info_ref.at[pl.ds(t, 1)], infos_ref.at[pl.ds(p, 1)], sem).start(priority=prio)
        return c

    lax.fori_loop(0, tm // 2, issue, 0, unroll=4)
    pltpu.make_async_copy(x3_ref, xs_ref.at[pl.ds(0, tm)], sem).wait()
    pltpu.make_async_copy(info_ref, infos_ref.at[pl.ds(0, tm)], sem).wait()


def _moe_expert_kernel(tg_ref, nv_ref, xs_ref, infos_ref, wg_ref, wu_ref, wd_ref, y_ref):
    j = pl.program_id(0)

    @pl.when(j < nv_ref[0])
    def _live():
        x = _from_slabs(xs_ref).astype(BF16)
        info = infos_ref[...]
        lane = lax.broadcasted_iota(jnp.int32, info.shape, 1)
        first = tg_ref[j] * MOE_PER_GROUP + MOE_GROUPS
        acc = None
        for e in range(MOE_PER_GROUP):
            we = jnp.sum(jnp.where(lane == first + e, info, 0.0), axis=1, keepdims=True)
            gate = jnp.dot(x, wg_ref[e], preferred_element_type=F32)
            up = jnp.dot(x, wu_ref[e], preferred_element_type=F32)
            act = gate * jax.nn.sigmoid(gate) * up * we
            part = jnp.dot(act.astype(BF16), wd_ref[e], preferred_element_type=F32)
            acc = part if acc is None else acc + part
        _to_slabs(y_ref, acc)

    @pl.when(j >= nv_ref[0])
    def _dead():
        y_ref[...] = jnp.zeros_like(y_ref)


def _moe_sparse(h, g, w_group, b_group, w_expert, b_expert, w_gate, w_up, w_down, tm=1024, tmd=512):
    s, d = h.shape
    assert d == SUBLANES * LANES
    tm, tmd, tb = min(tm, s), min(tmd, s), min(MOE_ROW_TILE, s)
    ff = w_gate.shape[2]
    nr = MOE_GROUPS + MOE_EXPERTS
    wr = jnp.pad(jnp.concatenate([w_group, w_expert], axis=1), ((0, 0), (0, LANES - nr)))
    br = jnp.pad(jnp.concatenate([b_group, b_expert]), (0, LANES - nr)).reshape(1, LANES)
    tri = (lax.broadcasted_iota(jnp.int32, (tm, tm), 0) > lax.broadcasted_iota(jnp.int32, (tm, tm), 1)).astype(BF16)
    const = lambda a: pl.BlockSpec(a.shape, lambda i: (0, 0))
    slab = lambda rows: pl.BlockSpec((rows, SUBLANES, LANES), lambda i: (i, 0, 0))
    x3, info, cnt = pl.pallas_call(
        _moe_route_kernel,
        out_shape=(jax.ShapeDtypeStruct((s, SUBLANES, LANES), F32), jax.ShapeDtypeStruct((s, LANES), F32),
                   jax.ShapeDtypeStruct((1, LANES), F32)),
        grid=(s // tm,),
        in_specs=[pl.BlockSpec((tm, d), lambda i: (i, 0)), const(g), const(wr), const(br), const(tri)],
        out_specs=(slab(tm), pl.BlockSpec((tm, LANES), lambda i: (i, 0)), pl.BlockSpec((1, LANES), lambda i: (0, 0))),
        scratch_shapes=[pltpu.VMEM((1, LANES), F32)],
        compiler_params=_cparams(("arbitrary",)),
        name="moe_route",
    )(h, g, wr, br, tri)

    counts = cnt[0, :MOE_GROUPS].astype(jnp.int32)
    padded = ((counts + tb - 1) // tb) * tb
    ends = jnp.cumsum(padded)
    starts = ends - padded
    tok_group = info[:, INFO_GROUP_LANE].astype(jnp.int32)
    tok_rank = info[:, INFO_RANK_LANE].astype(jnp.int32)
    pos = starts[tok_group] + tok_rank
    n_tiles = s // tb + MOE_GROUPS
    p_rows = n_tiles * tb
    tile_group = jnp.minimum(jnp.sum((jnp.arange(n_tiles) * tb)[:, None] >= ends[None, :], axis=1), MOE_GROUPS - 1)
    n_valid = (ends[-1] // tb).reshape(1)

    anyspec = pl.BlockSpec(memory_space=pl.ANY)
    xs, infos = pl.pallas_call(
        _moe_dispatch_kernel,
        out_shape=(jax.ShapeDtypeStruct((p_rows, SUBLANES, LANES), F32), jax.ShapeDtypeStruct((p_rows, LANES), F32)),
        grid=(s // tmd,),
        in_specs=[pl.BlockSpec((None, 1, tmd), lambda i: (i, 0, 0), memory_space=pltpu.SMEM),
                  slab(tmd), pl.BlockSpec((tmd, LANES), lambda i: (i, 0)), anyspec, anyspec],
        out_specs=(anyspec, anyspec),
        scratch_shapes=[pltpu.SemaphoreType.DMA],
        input_output_aliases={3: 0, 4: 1},
        compiler_params=_cparams(("arbitrary",)),
        name="moe_dispatch",
    )(pos.reshape(s // tmd, 1, tmd), x3, info, jnp.zeros((p_rows, SUBLANES, LANES), F32),
      jnp.zeros((p_rows, LANES), F32))

    wg, wu, wd = w_gate.astype(BF16), w_up.astype(BF16), w_down.astype(BF16)
    wsel = lambda j, tg, nv: (jnp.where(j < nv[0], tg[j], MOE_GROUPS - 1), 0, 0)
    y = pl.pallas_call(
        _moe_expert_kernel,
        out_shape=jax.ShapeDtypeStruct((p_rows, SUBLANES, LANES), F32),
        grid_spec=pltpu.PrefetchScalarGridSpec(
            num_scalar_prefetch=2,
            grid=(n_tiles,),
            in_specs=[pl.BlockSpec((tb, SUBLANES, LANES), lambda j, tg, nv: (j, 0, 0)),
                      pl.BlockSpec((tb, LANES), lambda j, tg, nv: (j, 0)),
                      pl.BlockSpec((MOE_PER_GROUP, d, ff), wsel), pl.BlockSpec((MOE_PER_GROUP, d, ff), wsel),
                      pl.BlockSpec((MOE_PER_GROUP, ff, d), wsel)],
            out_specs=pl.BlockSpec((tb, SUBLANES, LANES), lambda j, tg, nv: (j, 0, 0))),
        compiler_params=_cparams(("arbitrary",)),
        name="moe_expert",
    )(tile_group.astype(jnp.int32), n_valid.astype(jnp.int32), xs, infos, wg, wu, wd)
    return y, pos


def _ple_kernel(pos_ref, posn_ref, h_ref, y_hbm, g_ref, wg_ref, p_ref, wp_ref, gf_ref, out_ref, ybuf, sem, *, final):
    tm = h_ref.shape[0]
    i, n = pl.program_id(0), pl.num_programs(0)
    slot = lax.rem(i, 2)

    def gather(idx_ref, dst_slot):
        def issue(i, c):
            for prio in range(2):
                t = 2 * i + prio
                pltpu.make_async_copy(y_hbm.at[pl.ds(idx_ref[0, t], 1)], ybuf.at[dst_slot, pl.ds(t, 1)],
                                      sem.at[dst_slot]).start(priority=prio)
            return c
        lax.fori_loop(0, tm // 2, issue, 0, unroll=4)

    @pl.when(i == 0)
    def _first():
        gather(pos_ref, slot)

    @pl.when(i + 1 < n)
    def _next():
        gather(posn_ref, 1 - slot)

    pltpu.make_async_copy(y_hbm.at[pl.ds(0, tm)], ybuf.at[slot], sem.at[slot]).wait()
    x = h_ref[...] + _from_slabs(ybuf.at[slot])
    xn = _rms(x, g_ref[...]).astype(BF16)
    gate = jax.nn.sigmoid(jnp.dot(xn, wg_ref[...], preferred_element_type=F32))
    proj = jnp.dot(p_ref[...].astype(BF16), wp_ref[...], preferred_element_type=F32)
    y = x + gate * proj
    if final:
        y = _rms(y, gf_ref[...])
    out_ref[...] = y


def _ple(h, y_sorted, pos, g, wg, p, wp, gf, final, tm=512):
    s, d = h.shape
    tm = min(tm, s)
    wg, wp = wg.astype(BF16), wp.astype(BF16)
    const = lambda a: pl.BlockSpec(a.shape, lambda i: (0, 0))
    n = s // tm
    pos3 = pos.reshape(n, 1, tm)
    return pl.pallas_call(
        functools.partial(_ple_kernel, final=final),
        out_shape=jax.ShapeDtypeStruct((s, d), F32),
        grid=(n,),
        in_specs=[pl.BlockSpec((None, 1, tm), lambda i: (i, 0, 0), memory_space=pltpu.SMEM),
                  pl.BlockSpec((None, 1, tm), lambda i: (jnp.minimum(i + 1, n - 1), 0, 0), memory_space=pltpu.SMEM),
                  pl.BlockSpec((tm, d), lambda i: (i, 0)), pl.BlockSpec(memory_space=pl.ANY), const(g), const(wg),
                  pl.BlockSpec((tm, p.shape[1]), lambda i: (i, 0)), const(wp), const(gf)],
        out_specs=pl.BlockSpec((tm, d), lambda i: (i, 0)),
        scratch_shapes=[pltpu.VMEM((2, tm, SUBLANES, LANES), F32), pltpu.SemaphoreType.DMA((2,))],
        compiler_params=_cparams(("arbitrary",)),
        name="ple",
    )(pos3, pos3, h, y_sorted, g, wg, p, wp, gf)


PARTIAL_KPERM = ((8, 16), (0, 8), (16, 64))
AXIAL_KPERM = ((16, 32), (0, 16), (48, 64), (32, 48))
B_PAIRS = ((128, 1), (512, 4), (2048, 16))


def _partial_tables(s):
    inv = ROPE_THETA ** (-jnp.arange(0, ROT_DIM, 2, dtype=F32) / ROT_DIM)
    ang = jnp.arange(s).astype(F32)[:, None] * inv[None, :]
    return _rope_tables([ang], s, LOG2E * HEAD_DIM ** -0.5)


def _axial_tables(s):
    half = HEAD_DIM // 2
    inv = AXIAL_THETA ** (-jnp.arange(0, half, 2, dtype=F32) / half)
    t = jnp.arange(s)
    ang_r = (t // GRID_W).astype(F32)[:, None] * inv[None, :]
    ang_c = (t % GRID_W).astype(F32)[:, None] * inv[None, :]
    return _rope_tables([ang_r, ang_c], s, LOG2E * HEAD_DIM ** -0.5)


def _mixer_diff(h, g, w_in, w_out, lam_rows, subln, lam_init, tabs):
    d = h.shape[1]
    heads = d // (2 * HEAD_DIM)
    aw = 2 * heads * HEAD_DIM
    dv = 2 * HEAD_DIM
    q, kt, v = _project(h, g, w_in[:, :aw], w_in[:, aw:2 * aw], w_in[:, 2 * aw:], tabs["row_scaled"], tabs["t"],
                        hr=2 * heads, ht=2 * heads, hv=heads, dv=dv, nv=2 * dv,
                        shift=ROT_DIM // 2, tperm=PARTIAL_KPERM, qk_norm=False)
    o = _flash(q, kt, v, units=heads, q_per_unit=2, k_per_unit=2, v_per_unit=1,
               streams=((0, 0, 0), (1, 1, 0)), split=1, tq=1024, tk=2048, nv=2 * dv, dv=dv, out_w=dv,
               mode="diff", lam=lam_rows, subln=subln, lam_init=lam_init, name="flash_diff")
    return _outproj(h, o, w_out)


def _mixer_dilated(h, g, w_in, w_out, tabs):
    nh, hg = 12, 4
    bw = nh * HEAD_DIM
    q, kt, v = _project(h, g, w_in[:, :bw], w_in[:, bw:2 * bw], w_in[:, 2 * bw:], tabs["row_scaled"], tabs["t"],
                        hr=nh, ht=nh, hv=nh, dv=HEAD_DIM, nv=LANES,
                        shift=ROT_DIM // 2, tperm=PARTIAL_KPERM, qk_norm=False)
    os_, ls_ = [], []
    for gi, (win, dil) in enumerate(B_PAIRS):
        halfw = (win // (2 * dil)) * dil
        o, lse = _flash(q, kt, v, units=1, unit0=gi, q_per_unit=hg, k_per_unit=hg, v_per_unit=hg,
                        streams=tuple((j, j, j) for j in range(hg)), tq=256, tk=max(halfw, 256),
                        nv=LANES, dv=HEAD_DIM, out_w=hg * HEAD_DIM, band=(halfw, dil),
                        halo=-(-halfw // LANES) * LANES, mode="lse",
                        name=f"flash_dilated{gi}")
        os_.append(o)
        ls_.append(lse)
    return _outproj_groups(h, os_, ls_, w_out)


GQA_HEADS, GQA_KV_HEADS = 16, 4


def _mixer_window(h, g, w_in, w_out, sink, tabs):
    qd, kvd = GQA_HEADS * HEAD_DIM, GQA_KV_HEADS * HEAD_DIM
    grp = GQA_HEADS // GQA_KV_HEADS
    q, kt, v = _project(h, g, w_in[:, :qd], w_in[:, qd:qd + kvd], w_in[:, qd + kvd:], tabs["row_scaled"], tabs["t"],
                        hr=GQA_HEADS, ht=GQA_KV_HEADS, hv=GQA_KV_HEADS, dv=HEAD_DIM, nv=LANES,
                        shift=ROT_DIM // 2, tperm=PARTIAL_KPERM, qk_norm=False)
    o = _flash(q, kt, v, units=GQA_KV_HEADS, q_per_unit=grp, k_per_unit=1, v_per_unit=1,
               streams=tuple((j, 0, 0) for j in range(grp)), tq=256, tk=256, nv=LANES, dv=HEAD_DIM,
               out_w=grp * HEAD_DIM, band=(128, 1), halo=128, mode="gqa", sink=sink.reshape(GQA_KV_HEADS, grp),
               name="flash_window")
    return _outproj(h, o, w_out)


def _mixer_axial(h, g, w_in, w_out, q_norm, k_norm, tabs):
    qd, kvd = GQA_HEADS * HEAD_DIM, GQA_KV_HEADS * HEAD_DIM
    grp = GQA_HEADS // GQA_KV_HEADS
    qg = jnp.tile(q_norm, 2).reshape(1, LANES)
    kg = k_norm.reshape(HEAD_DIM, 1)
    q, kt, v = _project(h, g, w_in[:, :qd], w_in[:, qd:qd + kvd], w_in[:, qd + kvd:], tabs["row_scaled"], tabs["t"],
                        hr=GQA_HEADS, ht=GQA_KV_HEADS, hv=GQA_KV_HEADS, dv=HEAD_DIM, nv=LANES,
                        shift=HEAD_DIM // 4, tperm=AXIAL_KPERM, qk_norm=True, rg=qg, tg=kg)
    o = _flash(q, kt, v, units=GQA_KV_HEADS, q_per_unit=grp, k_per_unit=1, v_per_unit=1,
               streams=tuple((j, 0, 0) for j in range(grp)), split=8, tq=1024, tk=2048, nv=LANES, dv=HEAD_DIM,
               out_w=grp * HEAD_DIM, mode="gqa", name="flash_axial")
    return _outproj(h, o, w_out)


def kernel(x, p, norm_mix, norm_ffn, norm_ple, norm_final, a_w_in, a_w_out, a_lam_q1, a_lam_k1, a_lam_q2, a_lam_k2, a_subln, b_w_in, b_w_out, c_w_in, c_w_out, c_sink, d_w_in, d_w_out, d_q_norm, d_k_norm, moe_w_group, moe_b_group, moe_w_expert, moe_b_expert, moe_w_gate, moe_w_up, moe_w_down, ple_w_gate, ple_w_proj):
    bn, s, d = x.shape
    assert bn == 1
    depth = p.shape[0]
    h = x[0]
    ptabs = _partial_tables(s)
    atabs = _axial_tables(s)
    row = lambda a: a.reshape(1, -1)
    for i in range(depth):
        r, kind = divmod(i, 4)
        g = row(norm_mix[i])
        if kind == 0:
            lam_init = 0.8 - 0.6 * math.exp(-0.3 * i)
            lam_rows = jnp.stack([a_lam_q1[r], a_lam_k1[r], a_lam_q2[r], a_lam_k2[r]])
            h = _mixer_diff(h, g, a_w_in[r], a_w_out[r], lam_rows, row(a_subln[r]), lam_init, ptabs)
        elif kind == 1:
            h = _mixer_dilated(h, g, b_w_in[r], b_w_out[r], ptabs)
        elif kind == 2:
            h = _mixer_window(h, g, c_w_in[r], c_w_out[r], c_sink[r], ptabs)
        else:
            h = _mixer_axial(h, g, d_w_in[r], d_w_out[r], d_q_norm[r], d_k_norm[r], atabs)
        y_sorted, pos = _moe_sparse(h, row(norm_ffn[i]), moe_w_group[i], moe_b_group[i], moe_w_expert[i],
                                    moe_b_expert[i], moe_w_gate[i], moe_w_up[i], moe_w_down[i])
        h = _ple(h, y_sorted, pos, row(norm_ple[i]), ple_w_gate[i], p[i, 0], ple_w_proj[i], row(norm_final), final=(i == depth - 1))
    return h[None]
```

```python
import functools
import math

import jax
import jax.numpy as jnp
from jax import lax
from jax.experimental import pallas as pl
from jax.experimental.pallas import tpu as pltpu

F32 = jnp.float32
BF16 = jnp.bfloat16

HEAD_DIM = 64
LANES = 128
EPS = 1e-6
LOG2E = 1.4426950408889634
NEG_INF = -1e30
M_FLOOR = -1e29
ROPE_THETA = 500000.0
ROT_DIM = HEAD_DIM // 4
AXIAL_THETA = 10000.0
GRID_W = 64
MOE_GROUPS = 4
MOE_PER_GROUP = 4
MOE_EXPERTS = 16
VMEM_LIMIT = 56 * 1024 * 1024


def _cparams(sem):
    return pltpu.CompilerParams(dimension_semantics=sem, vmem_limit_bytes=VMEM_LIMIT)


def _rms(x, g):
    return x * lax.rsqrt(jnp.mean(x * x, axis=-1, keepdims=True) + EPS) * g


def _proj_kernel(h_ref, g_ref, wr_ref, wtt_ref, wv_ref, rc_ref, rs1_ref, rs2_ref, tc_ref, ts_ref,
                 rg_ref, tg_ref, r_ref, t_ref, v_ref, *, hr, ht, hv, nv, dv, shift, tperm, qk_norm):
    xn = _rms(h_ref[...], g_ref[...]).astype(BF16)
    rf = jnp.dot(xn, wr_ref[...], preferred_element_type=F32)
    rc, rs1, rs2 = rc_ref[...], rs1_ref[...], rs2_ref[...]
    low = lax.broadcasted_iota(jnp.int32, (rf.shape[0], LANES), 1) < HEAD_DIM
    for j in range(hr // 2):
        s = rf[:, j * LANES:(j + 1) * LANES]
        if qk_norm:
            sq = s * s
            ms = jnp.where(low, jnp.sum(jnp.where(low, sq, 0.0), axis=-1, keepdims=True),
                           jnp.sum(jnp.where(low, 0.0, sq), axis=-1, keepdims=True)) * (1.0 / HEAD_DIM)
            s = s * lax.rsqrt(ms + EPS) * rg_ref[...]
        r = s * rc + pltpu.roll(s, LANES - shift, 1) * rs1 + pltpu.roll(s, shift, 1) * rs2
        r_ref[2 * j] = r[:, :HEAD_DIM].astype(BF16)
        r_ref[2 * j + 1] = r[:, HEAD_DIM:].astype(BF16)
    nt = (((1,), (1,)), ((), ()))
    tf = lax.dot_general(wtt_ref[...], xn, nt, preferred_element_type=F32)
    tc, ts = tc_ref[...], ts_ref[...]
    for h in range(ht):
        s = tf[h * HEAD_DIM:(h + 1) * HEAD_DIM, :]
        if qk_norm:
            ms = jnp.sum(s * s, axis=0, keepdims=True) * (1.0 / HEAD_DIM)
            s = s * lax.rsqrt(ms + EPS) * tg_ref[...]
        partner = jnp.concatenate([s[a:b] for a, b in tperm], axis=0)
        t_ref[h] = (s * tc + partner * ts).astype(BF16)
    vf = jnp.dot(xn, wv_ref[...], preferred_element_type=F32).astype(BF16)
    ones_col = (lax.broadcasted_iota(jnp.int32, (vf.shape[0], nv - dv), 1) == 0).astype(BF16)
    for h in range(hv):
        v_ref[:, h * nv:h * nv + dv] = vf[:, h * dv:(h + 1) * dv]
        v_ref[:, h * nv + dv:(h + 1) * nv] = ones_col


def _project(h, g, wq, wk, wv, row_tabs, t_tabs, *, hr, ht, hv, dv, nv, shift, tperm, qk_norm, rg=None, tg=None,
             tm=512):
    s, d = h.shape
    tm = min(tm, s)
    assert hr % 2 == 0
    wr = wq.astype(BF16)
    wtt = wk.T.astype(BF16)
    wv_p = wv.astype(BF16)
    rc, rs1, rs2 = row_tabs
    tc, ts = t_tabs
    if rg is None:
        rg = jnp.ones((1, LANES), F32)
        tg = jnp.ones((HEAD_DIM, 1), F32)
    full = lambda a: pl.BlockSpec(a.shape, lambda i: (0,) * a.ndim)
    kern = functools.partial(_proj_kernel, hr=hr, ht=ht, hv=hv, nv=nv, dv=dv, shift=shift, tperm=tperm,
                             qk_norm=qk_norm)
    return pl.pallas_call(
        kern,
        out_shape=(jax.ShapeDtypeStruct((hr, s, HEAD_DIM), BF16),
                   jax.ShapeDtypeStruct((ht, HEAD_DIM, s), BF16),
                   jax.ShapeDtypeStruct((s, hv * nv), BF16)),
        grid=(s // tm,),
        in_specs=[pl.BlockSpec((tm, d), lambda i: (i, 0)), full(g), full(wr), full(wtt), full(wv_p),
                  pl.BlockSpec((tm, LANES), lambda i: (i, 0)), pl.BlockSpec((tm, LANES), lambda i: (i, 0)),
                  pl.BlockSpec((tm, LANES), lambda i: (i, 0)),
                  pl.BlockSpec((HEAD_DIM, tm), lambda i: (0, i)), pl.BlockSpec((HEAD_DIM, tm), lambda i: (0, i)),
                  full(rg), full(tg)],
        out_specs=(pl.BlockSpec((hr, tm, HEAD_DIM), lambda i: (0, i, 0)),
                   pl.BlockSpec((ht, HEAD_DIM, tm), lambda i: (0, 0, i)),
                   pl.BlockSpec((tm, hv * nv), lambda i: (i, 0))),
        compiler_params=_cparams(("parallel",)),
        name="proj",
    )(h, g, wr, wtt, wv_p, rc, rs1, rs2, tc, ts, rg, tg)


def _rope_tables(ang_list, s, scale):
    cs, sn1, sn2, ksn = [], [], [], []
    used = 0
    for ang in ang_list:
        c, sn = jnp.cos(ang), jnp.sin(ang)
        z = jnp.zeros_like(sn)
        cs += [c, c]
        sn1 += [-sn, z]
        sn2 += [z, sn]
        ksn += [-sn, sn]
        used += 2 * ang.shape[1]
    rest = HEAD_DIM - used
    ones, zeros = jnp.ones((s, rest), F32), jnp.zeros((s, rest), F32)
    c64 = jnp.concatenate(cs + [ones], axis=1)
    twice = lambda parts: jnp.concatenate(parts + parts, axis=1)
    row = (twice([c64]), twice(sn1 + [zeros]), twice(sn2 + [zeros]))
    tr = (c64.T, jnp.concatenate(ksn + [zeros], axis=1).T)
    return {"row_scaled": tuple(t * scale for t in row), "t": tr}


def _lane_tile(x, width):
    reps = width // LANES
    return x if reps == 1 else jnp.concatenate([x] * reps, axis=1)


def _flash_kernel(*refs, streams, split, tq, tk, nv, dv, band, halo, seq, mode, lam_init, has_sink):
    it = iter(refs)
    q_ref, kt_ref, v_ref = next(it), next(it), next(it)
    sink_ref = next(it) if has_sink else None
    lam_ref = subln_ref = None
    if mode == "diff":
        lam_ref, subln_ref = next(it), next(it)
    o_ref = next(it)
    lse_ref = next(it) if mode == "lse" else None
    m_scr, acc_scr = next(it), next(it)

    qb, kb = pl.program_id(1), pl.program_id(2)
    nsteps = pl.num_programs(2)
    cr = tq // split

    @pl.when(kb == 0)
    def _init():
        for si, (qi, _, _) in enumerate(streams):
            if has_sink:
                sink2 = jnp.full((tq, LANES), LOG2E, F32) * sink_ref[pl.program_id(0), qi]
                m0 = jnp.maximum(sink2, M_FLOOR)
                m_scr[si] = m0
                lane = lax.broadcasted_iota(jnp.int32, (tq, nv), 1)
                acc_scr[si] = jnp.where(lane == dv, _lane_tile(jnp.exp2(sink2 - m0), nv), 0.0)
            else:
                m_scr[si] = jnp.full((tq, LANES), M_FLOOR, F32)
                acc_scr[si] = jnp.zeros((tq, nv), F32)

    def _step():
        chunks = [(si, c * cr, qi, ki, vi) for si, (qi, ki, vi) in enumerate(streams) for c in range(split)]
        scores = [jnp.dot(q_ref[qi, r0:r0 + cr, :], kt_ref[ki], preferred_element_type=F32)
                  for _, r0, qi, ki, _ in chunks]
        for sc, (si, r0, _, _, vi) in zip(scores, chunks):
            if band is not None:
                halfw, dil, off = band
                row = lax.broadcasted_iota(jnp.int32, (cr, tk), 0) + r0
                col = lax.broadcasted_iota(jnp.int32, (cr, tk), 1)
                diff = col - row + kshift
                valid = jnp.abs(diff) <= halfw
                if dil > 1:
                    valid = valid & ((diff & (dil - 1)) == 0)
                sc = jnp.where(valid, sc, NEG_INF)
            m_old = m_scr[si, r0:r0 + cr, :]
            m_new = jnp.maximum(m_old, jnp.broadcast_to(jnp.max(sc, axis=1, keepdims=True), (cr, LANES)))
            p = jnp.exp2(sc - _lane_tile(m_new, tk))
            alpha = jnp.exp2(m_old - m_new)
            pv = jnp.dot(p.astype(BF16), v_ref[:, vi * nv:(vi + 1) * nv], preferred_element_type=F32)
            acc_scr[si, r0:r0 + cr, :] = acc_scr[si, r0:r0 + cr, :] * _lane_tile(alpha, nv) + pv
            m_scr[si, r0:r0 + cr, :] = m_new

    if band is None:
        _step()
    elif halo is not None:
        kshift = jnp.clip(qb * tq - halo, 0, seq - tk) - qb * tq
        _step()
    else:
        _, _, off = band
        kshift = (kb - off) * tk
        kabs = qb * (tq // tk) + kb - off
        pl.when((kabs >= 0) & (kabs < seq // tk))(_step)

    @pl.when(kb == nsteps - 1)
    def _fin():
        if mode == "diff":
            a0, a1 = acc_scr[0], acc_scr[1]
            lam_rows = lam_ref[...]
            lam = (jnp.exp(jnp.sum(lam_rows[0:1] * lam_rows[1:2], axis=1, keepdims=True))
                   - jnp.exp(jnp.sum(lam_rows[2:3] * lam_rows[3:4], axis=1, keepdims=True)) + lam_init)
            o = a0[:, :dv] / a0[:, dv:dv + 1] - lam * (a1[:, :dv] / a1[:, dv:dv + 1])
            o = _rms(o, subln_ref[...]) * (1.0 - lam_init)
            o_ref[...] = o.astype(o_ref.dtype)
        else:
            for si in range(len(streams)):
                a = acc_scr[si]
                l = a[:, dv:dv + 1]
                o_ref[:, si * dv:(si + 1) * dv] = (a[:, :dv] / l).astype(o_ref.dtype)
                if mode == "lse":
                    lse_ref[:, si * dv:(si + 1) * dv] = m_scr[si][:, :dv] + jnp.log2(l)


def _flash(q, kt, v, *, units, q_per_unit, k_per_unit, v_per_unit, streams, tq, tk, nv, dv, out_w,
           split=1, unit0=0, band=None, halo=None, mode="gqa", sink=None, lam=None, subln=None, lam_init=0.0,
           name="flash"):
    s = q.shape[1]
    tq, tk = min(tq, s), min(tk, s)
    if halo is not None and tq + 2 * halo > s:
        halo, tq = None, max(tq, tk)
    if halo is not None:
        tk = tq + 2 * halo
    assert s % tq == 0 and (halo is not None or (s % tk == 0 and (band is None or tq % tk == 0)))
    n_kblocks = s // tk
    kspec = vspec = None
    if band is None:
        nsteps = n_kblocks
        kmap = lambda u, i, j: (u + unit0, 0, j)
        vmap = lambda u, i, j: (j, u + unit0)
    elif halo is not None:
        assert tq % LANES == 0 and halo % LANES == 0
        nsteps = 1
        band = (band[0], band[1], 0)
        koff = lambda i: jnp.clip(i * (tq // LANES) - halo // LANES, 0, (s - tk) // LANES) * LANES
        kspec = pl.BlockSpec((pl.Element(k_per_unit), pl.Element(HEAD_DIM), pl.Element(tk)),
                             lambda u, i, j: ((u + unit0) * k_per_unit, 0, koff(i)))
        vspec = pl.BlockSpec((pl.Element(tk), pl.Element(v_per_unit * nv)),
                             lambda u, i, j: (koff(i), (u + unit0) * v_per_unit * nv))
    else:
        halfw = band[0]
        off = -(-halfw // tk)
        nsteps = tq // tk + 2 * off
        band = (band[0], band[1], off)
        kidx = lambda i, j: jnp.clip(i * (tq // tk) + j - off, 0, n_kblocks - 1)
        kmap = lambda u, i, j: (u + unit0, 0, kidx(i, j))
        vmap = lambda u, i, j: (kidx(i, j), u + unit0)
    in_specs = [pl.BlockSpec((q_per_unit, tq, HEAD_DIM), lambda u, i, j: (u + unit0, i, 0)),
                kspec or pl.BlockSpec((k_per_unit, HEAD_DIM, tk), kmap),
                vspec or pl.BlockSpec((tk, v_per_unit * nv), vmap)]
    args = [q, kt, v]
    if sink is not None:
        in_specs.append(pl.BlockSpec(memory_space=pltpu.SMEM))
        args.append(sink)
    if mode == "diff":
        in_specs += [pl.BlockSpec(lam.shape, lambda u, i, j: (0, 0)), pl.BlockSpec(subln.shape, lambda u, i, j: (0, 0))]
        args += [lam, subln]
    out_shape = [jax.ShapeDtypeStruct((s, units * out_w), BF16)]
    out_specs = [pl.BlockSpec((tq, out_w), lambda u, i, j: (i, u))]
    if mode == "lse":
        out_shape.append(jax.ShapeDtypeStruct((s, units * out_w), F32))
        out_specs.append(pl.BlockSpec((tq, out_w), lambda u, i, j: (i, u)))
    assert tq % split == 0
    kern = functools.partial(_flash_kernel, streams=streams, split=split, tq=tq, tk=tk, nv=nv, dv=dv, band=band,
                             halo=halo, seq=s, mode=mode, lam_init=lam_init, has_sink=sink is not None)
    res = pl.pallas_call(
        kern,
        out_shape=tuple(out_shape),
        grid=(units, s // tq, nsteps),
        in_specs=in_specs,
        out_specs=tuple(out_specs),
        scratch_shapes=[pltpu.VMEM((len(streams), tq, LANES), F32), pltpu.VMEM((len(streams), tq, nv), F32)],
        compiler_params=_cparams(("parallel", "parallel", "arbitrary")),
        name=name,
    )(*args)
    return res if mode == "lse" else res[0]


def _outproj_kernel(h_ref, o_ref, w_ref, out_ref):
    out_ref[...] = h_ref[...] + jnp.dot(o_ref[...], w_ref[...], preferred_element_type=F32)


def _outproj(h, o, w, tm=512):
    s, d = h.shape
    tm = min(tm, s)
    w = w.astype(BF16)
    return pl.pallas_call(
        _outproj_kernel,
        out_shape=jax.ShapeDtypeStruct((s, d), F32),
        grid=(s // tm,),
        in_specs=[pl.BlockSpec((tm, d), lambda i: (i, 0)), pl.BlockSpec((tm, o.shape[1]), lambda i: (i, 0)),
                  pl.BlockSpec(w.shape, lambda i: (0, 0))],
        out_specs=pl.BlockSpec((tm, d), lambda i: (i, 0)),
        compiler_params=_cparams(("parallel",)),
        name="outproj",
    )(h, o, w)


def _outproj_groups_kernel(h_ref, o0_ref, o1_ref, o2_ref, l0_ref, l1_ref, l2_ref, w_ref, out_ref):
    l0, l1, l2 = l0_ref[...], l1_ref[...], l2_ref[...]
    mx = jnp.maximum(jnp.maximum(l0, l1), l2)
    e0, e1, e2 = jnp.exp2(l0 - mx), jnp.exp2(l1 - mx), jnp.exp2(l2 - mx)
    tot = e0 + e1 + e2
    acc = h_ref[...]
    gw = o0_ref.shape[1]
    for g, (o_ref, e) in enumerate(((o0_ref, e0), (o1_ref, e1), (o2_ref, e2))):
        og = (o_ref[...].astype(F32) * (e / tot)).astype(BF16)
        acc = acc + jnp.dot(og, w_ref[g * gw:(g + 1) * gw, :], preferred_element_type=F32)
    out_ref[...] = acc


def _outproj_groups(h, os_, ls_, w, tm=512):
    s, d = h.shape
    tm = min(tm, s)
    w = w.astype(BF16)
    gw = os_[0].shape[1]
    row = lambda width: pl.BlockSpec((tm, width), lambda i: (i, 0))
    return pl.pallas_call(
        _outproj_groups_kernel,
        out_shape=jax.ShapeDtypeStruct((s, d), F32),
        grid=(s // tm,),
        in_specs=[row(d)] + [row(gw)] * 6 + [pl.BlockSpec(w.shape, lambda i: (0, 0))],
        out_specs=row(d),
        compiler_params=_cparams(("parallel",)),
        name="outproj_groups",
    )(h, *os_, *ls_, w)


def _route(logits):
    lane = lax.broadcasted_iota(jnp.int32, logits.shape, 1).astype(F32)
    big = 1e6
    gl = jnp.where(lane < MOE_GROUPS, logits, NEG_INF)
    gmax = jnp.max(gl, axis=1, keepdims=True)
    gidx = jnp.min(jnp.where(gl == gmax, lane, big), axis=1, keepdims=True)
    gw = 1.0 / jnp.sum(jnp.exp(gl - gmax), axis=1, keepdims=True)
    lo = MOE_GROUPS + gidx * MOE_PER_GROUP
    el = jnp.where((lane >= lo) & (lane < lo + MOE_PER_GROUP), logits, NEG_INF)
    v1 = jnp.max(el, axis=1, keepdims=True)
    i1 = jnp.min(jnp.where(el == v1, lane, big), axis=1, keepdims=True)
    el2 = jnp.where(lane == i1, NEG_INF, el)
    v2 = jnp.max(el2, axis=1, keepdims=True)
    i2 = jnp.min(jnp.where(el2 == v2, lane, big), axis=1, keepdims=True)
    e2 = jnp.exp(v2 - v1)
    w1 = gw / (1.0 + e2)
    w2 = w1 * e2
    return jnp.where(lane == i1, w1, 0.0) + jnp.where(lane == i2, w2, 0.0), gidx


MOE_ROW_TILE = 512
INFO_GROUP_LANE = 0
INFO_RANK_LANE = 1
SUBLANES = 8


def _to_slabs(ref, x):
    for c in range(SUBLANES):
        ref[:, c, :] = x[:, c * LANES:(c + 1) * LANES]


def _from_slabs(ref):
    return jnp.concatenate([ref[:, c, :] for c in range(SUBLANES)], axis=1)


def _moe_route_kernel(h_ref, g_ref, wr_ref, br_ref, tri_ref, x3_ref, info_ref, cnt_ref, run_scr):
    @pl.when(pl.program_id(0) == 0)
    def _init():
        run_scr[...] = jnp.zeros_like(run_scr)

    xn = _rms(h_ref[...], g_ref[...])
    w = wr_ref[...]
    xh, wh = xn.astype(BF16), w.astype(BF16)
    xl, wl = (xn - xh.astype(F32)).astype(BF16), (w - wh.astype(F32)).astype(BF16)
    logits = (jnp.dot(xh, wh, preferred_element_type=F32) + jnp.dot(xh, wl, preferred_element_type=F32)
              + jnp.dot(xl, wh, preferred_element_type=F32)) + br_ref[...]
    comb, gidx = _route(logits)
    lane = lax.broadcasted_iota(jnp.int32, comb.shape, 1).astype(F32)
    onehot = lane == gidx
    before = jnp.dot(tri_ref[...], onehot.astype(BF16), preferred_element_type=F32)
    run = run_scr[...]
    rank = jnp.sum(jnp.where(onehot, before + run, 0.0), axis=1, keepdims=True)
    run = run + jnp.sum(onehot.astype(F32), axis=0, keepdims=True)
    run_scr[...] = run
    cnt_ref[...] = run
    _to_slabs(x3_ref, xn)
    info_ref[...] = jnp.where(lane == INFO_GROUP_LANE, gidx, jnp.where(lane == INFO_RANK_LANE, rank, comb))


def _moe_dispatch_kernel(pos_ref, x3_ref, info_ref, xs_in_ref, infos_in_ref, xs_ref, infos_ref, sem):
    del xs_in_ref, infos_in_ref
    tm = x3_ref.shape[0]

    def issue(i, c):
        for prio in range(2):
            t = 2 * i + prio
            p = pos_ref[0, t]
            pltpu.make_async_copy(x3_ref.at[pl.ds(t, 1)], xs_ref.at[pl.ds(p, 1)], sem).start(priority=prio)
            pltpu.make_async_copy(info_ref.at[pl.ds(t, 1)], infos_ref.at[pl.ds(p, 1)], sem).start(priority=prio)
        return c

    lax.fori_loop(0, tm // 2, issue, 0, unroll=4)
    pltpu.make_async_copy(x3_ref, xs_ref.at[pl.ds(0, tm)], sem).wait()
    pltpu.make_async_copy(info_ref, infos_ref.at[pl.ds(0, tm)], sem).wait()


def _moe_expert_kernel(tg_ref, nv_ref, xs_ref, infos_ref, wg_ref, wu_ref, wd_ref, y_ref):
    j = pl.program_id(0)

    @pl.when(j < nv_ref[0])
    def _live():
        x = _from_slabs(xs_ref).astype(BF16)
        info = infos_ref[...]
        lane = lax.broadcasted_iota(jnp.int32, info.shape, 1)
        first = tg_ref[j] * MOE_PER_GROUP + MOE_GROUPS
        acc = None
        for e in range(MOE_PER_GROUP):
            we = jnp.sum(jnp.where(lane == first + e, info, 0.0), axis=1, keepdims=True)
            gate = jnp.dot(x, wg_ref[e], preferred_element_type=F32)
            up = jnp.dot(x, wu_ref[e], preferred_element_type=F32)
            act = gate * jax.nn.sigmoid(gate) * up * we
            part = jnp.dot(act.astype(BF16), wd_ref[e], preferred_element_type=F32)
            acc = part if acc is None else acc + part
        _to_slabs(y_ref, acc)

    @pl.when(j >= nv_ref[0])
    def _dead():
        y_ref[...] = jnp.zeros_like(y_ref)


def _moe_sparse(h, g, w_group, b_group, w_expert, b_expert, w_gate, w_up, w_down, tm=1024, tmd=512):
    s, d = h.shape
    assert d == SUBLANES * LANES
    tm, tmd, tb = min(tm, s), min(tmd, s), min(MOE_ROW_TILE, s)
    ff = w_gate.shape[2]
    nr = MOE_GROUPS + MOE_EXPERTS
    wr = jnp.pad(jnp.concatenate([w_group, w_expert], axis=1), ((0, 0), (0, LANES - nr)))
    br = jnp.pad(jnp.concatenate([b_group, b_expert]), (0, LANES - nr)).reshape(1, LANES)
    tri = (lax.broadcasted_iota(jnp.int32, (tm, tm), 0) > lax.broadcasted_iota(jnp.int32, (tm, tm), 1)).astype(BF16)
    const = lambda a: pl.BlockSpec(a.shape, lambda i: (0, 0))
    slab = lambda rows: pl.BlockSpec((rows, SUBLANES, LANES), lambda i: (i, 0, 0))
    x3, info, cnt = pl.pallas_call(
        _moe_route_kernel,
        out_shape=(jax.ShapeDtypeStruct((s, SUBLANES, LANES), F32), jax.ShapeDtypeStruct((s, LANES), F32),
                   jax.ShapeDtypeStruct((1, LANES), F32)),
        grid=(s // tm,),
        in_specs=[pl.BlockSpec((tm, d), lambda i: (i, 0)), const(g), const(wr), const(br), const(tri)],
        out_specs=(slab(tm), pl.BlockSpec((tm, LANES), lambda i: (i, 0)), pl.BlockSpec((1, LANES), lambda i: (0, 0))),
        scratch_shapes=[pltpu.VMEM((1, LANES), F32)],
        compiler_params=_cparams(("arbitrary",)),
        name="moe_route",
    )(h, g, wr, br, tri)

    counts = cnt[0, :MOE_GROUPS].astype(jnp.int32)
    padded = ((counts + tb - 1) // tb) * tb
    ends = jnp.cumsum(padded)
    starts = ends - padded
    tok_group = info[:, INFO_GROUP_LANE].astype(jnp.int32)
    tok_rank = info[:, INFO_RANK_LANE].astype(jnp.int32)
    pos = starts[tok_group] + tok_rank
    n_tiles = s // tb + MOE_GROUPS
    p_rows = n_tiles * tb
    tile_group = jnp.minimum(jnp.sum((jnp.arange(n_tiles) * tb)[:, None] >= ends[None, :], axis=1), MOE_GROUPS - 1)
    n_valid = (ends[-1] // tb).reshape(1)

    anyspec = pl.BlockSpec(memory_space=pl.ANY)
    xs, infos = pl.pallas_call(
        _moe_dispatch_kernel,
        out_shape=(jax.ShapeDtypeStruct((p_rows, SUBLANES, LANES), F32), jax.ShapeDtypeStruct((p_rows, LANES), F32)),
        grid=(s // tmd,),
        in_specs=[pl.BlockSpec((None, 1, tmd), lambda i: (i, 0, 0), memory_space=pltpu.SMEM),
                  slab(tmd), pl.BlockSpec((tmd, LANES), lambda i: (i, 0)), anyspec, anyspec],
        out_specs=(anyspec, anyspec),
        scratch_shapes=[pltpu.SemaphoreType.DMA],
        input_output_aliases={3: 0, 4: 1},
        compiler_params=_cparams(("arbitrary",)),
        name="moe_dispatch",
    )(pos.reshape(s // tmd, 1, tmd), x3, info, jnp.zeros((p_rows, SUBLANES, LANES), F32),
      jnp.zeros((p_rows, LANES), F32))

    wg, wu, wd = w_gate.astype(BF16), w_up.astype(BF16), w_down.astype(BF16)
    wsel = lambda j, tg, nv: (jnp.where(j < nv[0], tg[j], MOE_GROUPS - 1), 0, 0)
    y = pl.pallas_call(
        _moe_expert_kernel,
        out_shape=jax.ShapeDtypeStruct((p_rows, SUBLANES, LANES), F32),
        grid_spec=pltpu.PrefetchScalarGridSpec(
            num_scalar_prefetch=2,
            grid=(n_tiles,),
            in_specs=[pl.BlockSpec((tb, SUBLANES, LANES), lambda j, tg, nv: (j, 0, 0)),
                      pl.BlockSpec((tb, LANES), lambda j, tg, nv: (j, 0)),
                      pl.BlockSpec((MOE_PER_GROUP, d, ff), wsel), pl.BlockSpec((MOE_PER_GROUP, d, ff), wsel),
                      pl.BlockSpec((MOE_PER_GROUP, ff, d), wsel)],
            out_specs=pl.BlockSpec((tb, SUBLANES, LANES), lambda j, tg, nv: (j, 0, 0))),
        compiler_params=_cparams(("arbitrary",)),
        name="moe_expert",
    )(tile_group.astype(jnp.int32), n_valid.astype(jnp.int32), xs, infos, wg, wu, wd)
    return y, pos


def _ple_kernel(pos_ref, posn_ref, h_ref, y_hbm, g_ref, wg_ref, p_ref, wp_ref, gf_ref, out_ref, ybuf, sem, *, final):
    tm = h_ref.shape[0]
    i, n = pl.program_id(0), pl.num_programs(0)
    slot = lax.rem(i, 2)

    def gather(idx_ref, dst_slot):
        def issue(i, c):
            for prio in range(2):
                t = 2 * i + prio
                pltpu.make_async_copy(y_hbm.at[pl.ds(idx_ref[0, t], 1)], ybuf.at[dst_slot, pl.ds(t, 1)],
                                      sem.at[dst_slot]).start(priority=prio)
            return c
        lax.fori_loop(0, tm // 2, issue, 0, unroll=4)

    @pl.when(i == 0)
    def _first():
        gather(pos_ref, slot)

    @pl.when(i + 1 < n)
    def _next():
        gather(posn_ref, 1 - slot)

    pltpu.make_async_copy(y_hbm.at[pl.ds(0, tm)], ybuf.at[slot], sem.at[slot]).wait()
    x = h_ref[...] + _from_slabs(ybuf.at[slot])
    xn = _rms(x, g_ref[...]).astype(BF16)
    gate = jax.nn.sigmoid(jnp.dot(xn, wg_ref[...], preferred_element_type=F32))
    proj = jnp.dot(p_ref[...].astype(BF16), wp_ref[...], preferred_element_type=F32)
    y = x + gate * proj
    if final:
        y = _rms(y, gf_ref[...])
    out_ref[...] = y


def _ple(h, y_sorted, pos, g, wg, p, wp, gf, final, tm=512):
    s, d = h.shape
    tm = min(tm, s)
    wg, wp = wg.astype(BF16), wp.astype(BF16)
    const = lambda a: pl.BlockSpec(a.shape, lambda i: (0, 0))
    n = s // tm
    pos3 = pos.reshape(n, 1, tm)
    return pl.pallas_call(
        functools.partial(_ple_kernel, final=final),
        out_shape=jax.ShapeDtypeStruct((s, d), F32),
        grid=(n,),
        in_specs=[pl.BlockSpec((None, 1, tm), lambda i: (i, 0, 0), memory_space=pltpu.SMEM),
                  pl.BlockSpec((None, 1, tm), lambda i: (jnp.minimum(i + 1, n - 1), 0, 0), memory_space=pltpu.SMEM),
                  pl.BlockSpec((tm, d), lambda i: (i, 0)), pl.BlockSpec(memory_space=pl.ANY), const(g), const(wg),
                  pl.BlockSpec((tm, p.shape[1]), lambda i: (i, 0)), const(wp), const(gf)],
        out_specs=pl.BlockSpec((tm, d), lambda i: (i, 0)),
        scratch_shapes=[pltpu.VMEM((2, tm, SUBLANES, LANES), F32), pltpu.SemaphoreType.DMA((2,))],
        compiler_params=_cparams(("arbitrary",)),
        name="ple",
    )(pos3, pos3, h, y_sorted, g, wg, p, wp, gf)


PARTIAL_KPERM = ((8, 16), (0, 8), (16, 64))
AXIAL_KPERM = ((16, 32), (0, 16), (48, 64), (32, 48))
B_PAIRS = ((128, 1), (512, 4), (2048, 16))


def _partial_tables(s):
    inv = ROPE_THETA ** (-jnp.arange(0, ROT_DIM, 2, dtype=F32) / ROT_DIM)
    ang = jnp.arange(s).astype(F32)[:, None] * inv[None, :]
    return _rope_tables([ang], s, LOG2E * HEAD_DIM ** -0.5)


def _axial_tables(s):
    half = HEAD_DIM // 2
    inv = AXIAL_THETA ** (-jnp.arange(0, half, 2, dtype=F32) / half)
    t = jnp.arange(s)
    ang_r = (t // GRID_W).astype(F32)[:, None] * inv[None, :]
    ang_c = (t % GRID_W).astype(F32)[:, None] * inv[None, :]
    return _rope_tables([ang_r, ang_c], s, LOG2E * HEAD_DIM ** -0.5)


def _mixer_diff(h, g, w_in, w_out, lam_rows, subln, lam_init, tabs):
    d = h.shape[1]
    heads = d // (2 * HEAD_DIM)
    aw = 2 * heads * HEAD_DIM
    dv = 2 * HEAD_DIM
    q, kt, v = _project(h, g, w_in[:, :aw], w_in[:, aw:2 * aw], w_in[:, 2 * aw:], tabs["row_scaled"], tabs["t"],
                        hr=2 * heads, ht=2 * heads, hv=heads, dv=dv, nv=2 * dv,
                        shift=ROT_DIM // 2, tperm=PARTIAL_KPERM, qk_norm=False)
    o = _flash(q, kt, v, units=heads, q_per_unit=2, k_per_unit=2, v_per_unit=1,
               streams=((0, 0, 0), (1, 1, 0)), split=1, tq=1024, tk=2048, nv=2 * dv, dv=dv, out_w=dv,
               mode="diff", lam=lam_rows, subln=subln, lam_init=lam_init, name="flash_diff")
    return _outproj(h, o, w_out)


def _mixer_dilated(h, g, w_in, w_out, tabs):
    nh, hg = 12, 4
    bw = nh * HEAD_DIM
    q, kt, v = _project(h, g, w_in[:, :bw], w_in[:, bw:2 * bw], w_in[:, 2 * bw:], tabs["row_scaled"], tabs["t"],
                        hr=nh, ht=nh, hv=nh, dv=HEAD_DIM, nv=LANES,
                        shift=ROT_DIM // 2, tperm=PARTIAL_KPERM, qk_norm=False)
    os_, ls_ = [], []
    for gi, (win, dil) in enumerate(B_PAIRS):
        halfw = (win // (2 * dil)) * dil
        o, lse = _flash(q, kt, v, units=1, unit0=gi, q_per_unit=hg, k_per_unit=hg, v_per_unit=hg,
                        streams=tuple((j, j, j) for j in range(hg)), tq=256, tk=max(halfw, 256),
                        nv=LANES, dv=HEAD_DIM, out_w=hg * HEAD_DIM, band=(halfw, dil),
                        halo=-(-halfw // LANES) * LANES, mode="lse",
                        name=f"flash_dilated{gi}")
        os_.append(o)
        ls_.append(lse)
    return _outproj_groups(h, os_, ls_, w_out)


GQA_HEADS, GQA_KV_HEADS = 16, 4


def _mixer_window(h, g, w_in, w_out, sink, tabs):
    qd, kvd = GQA_HEADS * HEAD_DIM, GQA_KV_HEADS * HEAD_DIM
    grp = GQA_HEADS // GQA_KV_HEADS
    q, kt, v = _project(h, g, w_in[:, :qd], w_in[:, qd:qd + kvd], w_in[:, qd + kvd:], tabs["row_scaled"], tabs["t"],
                        hr=GQA_HEADS, ht=GQA_KV_HEADS, hv=GQA_KV_HEADS, dv=HEAD_DIM, nv=LANES,
                        shift=ROT_DIM // 2, tperm=PARTIAL_KPERM, qk_norm=False)
    o = _flash(q, kt, v, units=GQA_KV_HEADS, q_per_unit=grp, k_per_unit=1, v_per_unit=1,
               streams=tuple((j, 0, 0) for j in range(grp)), tq=256, tk=256, nv=LANES, dv=HEAD_DIM,
               out_w=grp * HEAD_DIM, band=(128, 1), halo=128, mode="gqa", sink=sink.reshape(GQA_KV_HEADS, grp),
               name="flash_window")
    return _outproj(h, o, w_out)


def _mixer_axial(h, g, w_in, w_out, q_norm, k_norm, tabs):
    qd, kvd = GQA_HEADS * HEAD_DIM, GQA_KV_HEADS * HEAD_DIM
    grp = GQA_HEADS // GQA_KV_HEADS
    qg = jnp.tile(q_norm, 2).reshape(1, LANES)
    kg = k_norm.reshape(HEAD_DIM, 1)
    q, kt, v = _project(h, g, w_in[:, :qd], w_in[:, qd:qd + kvd], w_in[:, qd + kvd:], tabs["row_scaled"], tabs["t"],
                        hr=GQA_HEADS, ht=GQA_KV_HEADS, hv=GQA_KV_HEADS, dv=HEAD_DIM, nv=LANES,
                        shift=HEAD_DIM // 4, tperm=AXIAL_KPERM, qk_norm=True, rg=qg, tg=kg)
    o = _flash(q, kt, v, units=GQA_KV_HEADS, q_per_unit=grp, k_per_unit=1, v_per_unit=1,
               streams=tuple((j, 0, 0) for j in range(grp)), split=8, tq=1024, tk=2048, nv=LANES, dv=HEAD_DIM,
               out_w=grp * HEAD_DIM, mode="gqa", name="flash_axial")
    return _outproj(h, o, w_out)


def kernel(x, p, norm_mix, norm_ffn, norm_ple, norm_final, a_w_in, a_w_out, a_lam_q1, a_lam_k1, a_lam_q2, a_lam_k2, a_subln, b_w_in, b_w_out, c_w_in, c_w_out, c_sink, d_w_in, d_w_out, d_q_norm, d_k_norm, moe_w_group, moe_b_group, moe_w_expert, moe_b_expert, moe_w_gate, moe_w_up, moe_w_down, ple_w_gate, ple_w_proj):
    bn, s, d = x.shape
    assert bn == 1
    depth = p.shape[0]
    h = x[0]
    ptabs = _partial_tables(s)
    atabs = _axial_tables(s)
    row = lambda a: a.reshape(1, -1)
    for i in range(depth):
        r, kind = divmod(i, 4)
        g = row(norm_mix[i])
        if kind == 0:
            lam_init = 0.8 - 0.6 * math.exp(-0.3 * i)
            lam_rows = jnp.stack([a_lam_q1[r], a_lam_k1[r], a_lam_q2[r], a_lam_k2[r]])
            h = _mixer_diff(h, g, a_w_in[r], a_w_out[r], lam_rows, row(a_subln[r]), lam_init, ptabs)
        elif kind == 1:
            h = _mixer_dilated(h, g, b_w_in[r], b_w_out[r], ptabs)
        elif kind == 2:
            h = _mixer_window(h, g, c_w_in[r], c_w_out[r], c_sink[r], ptabs)
        else:
            h = _mixer_axial(h, g, d_w_in[r], d_w_out[r], d_q_norm[r], d_k_norm[r], atabs)
        y_sorted, pos = _moe_sparse(h, row(norm_ffn[i]), moe_w_group[i], moe_b_group[i], moe_w_expert[i],
                                    moe_b_expert[i], moe_w_gate[i], moe_w_up[i], moe_w_down[i])
        h = _ple(h, y_sorted, pos, row(norm_ple[i]), ple_w_gate[i], p[i, 0], ple_w_proj[i], row(norm_final), final=(i == depth - 1))
    return h[None]
```
